```python
import math
import jax, jax.numpy as jnp
from jax import lax
import numpy as np

D_MODEL = 1024
BATCH = 8
SEQ = 4096
DEPTH = 2

CHUNK = 64

HGRN_HEADS = 4
HGRN_DK = 128
HGRN_DV = 128
HGRN_WIDTH = HGRN_HEADS * HGRN_DK
HGRN_VWIDTH = HGRN_HEADS * HGRN_DV
HGRN_BLOCK = 16
HGRN_F_MIN = 1e-30

MLA_HEADS = 4
MLA_Q_RANK = 256
MLA_KV_RANK = 128
MLA_NOPE = 128
MLA_ROPE = 64
MLA_V = 128
MLA_QK = MLA_NOPE + MLA_ROPE
ROPE_THETA = 10000.0
Q_BLOCK = 128
MASK_VALUE = -1e30

S5_GROUPS = 32
S5_GROUP_CH = 16
S5_STATE = 64
S5_WIDTH = S5_GROUPS * S5_GROUP_CH
S5_DT_MIN = 0.001
S5_DT_MAX = 0.1

LRU_WIDTH = 512
LRU_BLOCKS = 8
LRU_BLOCK_W = LRU_WIDTH // LRU_BLOCKS
LRU_CONV = 4
LRU_C = 8.0

N_BRANCH = 4
BRANCH_WIDTH = 512

N_EXPERTS = 32
TOP_K = 4
EXPERT_FF = 1024
SWIGLU_ALPHA = 1.702
SWIGLU_LIMIT = 7.0
EXPERT_BLOCK = 256

DEEPNORM_ALPHA = (2.0 * DEPTH) ** 0.25
DEEPNORM_BETA = (8.0 * DEPTH) ** -0.25
LN_EPS = 1e-5
RMS_EPS = 1e-6

IN_SPLITS = (HGRN_WIDTH, HGRN_WIDTH, HGRN_VWIDTH, HGRN_VWIDTH,
             MLA_Q_RANK, MLA_KV_RANK + MLA_ROPE,
             S5_WIDTH,
             LRU_WIDTH, LRU_WIDTH,
             N_BRANCH * D_MODEL)
D_IN = sum(IN_SPLITS)
IN_OFFSETS = tuple(int(v) for v in np.cumsum(IN_SPLITS)[:-1])

kernel_name = 'hybrid_streaming_encoder_block'


def layer_norm(x, w, b):
    xf = x.astype(jnp.float32)
    xc = xf - jnp.mean(xf, axis=-1, keepdims=True)
    var = jnp.mean(xc * xc, axis=-1, keepdims=True)
    return (xc * lax.rsqrt(var + LN_EPS) * w + b).astype(x.dtype)


def rms_norm(x, w):
    xf = x.astype(jnp.float32)
    ms = jnp.mean(xf * xf, axis=-1, keepdims=True)
    return (xf * lax.rsqrt(ms + RMS_EPS) * w).astype(x.dtype)


def linear_recurrence_combine(left, right):
    a_l, b_l = left
    a_r, b_r = right
    return a_l * a_r, a_r * b_l + b_r


def hgrn2_mixer(q_raw, f_raw, i_raw, g_raw, lb, norm_w):
    bsz, seq, _ = q_raw.shape
    f32 = jnp.float32
    nb = seq // HGRN_BLOCK
    lb = lb.astype(f32)
    z = f_raw.astype(f32)
    q = jax.nn.silu(q_raw.astype(f32))
    log_f = jnp.log(jnp.maximum(lb + (1.0 - lb) * jax.nn.sigmoid(z), HGRN_F_MIN))
    k = (1.0 - lb) * jax.nn.sigmoid(-z)
    v = i_raw.astype(f32)

    def to_blocks(t, d):
        return t.reshape(bsz, nb, HGRN_BLOCK, HGRN_HEADS, d).transpose(1, 0, 3, 2, 4)

    q, k, log_f = to_blocks(q, HGRN_DK), to_blocks(k, HGRN_DK), to_blocks(log_f, HGRN_DK)
    v = to_blocks(v, HGRN_DV)
    b = jnp.cumsum(log_f, axis=3)
    b_last = b[..., HGRN_BLOCK - 1:HGRN_BLOCK, :]
    q_inter = q * jnp.exp(b)
    k_state = k * jnp.exp(b_last - b)
    decay = jnp.exp(b_last[..., 0, :])
    causal = jnp.tril(jnp.ones((HGRN_BLOCK, HGRN_BLOCK), dtype=bool))[:, :, None]

    def block_step(state, xs):
        q_c, k_c, v_c, b_c, qi_c, ks_c, d_c = xs
        diff = b_c[:, :, :, None, :] - b_c[:, :, None, :, :]
        w = jnp.where(causal, jnp.exp(jnp.where(causal, diff, 0.0)), 0.0)
        scores = jnp.einsum('bhtd,bhsd,bhtsd->bhts', q_c, k_c, w)
        o_c = jnp.einsum('bhts,bhsv->bhtv', scores, v_c) + jnp.einsum('bhtd,bhdv->bhtv', qi_c, state)
        state = state * d_c[..., None] + jnp.einsum('bhsd,bhsv->bhdv', ks_c, v_c)
        return state, o_c

    state0 = jnp.zeros((bsz, HGRN_HEADS, HGRN_DK, HGRN_DV), f32)
    _, o = lax.scan(block_step, state0, (q, k, v, b, q_inter, k_state, decay))
    o = rms_norm(o, norm_w.astype(f32).reshape(HGRN_HEADS, 1, HGRN_DV))
    o = o.transpose(1, 0, 3, 2, 4).reshape(bsz, seq, HGRN_VWIDTH)
    return (o * jax.nn.silu(g_raw.astype(f32))).astype(q_raw.dtype)


def rope_cos_sin(positions):
    inv_freq = ROPE_THETA ** (-jnp.arange(0, MLA_ROPE, 2, dtype=jnp.float32) / MLA_ROPE)
    ang = positions.astype(jnp.float32)[..., None] * inv_freq
    return jnp.cos(ang), jnp.sin(ang)


def apply_rope(x, cos, sin):
    x1, x2 = jnp.split(x.astype(jnp.float32), 2, axis=-1)
    return jnp.concatenate([x1 * cos - x2 * sin, x2 * cos + x1 * sin], axis=-1).astype(x.dtype)


def mla_mixer(cq_raw, ckv_raw, positions, q_norm_w, wq_b, kv_norm_w, wkv_b):
    bsz, seq, _ = cq_raw.shape
    q = (rms_norm(cq_raw, q_norm_w) @ wq_b).reshape(bsz, seq, MLA_HEADS, MLA_QK)
    c_kv = rms_norm(ckv_raw[..., :MLA_KV_RANK], kv_norm_w)
    kv = (c_kv @ wkv_b).reshape(bsz, seq, MLA_HEADS, MLA_NOPE + MLA_V)
    k_nope, v = kv[..., :MLA_NOPE], kv[..., MLA_NOPE:]
    cos, sin = rope_cos_sin(positions)
    q_pe = apply_rope(q[..., MLA_NOPE:], cos[:, :, None, :], sin[:, :, None, :])
    k_pe = apply_rope(ckv_raw[..., MLA_KV_RANK:], cos, sin)
    q = jnp.concatenate([q[..., :MLA_NOPE], q_pe], axis=-1)
    k = jnp.concatenate([k_nope, jnp.broadcast_to(k_pe[:, :, None, :], (bsz, seq, MLA_HEADS, MLA_ROPE))], axis=-1)
    nqb = seq // Q_BLOCK
    q_blocks = q.reshape(bsz, nqb, Q_BLOCK, MLA_HEADS, MLA_QK).transpose(1, 0, 2, 3, 4)
    key_chunk = jnp.arange(seq) // CHUNK
    scale = MLA_QK ** -0.5

    def attend_block(args):
        q_blk, blk = args
        q_chunk = (blk * Q_BLOCK + jnp.arange(Q_BLOCK)) // CHUNK
        s = jnp.einsum('bqhd,bkhd->bhqk', q_blk, k).astype(jnp.float32) * scale
        s = jnp.where(key_chunk[None, :] <= q_chunk[:, None], s, MASK_VALUE)
        p = jax.nn.softmax(s, axis=-1).astype(v.dtype)
        return jnp.einsum('bhqk,bkhv->bqhv', p, v)

    o = lax.map(attend_block, (q_blocks, jnp.arange(nqb)))
    return o.transpose(1, 0, 2, 3, 4).reshape(bsz, seq, MLA_HEADS * MLA_V)


def s5_mixer(u_raw, a_re, a_im, log_dt, b_re, b_im, c_re, c_im, d_skip, w_glu, b_glu):
    bsz, seq, _ = u_raw.shape
    f32 = jnp.float32
    u = u_raw.astype(f32).reshape(bsz, seq, S5_GROUPS, S5_GROUP_CH)
    lam = lax.complex(a_re.astype(f32), a_im.astype(f32))
    dt = jnp.exp(log_dt.astype(f32))[:, None]
    lam_bar = jnp.exp(lam * dt)
    b_bar = ((lam_bar - 1.0) / lam)[..., None] * lax.complex(b_re.astype(f32), b_im.astype(f32))
    c_mat = lax.complex(c_re.astype(f32), c_im.astype(f32))
    bu = jnp.einsum('gnc,bsgc->bsgn', b_bar, u.astype(jnp.complex64))
    a_seq = jnp.broadcast_to(lam_bar[None, None], (1, seq, S5_GROUPS, S5_STATE))
    _, states = lax.associative_scan(linear_recurrence_combine, (a_seq, bu), axis=1)
    y = jnp.einsum('gcn,bsgn->bsgc', c_mat, states).real + d_skip.astype(f32) * u
    y = jax.nn.gelu(y.reshape(bsz, seq, S5_WIDTH))
    y = y * jax.nn.sigmoid(y @ w_glu.astype(f32) + b_glu.astype(f32))
    return y.astype(u_raw.dtype)


def rglru_mixer(x_raw, gate_raw, conv_w, conv_b, wa, ba, wx, bx, a_param):
    bsz, seq, width = x_raw.shape
    f32 = jnp.float32
    xc = lax.conv_general_dilated(x_raw, conv_w.astype(x_raw.dtype)[:, None, :], window_strides=(1,),
                                  padding=[(LRU_CONV - 1, 0)], dimension_numbers=('NWC', 'WIO', 'NWC'),
                                  feature_group_count=width) + conv_b
    xb = xc.reshape(bsz, seq, LRU_BLOCKS, LRU_BLOCK_W)
    r = jax.nn.sigmoid(jnp.einsum('bshi,hij->bshj', xb, wa).reshape(bsz, seq, width).astype(f32) + ba)
    i = jax.nn.sigmoid(jnp.einsum('bshi,hij->bshj', xb, wx).reshape(bsz, seq, width).astype(f32) + bx)
    log_a = -LRU_C * r * jax.nn.softplus(a_param.astype(f32))
    a = jnp.exp(log_a)
    b = jnp.sqrt(jnp.maximum(-jnp.expm1(2.0 * log_a), 0.0)) * (i * xc.astype(f32))
    _, h = lax.associative_scan(linear_recurrence_combine, (a, b), axis=1)
    return (h * jax.nn.gelu(gate_raw.astype(f32))).astype(x_raw.dtype)


def clamped_swiglu(h):
    glu, lin = jnp.split(h, 2, axis=-1)
    glu = jnp.minimum(glu, SWIGLU_LIMIT)
    lin = jnp.clip(lin, -SWIGLU_LIMIT, SWIGLU_LIMIT)
    return glu * jax.nn.sigmoid(SWIGLU_ALPHA * glu) * (lin + 1.0)


def moe_ffn(x, router_w, router_b, w1, b1, w2, b2):
    bsz, seq, d = x.shape
    n = bsz * seq
    xt = x.reshape(n, d)
    logits = (xt @ router_w + router_b).astype(jnp.float32)
    top_val, top_idx = lax.top_k(logits, TOP_K)
    top_w = jax.nn.softmax(top_val, axis=-1)
    nk = n * TOP_K
    expert_flat = top_idx.reshape(nk)
    token_flat = jnp.arange(nk, dtype=jnp.int32) // TOP_K
    weight_flat = top_w.reshape(nk)
    counts = jnp.bincount(expert_flat, length=N_EXPERTS)
    padded = ((counts + EXPERT_BLOCK - 1) // EXPERT_BLOCK) * EXPERT_BLOCK
    p_end = jnp.cumsum(padded)
    p_start = p_end - padded
    u_start = jnp.cumsum(counts) - counts
    order = jnp.argsort(expert_flat)
    e_sorted = expert_flat[order]
    dest = p_start[e_sorted] + (jnp.arange(nk, dtype=jnp.int32) - u_start[e_sorted])
    n_slots = nk + N_EXPERTS * EXPERT_BLOCK
    n_blocks = n_slots // EXPERT_BLOCK
    slot_token = jnp.full((n_slots,), n, dtype=jnp.int32).at[dest].set(token_flat[order])
    slot_w = jnp.zeros((n_slots,), jnp.float32).at[dest].set(weight_flat[order])
    block_expert = jnp.minimum(
        jnp.searchsorted(p_end, jnp.arange(n_blocks, dtype=jnp.int32) * EXPERT_BLOCK, side='right'),
        N_EXPERTS - 1)
    x_pad = jnp.concatenate([xt, jnp.zeros((1, d), xt.dtype)], axis=0)
    x_slots = x_pad[slot_token].reshape(n_blocks, EXPERT_BLOCK, d)

    def expert_block(args):
        xb, e = args
        return clamped_swiglu(xb @ w1[e] + b1[e]) @ w2[e] + b2[e]

    y_slots = lax.map(expert_block, (x_slots, block_expert)).reshape(n_slots, d)
    out = jnp.zeros((n + 1, d), jnp.float32).at[slot_token].add(y_slots.astype(jnp.float32) * slot_w[:, None])
    return out[:n].reshape(bsz, seq, d).astype(x.dtype)


def hybrid_layer(x, positions, lb, w_in, b_gate, hgrn_norm_w,
                 mla_q_norm_w, mla_wq_b, mla_kv_norm_w, mla_wkv_b,
                 s5_a_re, s5_a_im, s5_log_dt, s5_b_re, s5_b_im, s5_c_re, s5_c_im, s5_d, s5_w_glu, s5_b_glu,
                 lru_conv_w, lru_conv_b, lru_wa, lru_ba, lru_wx, lru_bx, lru_a_param,
                 w_branch, w_out, ln1_w, ln1_b, ln2_w, ln2_b,
                 router_w, router_b, moe_w1, moe_b1, moe_w2, moe_b2):
    bsz, seq, _ = x.shape
    proj = x @ w_in
    hq, hf, hi, hg, cq, ckv, su, lx, lg, gl = jnp.split(proj, IN_OFFSETS, axis=-1)
    y_a = hgrn2_mixer(hq, hf, hi, hg, lb, hgrn_norm_w)
    y_b = mla_mixer(cq, ckv, positions, mla_q_norm_w, mla_wq_b, mla_kv_norm_w, mla_wkv_b)
    y_c = s5_mixer(su, s5_a_re, s5_a_im, s5_log_dt, s5_b_re, s5_b_im, s5_c_re, s5_c_im, s5_d, s5_w_glu, s5_b_glu)
    y_d = rglru_mixer(lx, lg, lru_conv_w, lru_conv_b, lru_wa, lru_ba, lru_wx, lru_bx, lru_a_param)
    branches = jnp.stack([y_a.astype(x.dtype), y_b.astype(x.dtype), y_c.astype(x.dtype), y_d.astype(x.dtype)], axis=2)
    branch_out = jnp.einsum('bsnw,nwd->bsnd', branches, w_branch)
    gates = jax.nn.sigmoid(gl.reshape(bsz, seq, N_BRANCH, D_MODEL) + b_gate)
    mix = jnp.sum(gates * branch_out, axis=2) @ w_out
    h = layer_norm(DEEPNORM_ALPHA * x + mix, ln1_w, ln1_b)
    ffn = moe_ffn(h, router_w, router_b, moe_w1, moe_b1, moe_w2, moe_b2)
    return layer_norm(DEEPNORM_ALPHA * h + ffn, ln2_w, ln2_b)


def setup_inputs(seed: int = 0) -> dict:
    key = jax.random.key(seed)
    ks = jax.random.split(key, 40)
    f32 = jnp.float32
    L = DEPTH

    def nrm(k, shape, scale):
        return jax.random.normal(k, shape, f32) * scale

    x = nrm(ks[0], (BATCH, SEQ, D_MODEL), 1.0)
    offsets = jax.random.randint(ks[1], (BATCH, 1), 0, 4 * SEQ, dtype=jnp.int32)
    positions = offsets + jnp.arange(SEQ, dtype=jnp.int32)[None, :]
    u_a = jax.random.uniform(ks[26], (L, LRU_WIDTH), f32, 0.9, 0.999)
    return {
        'x': x,
        'positions': positions,
        'w_in': nrm(ks[2], (L, D_MODEL, D_IN), D_MODEL ** -0.5),
        'b_gate': nrm(ks[3], (L, N_BRANCH, D_MODEL), 0.02),
        'hgrn_lb_logits': 1.0 + nrm(ks[4], (L, HGRN_WIDTH), 0.1),
        'hgrn_norm_w': 1.0 + nrm(ks[5], (L, HGRN_VWIDTH), 0.02),
        'mla_q_norm_w': 1.0 + nrm(ks[6], (L, MLA_Q_RANK), 0.02),
        'mla_wq_b': nrm(ks[7], (L, MLA_Q_RANK, MLA_HEADS * MLA_QK), MLA_Q_RANK ** -0.5),
        'mla_kv_norm_w': 1.0 + nrm(ks[8], (L, MLA_KV_RANK), 0.02),
        'mla_wkv_b': nrm(ks[9], (L, MLA_KV_RANK, MLA_HEADS * (MLA_NOPE + MLA_V)), MLA_KV_RANK ** -0.5),
        's5_a_re': -0.5 + nrm(ks[10], (L, S5_GROUPS, S5_STATE), 0.01),
        's5_a_im': jnp.tile(jnp.pi * jnp.arange(S5_STATE, dtype=f32), (L, S5_GROUPS, 1)),
        's5_log_dt': jax.random.uniform(ks[11], (L, S5_GROUPS), f32, math.log(S5_DT_MIN), math.log(S5_DT_MAX)),
        's5_b_re': nrm(ks[12], (L, S5_GROUPS, S5_STATE, S5_GROUP_CH), (2.0 * S5_GROUP_CH) ** -0.5),
        's5_b_im': nrm(ks[13], (L, S5_GROUPS, S5_STATE, S5_GROUP_CH), (2.0 * S5_GROUP_CH) ** -0.5),
        's5_c_re': nrm(ks[14], (L, S5_GROUPS, S5_GROUP_CH, S5_STATE), (2.0 * S5_STATE) ** -0.5),
        's5_c_im': nrm(ks[15], (L, S5_GROUPS, S5_GROUP_CH, S5_STATE), (2.0 * S5_STATE) ** -0.5),
        's5_d': nrm(ks[16], (L, S5_GROUPS, S5_GROUP_CH), 1.0),
        's5_w_glu': nrm(ks[17], (L, S5_WIDTH, S5_WIDTH), S5_WIDTH ** -0.5),
        's5_b_glu': nrm(ks[18], (L, S5_WIDTH), 0.02),
        'lru_conv_w': nrm(ks[19], (L, LRU_CONV, LRU_WIDTH), LRU_CONV ** -0.5),
        'lru_conv_b': nrm(ks[20], (L, LRU_WIDTH), 0.02),
        'lru_wa': nrm(ks[21], (L, LRU_BLOCKS, LRU_BLOCK_W, LRU_BLOCK_W), LRU_BLOCK_W ** -0.5),
        'lru_ba': nrm(ks[22], (L, LRU_WIDTH), 0.02),
        'lru_wx': nrm(ks[23], (L, LRU_BLOCKS, LRU_BLOCK_W, LRU_BLOCK_W), LRU_BLOCK_W ** -0.5),
        'lru_bx': nrm(ks[24], (L, LRU_WIDTH), 0.02),
        'lru_a_param': jnp.log(jnp.expm1(-jnp.log(u_a) / LRU_C)),
        'w_branch': nrm(ks[25], (L, N_BRANCH, BRANCH_WIDTH, D_MODEL), DEEPNORM_BETA * BRANCH_WIDTH ** -0.5),
        'w_out': nrm(ks[27], (L, D_MODEL, D_MODEL), DEEPNORM_BETA * D_MODEL ** -0.5),
        'ln1_w': 1.0 + nrm(ks[28], (L, D_MODEL), 0.02),
        'ln1_b': nrm(ks[29], (L, D_MODEL), 0.02),
        'ln2_w': 1.0 + nrm(ks[30], (L, D_MODEL), 0.02),
        'ln2_b': nrm(ks[31], (L, D_MODEL), 0.02),
        'router_w': nrm(ks[32], (L, D_MODEL, N_EXPERTS), D_MODEL ** -0.5),
        'router_b': nrm(ks[33], (L, N_EXPERTS), 0.01),
        'moe_w1': nrm(ks[34], (L, N_EXPERTS, D_MODEL, 2 * EXPERT_FF), DEEPNORM_BETA * D_MODEL ** -0.5),
        'moe_b1': nrm(ks[35], (L, N_EXPERTS, 2 * EXPERT_FF), 0.01),
        'moe_w2': nrm(ks[36], (L, N_EXPERTS, EXPERT_FF, D_MODEL), DEEPNORM_BETA * EXPERT_FF ** -0.5),
        'moe_b2': nrm(ks[37], (L, N_EXPERTS, D_MODEL), 0.01),
    }


def reference(x, positions, w_in, b_gate, hgrn_lb_logits, hgrn_norm_w,
              mla_q_norm_w, mla_wq_b, mla_kv_norm_w, mla_wkv_b,
              s5_a_re, s5_a_im, s5_log_dt, s5_b_re, s5_b_im, s5_c_re, s5_c_im, s5_d, s5_w_glu, s5_b_glu,
              lru_conv_w, lru_conv_b, lru_wa, lru_ba, lru_wx, lru_bx, lru_a_param,
              w_branch, w_out, ln1_w, ln1_b, ln2_w, ln2_b,
              router_w, router_b, moe_w1, moe_b1, moe_w2, moe_b2):
    probs = jax.nn.softmax(hgrn_lb_logits.astype(jnp.float32), axis=0)
    lower_bounds = jnp.cumsum(probs, axis=0) - probs[0:1]
    for l in range(DEPTH):
        x = hybrid_layer(x, positions, lower_bounds[l], w_in[l], b_gate[l], hgrn_norm_w[l],
                         mla_q_norm_w[l], mla_wq_b[l], mla_kv_norm_w[l], mla_wkv_b[l],
                         s5_a_re[l], s5_a_im[l], s5_log_dt[l], s5_b_re[l], s5_b_im[l], s5_c_re[l], s5_c_im[l],
                         s5_d[l], s5_w_glu[l], s5_b_glu[l],
                         lru_conv_w[l], lru_conv_b[l], lru_wa[l], lru_ba[l], lru_wx[l], lru_bx[l], lru_a_param[l],
                         w_branch[l], w_out[l], ln1_w[l], ln1_b[l], ln2_w[l], ln2_b[l],
                         router_w[l], router_b[l], moe_w1[l], moe_b1[l], moe_w2[l], moe_b2[l])
    return x
```

```python
import functools
import math

import numpy as np
import jax
import jax.numpy as jnp
from jax import lax
from jax.experimental import pallas as pl
from jax.experimental.pallas import tpu as pltpu

F32 = jnp.float32
BF16 = jnp.bfloat16

D_MODEL = 1024
DEPTH = 2
CHUNK = 64

HGRN_HEADS = 4
HGRN_DK = 128
HGRN_WIDTH = 512
HGRN_F_MIN = 1e-30
HGRN_CHUNK = 64
HGRN_LEVELS = (1, 2, 4, 8, 16, 32)

MLA_HEADS = 4
MLA_Q_RANK = 256
MLA_KV_RANK = 128
MLA_NOPE = 128
MLA_ROPE = 64
MLA_V = 128
MLA_QK = MLA_NOPE + MLA_ROPE
ROPE_THETA = 10000.0

S5_GROUPS = 32
S5_GROUP_CH = 16
S5_STATE = 64
S5_WIDTH = 512
S5_BLOCK = 16

LRU_WIDTH = 512
LRU_BLOCKS = 8
LRU_BLOCK_W = 64
LRU_CONV = 4
LRU_C = 8.0

N_BRANCH = 4
BRANCH_WIDTH = 512

N_EXPERTS = 32
TOP_K = 4
EXPERT_FF = 1024
SWIGLU_ALPHA = 1.702
SWIGLU_LIMIT = 7.0
EXPERT_TILE = 512

DEEPNORM_ALPHA = (2.0 * DEPTH) ** 0.25
LN_EPS = 1e-5
RMS_EPS = 1e-6

COL_HQ, COL_HF, COL_HI, COL_HG = 0, 512, 1024, 1536
COL_SU, COL_LX, COL_LG = 2048, 2560, 3072
COL_CQ, COL_CKV, COL_KPE, COL_KPER = 3584, 3840, 3968, 4096
MIX_WIDTH = 4224

VMEM_LIMIT = 56 * 1024 * 1024


def _cparams(sem):
    return pltpu.CompilerParams(dimension_semantics=sem, vmem_limit_bytes=VMEM_LIMIT)


def _dot(a, b):
    return jnp.dot(a, b, preferred_element_type=F32)


def _dot_nt(a, b):
    return lax.dot_general(a, b, (((1,), (1,)), ((), ())), preferred_element_type=F32)


def _dot_tn(a, b):
    return lax.dot_general(a, b, (((0,), (0,)), ((), ())), preferred_element_type=F32)


def _split(x):
    hi = x.astype(BF16)
    lo = (x - hi.astype(F32)).astype(BF16)
    return hi, lo


def _dot3(a, b_hi, b_lo):
    a_hi, a_lo = _split(a)
    return _dot(a_hi, b_hi) + (_dot(a_lo, b_hi) + _dot(a_hi, b_lo))


def _shift_rows(x, k, fill):
    rows = lax.broadcasted_iota(jnp.int32, x.shape, 0)
    return jnp.where(rows >= k, pltpu.roll(x, k, 0), fill)


def _mm_kernel(x_ref, w_ref, o_ref):
    o_ref[...] = _dot(x_ref[...].astype(BF16), w_ref[...]).astype(o_ref.dtype)


def _matmul(x, w, tm, tn, out_dtype):
    n, k = x.shape
    m = w.shape[1]
    tm = min(tm, n)
    return pl.pallas_call(
        _mm_kernel,
        grid=(n // tm, m // tn),
        in_specs=[pl.BlockSpec((tm, k), lambda i, j: (i, 0)),
                  pl.BlockSpec((k, tn), lambda i, j: (0, j))],
        out_specs=pl.BlockSpec((tm, tn), lambda i, j: (i, j)),
        out_shape=jax.ShapeDtypeStruct((n, m), out_dtype),
        compiler_params=_cparams(("parallel", "arbitrary")),
        name="in_proj",
    )(x, w)


def _hgrn_tables():
    c = HGRN_CHUNK
    t = np.arange(c)
    nl = len(HGRN_LEVELS)
    sel = np.zeros((nl, 2 * c, c), np.float32)
    keep = np.zeros((nl, 2 * c, 1), np.float32)
    mask = np.zeros((nl + 1, c, c), np.float32)
    for li, h in enumerate(HGRN_LEVELS):
        blk = t // h
        odd = blk % 2 == 1
        for r in t:
            if odd[r]:
                sel[li, r, h * blk[r] - 1] = 1.0
            else:
                sel[li, c + r, h * blk[r] + 2 * h - 1] = 1.0
        keep[li, :c, 0] = (~odd).astype(np.float32)
        keep[li, c:, 0] = odd.astype(np.float32)
        mask[li] = (odd[:, None] & (blk[None, :] == blk[:, None] - 1)).astype(np.float32)
    mask[nl] = np.eye(c, dtype=np.float32)
    keep = np.broadcast_to(keep, (nl, 2 * c, HGRN_DK)).copy()
    return sel, keep, mask


def _hgrn_kernel(hq_ref, hf_ref, hi_ref, hg_ref, lb_ref, nw_ref, sel_ref, keep_ref, mask_ref,
                 o_ref, state_ref, *, n_chunks):
    c = HGRN_CHUNK

    @pl.when(pl.program_id(1) == 0)
    def _():
        state_ref[...] = jnp.zeros_like(state_ref)

    def chunk_body(ci, carry):
        r0 = pl.multiple_of(ci * c, c)
        for hh in range(HGRN_HEADS):
            ls = slice(hh * HGRN_DK, (hh + 1) * HGRN_DK)
            lb = lb_ref[:, ls]
            z = hf_ref[pl.ds(r0, c), ls]
            f = jnp.maximum(lb + (1.0 - lb) * jax.nn.sigmoid(z), HGRN_F_MIN)
            k = (1.0 - lb) * jax.nn.sigmoid(-z)
            q = jax.nn.silu(hq_ref[pl.ds(r0, c), ls])
            v = hi_ref[pl.ds(r0, c), ls]
            q16 = q.astype(BF16)
            prefix = f
            suffix = jnp.ones_like(f)
            scores = mask_ref[len(HGRN_LEVELS)] * _dot_nt(q16, k.astype(BF16))
            for li in range(len(HGRN_LEVELS)):
                s_l = _dot_nt((q * prefix).astype(BF16), (k * suffix).astype(BF16))
                scores = scores + mask_ref[li] * s_l
                p_hi, p_lo = _split(prefix)
                sel = sel_ref[li]
                gathered = _dot(sel, p_hi) + _dot(sel, p_lo) + keep_ref[li]
                prefix = prefix * gathered[:c]
                suffix = suffix * gathered[c:]
            st = state_ref[hh]
            o = _dot(scores.astype(BF16), v.astype(BF16)) + _dot_nt((q * prefix).astype(BF16), st.astype(BF16))
            decay = prefix[c - 1:c, :]
            state_ref[hh] = st * decay + _dot_tn(v.astype(BF16), (k * suffix).astype(BF16))
            ms = jnp.mean(o * o, axis=-1, keepdims=True)
            o = o * lax.rsqrt(ms + RMS_EPS) * nw_ref[:, ls]
            o_ref[pl.ds(r0, c), ls] = o * jax.nn.silu(hg_ref[pl.ds(r0, c), ls])
        return carry

    lax.fori_loop(0, n_chunks, chunk_body, 0)


def _hgrn(p, lb, norm_w, bsz, seq):
    t = min(256, seq)
    nt = seq // t
    sel, keep, mask = _hgrn_tables()
    col = lambda cb: pl.BlockSpec((t, HGRN_WIDTH), lambda b, i: (b * nt + i, cb))
    const2 = lambda shape: pl.BlockSpec(shape, lambda b, i: (0,) * len(shape))
    return pl.pallas_call(
        functools.partial(_hgrn_kernel, n_chunks=t // HGRN_CHUNK),
        grid=(bsz, nt),
        in_specs=[col(COL_HQ // 512), col(COL_HF // 512), col(COL_HI // 512), col(COL_HG // 512),
                  const2((1, HGRN_WIDTH)), const2((1, HGRN_WIDTH)),
                  const2(sel.shape), const2(keep.shape), const2(mask.shape)],
        out_specs=pl.BlockSpec((t, HGRN_WIDTH), lambda b, i: (b * nt + i, 0)),
        out_shape=jax.ShapeDtypeStruct((bsz * seq, HGRN_WIDTH), F32),
        scratch_shapes=[pltpu.VMEM((HGRN_HEADS, HGRN_DK, HGRN_DK), F32)],
        compiler_params=_cparams(("parallel", "arbitrary")),
        name="hgrn",
    )(p, p, p, p, lb.reshape(1, -1), norm_w.reshape(1, -1),
      jnp.asarray(sel, BF16), jnp.asarray(keep), jnp.asarray(mask))


def _rms(x, w):
    ms = jnp.mean(x * x, axis=-1, keepdims=True)
    return x * lax.rsqrt(ms + RMS_EPS) * w


def _mla_proj_kernel(cq_ref, ckv_ref, kpe_ref, kper_ref, ang_ref, qnw_ref, kvnw_ref,
                     wqn_ref, wqp_ref, wqpr_ref, wkn_ref, wv_ref,
                     qn_ref, qp_ref, kn_ref, kp_ref, v_ref):
    ang = ang_ref[...]
    cos = jnp.cos(ang)
    sin = jnp.sin(ang)
    qn = _rms(cq_ref[...], qnw_ref[...]).astype(BF16)
    cos4 = jnp.concatenate([cos] * MLA_HEADS, axis=-1)
    sin4 = jnp.concatenate([sin] * MLA_HEADS, axis=-1)
    qn_ref[...] = _dot(qn, wqn_ref[...]).astype(BF16)
    qp_ref[...] = (_dot(qn, wqp_ref[...]) * cos4 + _dot(qn, wqpr_ref[...]) * sin4).astype(BF16)
    cn = _rms(ckv_ref[...], kvnw_ref[...]).astype(BF16)
    kn_ref[...] = _dot(cn, wkn_ref[...]).astype(BF16)
    v_ref[...] = _dot(cn, wv_ref[...]).astype(BF16)
    kp_ref[...] = (kpe_ref[...] * cos + kper_ref[...] * sin).astype(BF16)


def _rot_half_cols(w):
    half = w.shape[-1] // 2
    return jnp.concatenate([-w[..., half:], w[..., :half]], axis=-1)


def _mla_proj(p, positions, q_norm_w, wq_b, kv_norm_w, wkv_b):
    n = p.shape[0]
    tm = min(512, n)
    scale = MLA_QK ** -0.5
    wq = wq_b.reshape(MLA_Q_RANK, MLA_HEADS, MLA_QK) * scale
    w_nope = wq[:, :, :MLA_NOPE].reshape(MLA_Q_RANK, MLA_HEADS * MLA_NOPE)
    w_pe = wq[:, :, MLA_NOPE:]
    pad = jnp.zeros((MLA_Q_RANK, MLA_HEADS, 128 - MLA_ROPE), F32)
    w_pe_p = jnp.concatenate([w_pe, pad], axis=-1).reshape(MLA_Q_RANK, MLA_HEADS * 128)
    w_per_p = jnp.concatenate([_rot_half_cols(w_pe), pad], axis=-1).reshape(MLA_Q_RANK, MLA_HEADS * 128)
    wkv = wkv_b.reshape(MLA_KV_RANK, MLA_HEADS, MLA_NOPE + MLA_V)
    w_kn = wkv[:, :, :MLA_NOPE].reshape(MLA_KV_RANK, MLA_HEADS * MLA_NOPE)
    w_v = wkv[:, :, MLA_NOPE:].reshape(MLA_KV_RANK, MLA_HEADS * MLA_V)
    inv_freq = ROPE_THETA ** (-jnp.arange(0, MLA_ROPE, 2, dtype=F32) / MLA_ROPE)
    freq_lane = jnp.concatenate([inv_freq, inv_freq, jnp.zeros((128 - MLA_ROPE,), F32)])
    ang = positions.reshape(n, 1).astype(F32) * freq_lane[None, :]

    row = lambda w, cb: pl.BlockSpec((tm, w), lambda i: (i, cb))
    full = lambda a: pl.BlockSpec(a.shape, lambda i: (0,) * a.ndim)
    weights = [w_nope.astype(BF16), w_pe_p.astype(BF16), w_per_p.astype(BF16), w_kn.astype(BF16), w_v.astype(BF16)]
    qnw = q_norm_w.reshape(1, -1)
    kvnw = kv_norm_w.reshape(1, -1)
    outs = pl.pallas_call(
        _mla_proj_kernel,
        grid=(n // tm,),
        in_specs=[row(256, COL_CQ // 256), row(128, COL_CKV // 128), row(128, COL_KPE // 128),
                  row(128, COL_KPER // 128), row(128, 0), full(qnw), full(kvnw)] + [full(w) for w in weights],
        out_specs=[row(512, 0), row(512, 0), row(512, 0), row(128, 0), row(512, 0)],
        out_shape=[jax.ShapeDtypeStruct((n, 512), BF16), jax.ShapeDtypeStruct((n, 512), BF16),
                   jax.ShapeDtypeStruct((n, 512), BF16), jax.ShapeDtypeStruct((n, 128), BF16),
                   jax.ShapeDtypeStruct((n, 512), BF16)],
        compiler_params=_cparams(("parallel",)),
        name="mla_proj",
    )(p, p, p, p, ang, qnw, kvnw, *weights)
    return outs


def _flash_kernel(qn_ref, qp_ref, kn_ref, kp_ref, v_ref, o_ref, *, tq):
    i = pl.program_id(2)
    q = jnp.concatenate([qn_ref[...], qp_ref[...]], axis=-1)

    def scores(j):
        r0 = pl.multiple_of(j * tq, tq)
        kj = jnp.concatenate([kn_ref[pl.ds(r0, tq), :], kp_ref[pl.ds(r0, tq), :]], axis=-1)
        return _dot_nt(q, kj), v_ref[pl.ds(r0, tq), :]

    def update(carry, s, vj):
        m, l, acc = carry
        m_new = jnp.maximum(m, jnp.max(s, axis=-1, keepdims=True))
        alpha = jnp.exp(m - m_new)
        pj = jnp.exp(s - m_new)
        l = alpha * l + jnp.sum(pj, axis=-1, keepdims=True)
        acc = alpha * acc + _dot(pj.astype(BF16), vj)
        return m_new, l, acc

    def body(j, carry):
        s, vj = scores(j)
        return update(carry, s, vj)

    init = (jnp.full((tq, 1), -1e30, F32), jnp.zeros((tq, 1), F32), jnp.zeros((tq, MLA_V), F32))
    carry = lax.fori_loop(0, i, body, init)
    s, vj = scores(i)
    shift = CHUNK.bit_length() - 1
    rq = lax.shift_right_logical(lax.broadcasted_iota(jnp.int32, (tq, tq), 0), shift)
    ck = lax.shift_right_logical(lax.broadcasted_iota(jnp.int32, (tq, tq), 1), shift)
    s = jnp.where(ck <= rq, s, -1e30)
    m, l, acc = update(carry, s, vj)
    o_ref[...] = (acc / l).astype(o_ref.dtype)


def _flash(qn, qp, kn, kp, v, bsz, seq):
    tq = min(256, seq)
    nq = seq // tq
    qspec = pl.BlockSpec((tq, 128), lambda b, h, i: (b * nq + i, h))
    kspec = pl.BlockSpec((seq, 128), lambda b, h, i: (b, h))
    kpspec = pl.BlockSpec((seq, 128), lambda b, h, i: (b, 0))
    return pl.pallas_call(
        functools.partial(_flash_kernel, tq=tq),
        grid=(bsz, MLA_HEADS, nq),
        in_specs=[qspec, qspec, kspec, kpspec, kspec],
        out_specs=qspec,
        out_shape=jax.ShapeDtypeStruct((bsz * seq, MLA_HEADS * MLA_V), BF16),
        compiler_params=_cparams(("parallel", "parallel", "arbitrary")),
        name="flash",
    )(qn, qp, kn, kp, v)


def _s5_tables(a_re, a_im, log_dt, b_re, b_im, c_re, c_im, d_skip):
    lb = S5_BLOCK
    lam = lax.complex(a_re.astype(F32), a_im.astype(F32))
    dt = jnp.exp(log_dt.astype(F32))[:, None]
    lam_dt = lam * dt
    lam_bar = jnp.exp(lam_dt)
    b_bar = ((lam_bar - 1.0) / lam)[..., None] * lax.complex(b_re.astype(F32), b_im.astype(F32))
    c_mat = lax.complex(c_re.astype(F32), c_im.astype(F32))
    tau = jnp.arange(lb + 1, dtype=F32)
    pw = jnp.exp(lam_dt[None] * tau[:, None, None])
    kern = jnp.einsum('gcn,tgn,gnd->tgcd', c_mat, pw[:lb], b_bar).real
    s_idx = np.arange(lb)[:, None]
    t_idx = np.arange(lb)[None, :]
    lag = np.clip(t_idx - s_idx, 0, lb - 1)
    causal = jnp.asarray((t_idx >= s_idx).astype(np.float32))
    toe = kern[lag] * causal[:, :, None, None, None]
    toe = toe.transpose(2, 0, 4, 1, 3).reshape(S5_GROUPS, lb * S5_GROUP_CH, lb * S5_GROUP_CH)
    cp = c_mat[None] * pw[1:, :, None, :]
    emit = jnp.concatenate([cp.real, -cp.imag], axis=-1)
    emit = emit.transpose(1, 3, 0, 2).reshape(S5_GROUPS, 2 * S5_STATE, lb * S5_GROUP_CH)
    bp = b_bar[None] * pw[:lb][::-1][:, :, :, None]
    fold = jnp.concatenate([bp.real, bp.imag], axis=2)
    fold = fold.transpose(1, 0, 3, 2).reshape(S5_GROUPS, lb * S5_GROUP_CH, 2 * S5_STATE)
    dtile = jnp.tile(d_skip.astype(F32), (1, lb)).reshape(S5_GROUPS, 1, lb * S5_GROUP_CH)
    return toe, emit, fold, dtile, lam_dt


def _s5_kernel(u_ref, toe_hi, toe_lo, emit_hi, emit_lo, fold_hi, fold_lo, d_ref, pw_re, pw_im, y_ref,
               *, bsz, n_rows):
    steps = max(1, int(math.ceil(math.log2(n_rows))))
    for b in range(bsz):
        rs = slice(b * n_rows, (b + 1) * n_rows)
        u = u_ref[0, rs, :]
        x = _dot3(u, fold_hi[0], fold_lo[0])
        for j in range(steps):
            xs = _shift_rows(x, 2 ** j, 0.0)
            x = x + pw_re[0, j:j + 1, :] * xs + pw_im[0, j:j + 1, :] * pltpu.roll(xs, S5_STATE, 1)
        x_prev = _shift_rows(x, 1, 0.0)
        y = _dot3(u, toe_hi[0], toe_lo[0]) + _dot3(x_prev, emit_hi[0], emit_lo[0]) + d_ref[0] * u
        y_ref[0, rs, :] = y


def _s5(p, a_re, a_im, log_dt, b_re, b_im, c_re, c_im, d_skip, bsz, seq):
    n = bsz * seq
    lb = S5_BLOCK
    rows = n // lb
    n_rows = seq // lb
    toe, emit, fold, dtile, lam_dt = _s5_tables(a_re, a_im, log_dt, b_re, b_im, c_re, c_im, d_skip)
    steps = max(1, int(math.ceil(math.log2(n_rows))))
    pw = jnp.exp(lam_dt[:, None, :] * (lb * 2.0 ** jnp.arange(steps, dtype=F32))[None, :, None])
    pw_re = jnp.concatenate([pw.real, pw.real], axis=-1)
    pw_im = jnp.concatenate([-pw.imag, pw.imag], axis=-1)
    su = p[:, COL_SU:COL_SU + S5_WIDTH]
    u = su.reshape(rows, lb, S5_GROUPS, S5_GROUP_CH).transpose(2, 0, 1, 3).reshape(S5_GROUPS, rows, lb * S5_GROUP_CH)
    mats = []
    for m in (toe, emit, fold):
        hi = m.astype(BF16)
        mats += [hi, (m - hi.astype(F32)).astype(BF16)]
    gspec = lambda a: pl.BlockSpec((1,) + a.shape[1:], lambda g: (g, 0, 0))
    args = [u] + mats + [dtile, pw_re, pw_im]
    y = pl.pallas_call(
        functools.partial(_s5_kernel, bsz=bsz, n_rows=n_rows),
        grid=(S5_GROUPS,),
        in_specs=[gspec(a) for a in args],
        out_specs=gspec(u),
        out_shape=jax.ShapeDtypeStruct(u.shape, F32),
        compiler_params=_cparams(("parallel",)),
        name="s5",
    )(*args)
    return y.reshape(S5_GROUPS, rows, lb, S5_GROUP_CH).transpose(1, 2, 0, 3).reshape(n, S5_WIDTH)


def _lru_kernel(lx_ref, lg_ref, cw_ref, cb_ref, wg_ref, bg_ref, sp_ref, o_ref, tail_ref, h_ref, *, ts):
    @pl.when(pl.program_id(1) == 0)
    def _():
        tail_ref[...] = jnp.zeros_like(tail_ref)
        h_ref[...] = jnp.zeros_like(h_ref)

    x = lx_ref[...]
    xe = jnp.concatenate([tail_ref[...], x], axis=0)
    xc = cb_ref[...] + cw_ref[LRU_CONV - 1:LRU_CONV, :] * x
    for j in range(1, LRU_CONV):
        xc = xc + cw_ref[LRU_CONV - 1 - j:LRU_CONV - j, :] * pltpu.roll(xe, j, 0)[8:, :]
    tail_ref[...] = x[ts - 8:, :]
    gates = jax.nn.sigmoid(_dot3(xc, wg_ref[0], wg_ref[1]) + bg_ref[...])
    r = gates[:, :LRU_WIDTH]
    ig = gates[:, LRU_WIDTH:]
    log_a = -LRU_C * r * sp_ref[...]
    a = jnp.exp(log_a)
    b = jnp.sqrt(jnp.maximum(1.0 - jnp.exp(2.0 * log_a), 0.0)) * (ig * xc)
    k = 1
    while k < ts:
        b = b + a * _shift_rows(b, k, 0.0)
        a = a * _shift_rows(a, k, 1.0)
        k *= 2
    h = b + a * h_ref[0:1, :]
    h_ref[0:1, :] = h[ts - 1:ts, :]
    o_ref[...] = h * jax.nn.gelu(lg_ref[...])


def _block_diag(w):
    nb, bw, _ = w.shape
    eye = jnp.eye(nb, dtype=w.dtype)
    return (eye[:, None, :, None] * w[:, :, None, :]).reshape(nb * bw, nb * bw)


def _lru(p, conv_w, conv_b, wa, ba, wx, bx, a_param, bsz, seq):
    ts = min(512, seq)
    nt = seq // ts
    wg = jnp.concatenate([_block_diag(wa), _block_diag(wx)], axis=1)
    wg_hi = wg.astype(BF16)
    wg2 = jnp.stack([wg_hi, (wg - wg_hi.astype(F32)).astype(BF16)])
    bg = jnp.concatenate([ba, bx]).reshape(1, -1)
    sp = jax.nn.softplus(a_param.astype(F32)).reshape(1, -1)
    col = lambda cb: pl.BlockSpec((ts, LRU_WIDTH), lambda b, t: (b * nt + t, cb))
    full = lambda a: pl.BlockSpec(a.shape, lambda b, t: (0,) * a.ndim)
    cb2 = conv_b.reshape(1, -1)
    return pl.pallas_call(
        functools.partial(_lru_kernel, ts=ts),
        grid=(bsz, nt),
        in_specs=[col(COL_LX // 512), col(COL_LG // 512), full(conv_w), full(cb2), full(wg2), full(bg), full(sp)],
        out_specs=pl.BlockSpec((ts, LRU_WIDTH), lambda b, t: (b * nt + t, 0)),
        out_shape=jax.ShapeDtypeStruct((bsz * seq, LRU_WIDTH), F32),
        scratch_shapes=[pltpu.VMEM((8, LRU_WIDTH), F32), pltpu.VMEM((8, LRU_WIDTH), F32)],
        compiler_params=_cparams(("parallel", "arbitrary")),
        name="rglru",
    )(p, p, conv_w, cb2, wg2, bg, sp)


def _layer_norm(x, w, b):
    xc = x - jnp.mean(x, axis=-1, keepdims=True)
    var = jnp.mean(xc * xc, axis=-1, keepdims=True)
    return xc * lax.rsqrt(var + LN_EPS) * w + b


def _merge_kernel(ya_ref, yb_ref, ys_ref, yd_ref, gl_ref, x_ref, wbr_ref, wout_ref, wglu_ref, bglu_ref,
                  bgate_ref, lnw_ref, lnb_ref, rw_ref, rb_ref, h_ref, hb_ref, lg_ref):
    yc = jax.nn.gelu(ys_ref[...])
    yc = yc * jax.nn.sigmoid(_dot(yc.astype(BF16), wglu_ref[...]) + bglu_ref[...])
    branches = (ya_ref[...].astype(BF16), yb_ref[...], yc.astype(BF16), yd_ref[...].astype(BF16))
    mix = None
    for bi, yb in enumerate(branches):
        cs = slice(bi * D_MODEL, (bi + 1) * D_MODEL)
        gate = jax.nn.sigmoid(gl_ref[:, cs] + bgate_ref[:, cs])
        term = gate * _dot(yb, wbr_ref[bi])
        mix = term if mix is None else mix + term
    mo = _dot(mix.astype(BF16), wout_ref[...])
    h = _layer_norm(DEEPNORM_ALPHA * x_ref[...] + mo, lnw_ref[...], lnb_ref[...])
    h_ref[...] = h
    h_hi, h_lo = _split(h)
    hb_ref[...] = h_hi
    hcat = jnp.concatenate([h_hi, h_lo, h_hi], axis=-1)
    lg_ref[...] = _dot_nt(rw_ref[...], hcat) + rb_ref[...]


def _merge(ya, yb, ys, yd, gl, x, w_branch, w_out, w_glu, b_glu, b_gate, ln_w, ln_b, router_w, router_b):
    n = x.shape[0]
    tm = min(256, n)
    rwt = router_w.T.astype(F32)
    rw_hi = rwt.astype(BF16)
    rw_lo = (rwt - rw_hi.astype(F32)).astype(BF16)
    rw3 = jnp.concatenate([rw_hi, rw_hi, rw_lo], axis=1)
    row = lambda w: pl.BlockSpec((tm, w), lambda i: (i, 0))
    full = lambda a: pl.BlockSpec(a.shape, lambda i: (0,) * a.ndim)
    consts = [w_branch.astype(BF16), w_out.astype(BF16), w_glu.astype(BF16), b_glu.reshape(1, -1),
              b_gate.reshape(1, -1), ln_w.reshape(1, -1), ln_b.reshape(1, -1), rw3, router_b.reshape(-1, 1)]
    return pl.pallas_call(
        _merge_kernel,
        grid=(n // tm,),
        in_specs=[row(512), row(512), row(512), row(512), row(N_BRANCH * D_MODEL), row(D_MODEL)]
                 + [full(c) for c in consts],
        out_specs=[row(D_MODEL), row(D_MODEL), pl.BlockSpec((N_EXPERTS, tm), lambda i: (0, i))],
        out_shape=[jax.ShapeDtypeStruct((n, D_MODEL), F32), jax.ShapeDtypeStruct((n, D_MODEL), BF16),
                   jax.ShapeDtypeStruct((N_EXPERTS, n), F32)],
        compiler_params=_cparams(("parallel",)),
        name="merge",
    )(ya, yb, ys, yd, gl, x, *consts)


def _route_kernel(lg_ref, tri_ref, idx_ref, w_ref, rank_ref, cnt_ref, carry_ref, *, tr):
    @pl.when(pl.program_id(0) == 0)
    def _():
        carry_ref[...] = jnp.zeros_like(carry_ref)

    l = lg_ref[...]
    eidx = lax.broadcasted_iota(jnp.int32, l.shape, 0)
    vals, hots = [], []
    for _ in range(TOP_K):
        m = jnp.max(l, axis=0, keepdims=True)
        first = jnp.min(jnp.where(l == m, eidx, N_EXPERTS), axis=0, keepdims=True)
        hot = eidx == first
        l = jnp.where(hot, -jnp.inf, l)
        vals.append(m)
        hots.append(hot)
        idx_ref[len(vals) - 1:len(vals), :] = first
    ex = [jnp.exp(v - vals[0]) for v in vals]
    den = ex[0] + ex[1] + ex[2] + ex[3]
    for k in range(TOP_K):
        w_ref[k:k + 1, :] = ex[k] / den
    member = jnp.zeros(l.shape, F32)
    for hot in hots:
        member = member + hot.astype(F32)
    before = _dot(member.astype(BF16), tri_ref[...]) + carry_ref[:, 0:1]
    for k in range(TOP_K):
        rank = jnp.sum(jnp.where(hots[k], before, 0.0), axis=0, keepdims=True)
        rank_ref[k:k + 1, :] = rank.astype(jnp.int32)
    carry_ref[...] = carry_ref[...] + jnp.sum(member, axis=1, keepdims=True)
    cnt_ref[...] = carry_ref[...]
    idx_ref[TOP_K:, :] = jnp.zeros((8 - TOP_K, tr), jnp.int32)
    w_ref[TOP_K:, :] = jnp.zeros((8 - TOP_K, tr), F32)
    rank_ref[TOP_K:, :] = jnp.zeros((8 - TOP_K, tr), jnp.int32)


def _route(logits_t):
    n = logits_t.shape[1]
    tr = min(512, n)
    tri = jnp.asarray(np.triu(np.ones((tr, tr), np.float32), 1), BF16)
    tok = pl.BlockSpec((8, tr), lambda i: (0, i))
    return pl.pallas_call(
        functools.partial(_route_kernel, tr=tr),
        grid=(n // tr,),
        in_specs=[pl.BlockSpec((N_EXPERTS, tr), lambda i: (0, i)), pl.BlockSpec((tr, tr), lambda i: (0, 0))],
        out_specs=[tok, tok, tok, pl.BlockSpec((N_EXPERTS, 128), lambda i: (0, 0))],
        out_shape=[jax.ShapeDtypeStruct((8, n), jnp.int32), jax.ShapeDtypeStruct((8, n), F32),
                   jax.ShapeDtypeStruct((8, n), jnp.int32), jax.ShapeDtypeStruct((N_EXPERTS, 128), F32)],
        scratch_shapes=[pltpu.VMEM((N_EXPERTS, 128), F32)],
        compiler_params=_cparams(("arbitrary",)),
        name="route",
    )(logits_t, tri)


def _expert_kernel(be_ref, nu_ref, x_ref, w1_ref, b1_ref, w2_ref, b2_ref, o_ref):
    @pl.when(pl.program_id(0) < nu_ref[0])
    def _():
        h1 = _dot(x_ref[...], w1_ref[0]) + b1_ref[0]
        glu = jnp.minimum(h1[:, :EXPERT_FF], SWIGLU_LIMIT)
        lin = jnp.clip(h1[:, EXPERT_FF:], -SWIGLU_LIMIT, SWIGLU_LIMIT)
        act = glu * jax.nn.sigmoid(SWIGLU_ALPHA * glu) * (lin + 1.0)
        o_ref[...] = (_dot(act.astype(BF16), w2_ref[0]) + b2_ref[0]).astype(o_ref.dtype)

    @pl.when(pl.program_id(0) >= nu_ref[0])
    def _():
        o_ref[...] = jnp.zeros_like(o_ref)


def _experts(x_slots, block_expert, n_used, w1, b1, w2, b2):
    n_slots = x_slots.shape[0]
    tm = EXPERT_TILE
    n_blocks = n_slots // tm
    grid_spec = pltpu.PrefetchScalarGridSpec(
        num_scalar_prefetch=2,
        grid=(n_blocks,),
        in_specs=[pl.BlockSpec((tm, D_MODEL), lambda i, be, nu: (i, 0)),
                  pl.BlockSpec((1, D_MODEL, 2 * EXPERT_FF), lambda i, be, nu: (be[i], 0, 0)),
                  pl.BlockSpec((1, 1, 2 * EXPERT_FF), lambda i, be, nu: (be[i], 0, 0)),
                  pl.BlockSpec((1, EXPERT_FF, D_MODEL), lambda i, be, nu: (be[i], 0, 0)),
                  pl.BlockSpec((1, 1, D_MODEL), lambda i, be, nu: (be[i], 0, 0))],
        out_specs=pl.BlockSpec((tm, D_MODEL), lambda i, be, nu: (i, 0)),
    )
    return pl.pallas_call(
        _expert_kernel,
        grid_spec=grid_spec,
        out_shape=jax.ShapeDtypeStruct((n_slots, D_MODEL), BF16),
        compiler_params=_cparams(("arbitrary",)),
        name="experts",
    )(block_expert, n_used, x_slots, w1, b1.reshape(N_EXPERTS, 1, -1), w2, b2.reshape(N_EXPERTS, 1, -1))


def _combine_kernel(h_ref, y_ref, w_ref, lnw_ref, lnb_ref, o_ref):
    acc = DEEPNORM_ALPHA * h_ref[...]
    for k in range(TOP_K):
        acc = acc + w_ref[:, k:k + 1] * y_ref[k].astype(F32)
    o_ref[...] = _layer_norm(acc, lnw_ref[...], lnb_ref[...])


def _combine(h, y_sel, w_tok, ln_w, ln_b):
    n = h.shape[0]
    tm = min(256, n)
    lnw = ln_w.reshape(1, -1)
    lnb = ln_b.reshape(1, -1)
    return pl.pallas_call(
        _combine_kernel,
        grid=(n // tm,),
        in_specs=[pl.BlockSpec((tm, D_MODEL), lambda i: (i, 0)),
                  pl.BlockSpec((TOP_K, tm, D_MODEL), lambda i: (0, i, 0)),
                  pl.BlockSpec((tm, TOP_K), lambda i: (i, 0)),
                  pl.BlockSpec(lnw.shape, lambda i: (0, 0)), pl.BlockSpec(lnb.shape, lambda i: (0, 0))],
        out_specs=pl.BlockSpec((tm, D_MODEL), lambda i: (i, 0)),
        out_shape=jax.ShapeDtypeStruct((n, D_MODEL), F32),
        compiler_params=_cparams(("parallel",)),
        name="combine",
    )(h, y_sel, w_tok, lnw, lnb)


def _moe(h, h16, logits_t, w1, b1, w2, b2, ln_w, ln_b):
    n = h.shape[0]
    tm = EXPERT_TILE
    idx8, w8, rank8, cnt = _route(logits_t)
    idx, w_top, rank = idx8[:TOP_K], w8[:TOP_K], rank8[:TOP_K]
    counts = cnt[:, 0].astype(jnp.int32)
    padded = ((counts + tm - 1) // tm) * tm
    p_end = jnp.cumsum(padded)
    p_start = p_end - padded
    dest = p_start[idx] + rank
    n_slots = n * TOP_K + N_EXPERTS * tm
    n_blocks = n_slots // tm
    block_expert = jnp.minimum(
        jnp.searchsorted(p_end, jnp.arange(n_blocks, dtype=jnp.int32) * tm, side='right'),
        N_EXPERTS - 1).astype(jnp.int32)
    n_used = (p_end[-1] // tm).astype(jnp.int32).reshape(1)
    tok = jnp.broadcast_to(jnp.arange(n, dtype=jnp.int32)[None, :], (TOP_K, n))
    slot_token = jnp.zeros((n_slots,), jnp.int32).at[dest.reshape(-1)].set(tok.reshape(-1))
    x_slots = jnp.take(h16, slot_token, axis=0)
    y_slots = _experts(x_slots, block_expert, n_used, w1, b1, w2, b2)
    y_sel = jnp.take(y_slots, dest, axis=0)
    return _combine(h, y_sel, w_top.T, ln_w, ln_b)


def _mixer_weight(w_in):
    o = np.cumsum((512, 512, 512, 512, MLA_Q_RANK, MLA_KV_RANK + MLA_ROPE, S5_WIDTH, LRU_WIDTH, LRU_WIDTH))
    hgrn, cq, ckv = w_in[:, :o[3]], w_in[:, o[3]:o[4]], w_in[:, o[4]:o[4] + MLA_KV_RANK]
    kpe = w_in[:, o[4] + MLA_KV_RANK:o[5]]
    su, lx, lg = w_in[:, o[5]:o[6]], w_in[:, o[6]:o[7]], w_in[:, o[7]:o[8]]
    pad = jnp.zeros((D_MODEL, 128 - MLA_ROPE), w_in.dtype)
    w_mix = jnp.concatenate([hgrn, su, lx, lg, cq, ckv, kpe, pad, _rot_half_cols(kpe), pad], axis=1)
    return w_mix, w_in[:, o[8]:]


def _layer(x, positions, lb, bsz, seq, w_in, b_gate, hgrn_norm_w, mla_q_norm_w, mla_wq_b, mla_kv_norm_w, mla_wkv_b,
           s5_a_re, s5_a_im, s5_log_dt, s5_b_re, s5_b_im, s5_c_re, s5_c_im, s5_d, s5_w_glu, s5_b_glu,
           lru_conv_w, lru_conv_b, lru_wa, lru_ba, lru_wx, lru_bx, lru_a_param,
           w_branch, w_out, ln1_w, ln1_b, ln2_w, ln2_b, router_w, router_b, moe_w1, moe_b1, moe_w2, moe_b2):
    w_mix, w_gl = _mixer_weight(w_in)
    p = _matmul(x, w_mix.astype(BF16), 1024, MIX_WIDTH // 3, F32)
    gl = _matmul(x, w_gl.astype(BF16), 1024, 1024, F32)
    y_a = _hgrn(p, lb, hgrn_norm_w, bsz, seq)
    qn, qp, kn, kp, v = _mla_proj(p, positions, mla_q_norm_w, mla_wq_b, mla_kv_norm_w, mla_wkv_b)
    y_b = _flash(qn, qp, kn, kp, v, bsz, seq)
    y_s = _s5(p, s5_a_re, s5_a_im, s5_log_dt, s5_b_re, s5_b_im, s5_c_re, s5_c_im, s5_d, bsz, seq)
    y_d = _lru(p, lru_conv_w, lru_conv_b, lru_wa, lru_ba, lru_wx, lru_bx, lru_a_param, bsz, seq)
    h, h16, logits_t = _merge(y_a, y_b, y_s, y_d, gl, x, w_branch, w_out, s5_w_glu, s5_b_glu, b_gate,
                              ln1_w, ln1_b, router_w, router_b)
    return _moe(h, h16, logits_t, moe_w1.astype(BF16), moe_b1, moe_w2.astype(BF16), moe_b2, ln2_w, ln2_b)


def kernel(x, positions, w_in, b_gate, hgrn_lb_logits, hgrn_norm_w, mla_q_norm_w, mla_wq_b, mla_kv_norm_w, mla_wkv_b, s5_a_re, s5_a_im, s5_log_dt, s5_b_re, s5_b_im, s5_c_re, s5_c_im, s5_d, s5_w_glu, s5_b_glu, lru_conv_w, lru_conv_b, lru_wa, lru_ba, lru_wx, lru_bx, lru_a_param, w_branch, w_out, ln1_w, ln1_b, ln2_w, ln2_b, router_w, router_b, moe_w1, moe_b1, moe_w2, moe_b2):
    bsz, seq, _ = x.shape
    probs = jax.nn.softmax(hgrn_lb_logits.astype(F32), axis=0)
    lower_bounds = jnp.cumsum(probs, axis=0) - probs[0:1]
    per_layer = (w_in, b_gate, hgrn_norm_w, mla_q_norm_w, mla_wq_b, mla_kv_norm_w, mla_wkv_b,
                 s5_a_re, s5_a_im, s5_log_dt, s5_b_re, s5_b_im, s5_c_re, s5_c_im, s5_d, s5_w_glu, s5_b_glu,
                 lru_conv_w, lru_conv_b, lru_wa, lru_ba, lru_wx, lru_bx, lru_a_param,
                 w_branch, w_out, ln1_w, ln1_b, ln2_w, ln2_b, router_w, router_b, moe_w1, moe_b1, moe_w2, moe_b2)
    xf = x.reshape(bsz * seq, D_MODEL)
    for l in range(DEPTH):
        xf = _layer(xf, positions, lower_bounds[l], bsz, seq, *[a[l] for a in per_layer])
    return xf.reshape(bsz, seq, D_MODEL)
```

```python
import functools
import math

import numpy as np
import jax
import jax.numpy as jnp
from jax import lax
from jax.experimental import pallas as pl
from jax.experimental.pallas import tpu as pltpu

F32 = jnp.float32
BF16 = jnp.bfloat16

D_MODEL = 1024
DEPTH = 2
CHUNK = 64

HGRN_HEADS = 4
HGRN_DK = 128
HGRN_WIDTH = 512
HGRN_F_MIN = 1e-30
HGRN_CHUNK = 64
HGRN_LEVELS = (1, 2, 4, 8, 16, 32)

MLA_HEADS = 4
MLA_Q_RANK = 256
MLA_KV_RANK = 128
MLA_NOPE = 128
MLA_ROPE = 64
MLA_V = 128
MLA_QK = MLA_NOPE + MLA_ROPE
ROPE_THETA = 10000.0

S5_GROUPS = 32
S5_GROUP_CH = 16
S5_STATE = 64
S5_WIDTH = 512
S5_BLOCK = 16

LRU_WIDTH = 512
LRU_BLOCKS = 8
LRU_BLOCK_W = 64
LRU_CONV = 4
LRU_C = 8.0

N_BRANCH = 4
BRANCH_WIDTH = 512

N_EXPERTS = 32
TOP_K = 4
EXPERT_FF = 1024
SWIGLU_ALPHA = 1.702
SWIGLU_LIMIT = 7.0
EXPERT_TILE = 512

DEEPNORM_ALPHA = (2.0 * DEPTH) ** 0.25
LN_EPS = 1e-5
RMS_EPS = 1e-6

COL_HQ, COL_HF, COL_HI, COL_HG = 0, 512, 1024, 1536
COL_SU, COL_LX, COL_LG = 2048, 2560, 3072
COL_CQ, COL_CKV, COL_KPE, COL_KPER = 3584, 3840, 3968, 4096
MIX_WIDTH = 4224

VMEM_LIMIT = 56 * 1024 * 1024


def _cparams(sem):
    return pltpu.CompilerParams(dimension_semantics=sem, vmem_limit_bytes=VMEM_LIMIT)


def _dot(a, b):
    return jnp.dot(a, b, preferred_element_type=F32)


def _dot_nt(a, b):
    return lax.dot_general(a, b, (((1,), (1,)), ((), ())), preferred_element_type=F32)


def _dot_tn(a, b):
    return lax.dot_general(a, b, (((0,), (0,)), ((), ())), preferred_element_type=F32)


def _split(x):
    hi = x.astype(BF16)
    lo = (x - hi.astype(F32)).astype(BF16)
    return hi, lo


def _dot3(a, b_hi, b_lo):
    a_hi, a_lo = _split(a)
    return _dot(a_hi, b_hi) + (_dot(a_lo, b_hi) + _dot(a_hi, b_lo))


def _shift_rows(x, k, fill):
    rows = lax.broadcasted_iota(jnp.int32, x.shape, 0)
    return jnp.where(rows >= k, pltpu.roll(x, k, 0), fill)


def _mm_kernel(x_ref, w_ref, o_ref):
    o_ref[...] = _dot(x_ref[...].astype(BF16), w_ref[...]).astype(o_ref.dtype)


def _matmul(x, w, tm, tn, out_dtype):
    n, k = x.shape
    m = w.shape[1]
    tm = min(tm, n)
    return pl.pallas_call(
        _mm_kernel,
        grid=(n // tm, m // tn),
        in_specs=[pl.BlockSpec((tm, k), lambda i, j: (i, 0)),
                  pl.BlockSpec((k, tn), lambda i, j: (0, j))],
        out_specs=pl.BlockSpec((tm, tn), lambda i, j: (i, j)),
        out_shape=jax.ShapeDtypeStruct((n, m), out_dtype),
        compiler_params=_cparams(("parallel", "arbitrary")),
        name="in_proj",
    )(x, w)


def _hgrn_tables():
    c = HGRN_CHUNK
    t = np.arange(c)
    windows = [np.tril(np.ones((c, c), bool)),
               np.triu(np.ones((c, c), bool), 1)]
    masks = [np.eye(c, dtype=bool)]
    for h in HGRN_LEVELS:
        blk = t // h
        odd = blk % 2 == 1
        masks.append(odd[:, None] & (blk[None, :] == blk[:, None] - 1))
        if h > 1:
            windows.append((t[None, :] >= (h * blk)[:, None]) & (t[None, :] <= t[:, None]))
            windows.append((t[None, :] > t[:, None]) & (t[None, :] <= (h * blk + h - 1)[:, None]))
    return np.concatenate(windows, 0).astype(np.float32), np.stack(masks).astype(np.float32)


def _hgrn_kernel(hq_ref, hf_ref, hi_ref, hg_ref, lb_ref, nw_ref, win_ref, mask_ref,
                 o_ref, state_ref, *, n_chunks):
    c = HGRN_CHUNK
    heads = range(HGRN_HEADS)
    hs = [slice(h * HGRN_DK, (h + 1) * HGRN_DK) for h in heads]
    n_lvl = len(HGRN_LEVELS) + 1

    @pl.when(pl.program_id(1) == 0)
    def _():
        state_ref[...] = jnp.zeros_like(state_ref)

    def chunk_body(ci, carry):
        rows = pl.ds(pl.multiple_of(ci * c, c), c)
        lb = lb_ref[...]
        z = hf_ref[rows, :]
        f = jnp.maximum(lb + (1.0 - lb) * jax.nn.sigmoid(z), HGRN_F_MIN)
        k = (1.0 - lb) * jax.nn.sigmoid(-z)
        q = jax.nn.silu(hq_ref[rows, :])
        v16 = hi_ref[rows, :].astype(BF16)
        lf_hi, lf_lo = _split(jnp.log(f))
        win = win_ref[...]
        e = jnp.exp(_dot(win, lf_hi) + _dot(win, lf_lo))
        blk = lambda w: e[w * c:(w + 1) * c]
        k16 = k.astype(BF16)
        qs = [q.astype(BF16), (q * f).astype(BF16)] + [(q * blk(2 * w)).astype(BF16) for w in range(1, n_lvl - 1)]
        ks = [k16, k16] + [(k * blk(2 * w + 1)).astype(BF16) for w in range(1, n_lvl - 1)]
        q_in = (q * blk(0)).astype(BF16)
        k_out = (k * blk(1)).astype(BF16)
        decay = e[c - 1:c, :]
        scores = [None] * HGRN_HEADS
        for lv in range(n_lvl):
            s_l = [_dot_nt(qs[lv][:, hs[h]], ks[lv][:, hs[h]]) for h in heads]
            for h in heads:
                term = mask_ref[lv] * s_l[h]
                scores[h] = term if scores[h] is None else scores[h] + term
        st = [state_ref[h] for h in heads]
        o = [_dot(scores[h].astype(BF16), v16[:, hs[h]]) + _dot_nt(q_in[:, hs[h]], st[h].astype(BF16)) for h in heads]
        upd = [_dot_tn(v16[:, hs[h]], k_out[:, hs[h]]) for h in heads]
        for h in heads:
            state_ref[h] = st[h] * decay[:, hs[h]] + upd[h]
        o = [o[h] * lax.rsqrt(jnp.mean(o[h] * o[h], axis=-1, keepdims=True) + RMS_EPS) for h in heads]
        o_ref[rows, :] = jnp.concatenate(o, axis=-1) * nw_ref[...] * jax.nn.silu(hg_ref[rows, :])
        return carry

    lax.fori_loop(0, n_chunks, chunk_body, 0)


def _hgrn(p, lb, norm_w, bsz, seq):
    t = min(256, seq)
    nt = seq // t
    win, mask = _hgrn_tables()
    col = lambda cb: pl.BlockSpec((t, HGRN_WIDTH), lambda b, i: (b * nt + i, cb))
    const2 = lambda shape: pl.BlockSpec(shape, lambda b, i: (0,) * len(shape))
    return pl.pallas_call(
        functools.partial(_hgrn_kernel, n_chunks=t // HGRN_CHUNK),
        grid=(bsz, nt),
        in_specs=[col(COL_HQ // 512), col(COL_HF // 512), col(COL_HI // 512), col(COL_HG // 512),
                  const2((1, HGRN_WIDTH)), const2((1, HGRN_WIDTH)), const2(win.shape), const2(mask.shape)],
        out_specs=pl.BlockSpec((t, HGRN_WIDTH), lambda b, i: (b * nt + i, 0)),
        out_shape=jax.ShapeDtypeStruct((bsz * seq, HGRN_WIDTH), F32),
        scratch_shapes=[pltpu.VMEM((HGRN_HEADS, HGRN_DK, HGRN_DK), F32)],
        compiler_params=_cparams(("parallel", "arbitrary")),
        name="hgrn",
    )(p, p, p, p, lb.reshape(1, -1), norm_w.reshape(1, -1), jnp.asarray(win, BF16), jnp.asarray(mask))


def _rms(x, w):
    ms = jnp.mean(x * x, axis=-1, keepdims=True)
    return x * lax.rsqrt(ms + RMS_EPS) * w


def _mla_proj_kernel(cq_ref, ckv_ref, kpe_ref, kper_ref, ang_ref, qnw_ref, kvnw_ref,
                     wqn_ref, wqp_ref, wqpr_ref, wkn_ref, wv_ref,
                     qn_ref, qp_ref, kn_ref, kp_ref, v_ref):
    ang = ang_ref[...]
    cos = jnp.cos(ang)
    sin = jnp.sin(ang)
    qn = _rms(cq_ref[...], qnw_ref[...]).astype(BF16)
    cos4 = jnp.concatenate([cos] * MLA_HEADS, axis=-1)
    sin4 = jnp.concatenate([sin] * MLA_HEADS, axis=-1)
    qn_ref[...] = _dot(qn, wqn_ref[...]).astype(BF16)
    qp_ref[...] = (_dot(qn, wqp_ref[...]) * cos4 + _dot(qn, wqpr_ref[...]) * sin4).astype(BF16)
    cn = _rms(ckv_ref[...], kvnw_ref[...]).astype(BF16)
    kn_ref[...] = _dot(cn, wkn_ref[...]).astype(BF16)
    v_ref[0] = _dot_nt(wv_ref[...], cn).astype(BF16)
    kp_ref[...] = (kpe_ref[...] * cos + kper_ref[...] * sin).astype(BF16)


def _rot_half_cols(w):
    half = w.shape[-1] // 2
    return jnp.concatenate([-w[..., half:], w[..., :half]], axis=-1)


def _mla_proj(p, positions, q_norm_w, wq_b, kv_norm_w, wkv_b, tm):
    n = p.shape[0]
    scale = MLA_QK ** -0.5
    wq = wq_b.reshape(MLA_Q_RANK, MLA_HEADS, MLA_QK) * scale
    w_nope = wq[:, :, :MLA_NOPE].reshape(MLA_Q_RANK, MLA_HEADS * MLA_NOPE)
    w_pe = wq[:, :, MLA_NOPE:]
    pad = jnp.zeros((MLA_Q_RANK, MLA_HEADS, 128 - MLA_ROPE), F32)
    w_pe_p = jnp.concatenate([w_pe, pad], axis=-1).reshape(MLA_Q_RANK, MLA_HEADS * 128)
    w_per_p = jnp.concatenate([_rot_half_cols(w_pe), pad], axis=-1).reshape(MLA_Q_RANK, MLA_HEADS * 128)
    wkv = wkv_b.reshape(MLA_KV_RANK, MLA_HEADS, MLA_NOPE + MLA_V)
    w_kn = wkv[:, :, :MLA_NOPE].reshape(MLA_KV_RANK, MLA_HEADS * MLA_NOPE)
    w_v = wkv[:, :, MLA_NOPE:].reshape(MLA_KV_RANK, MLA_HEADS * MLA_V)
    inv_freq = ROPE_THETA ** (-jnp.arange(0, MLA_ROPE, 2, dtype=F32) / MLA_ROPE)
    freq_lane = jnp.concatenate([inv_freq, inv_freq, jnp.zeros((128 - MLA_ROPE,), F32)])
    ang = positions.reshape(n, 1).astype(F32) * freq_lane[None, :]

    row = lambda w, cb: pl.BlockSpec((tm, w), lambda i: (i, cb))
    full = lambda a: pl.BlockSpec(a.shape, lambda i: (0,) * a.ndim)
    weights = [w_nope.astype(BF16), w_pe_p.astype(BF16), w_per_p.astype(BF16), w_kn.astype(BF16),
               w_v.T.astype(BF16)]
    qnw = q_norm_w.reshape(1, -1)
    kvnw = kv_norm_w.reshape(1, -1)
    outs = pl.pallas_call(
        _mla_proj_kernel,
        grid=(n // tm,),
        in_specs=[row(256, COL_CQ // 256), row(128, COL_CKV // 128), row(128, COL_KPE // 128),
                  row(128, COL_KPER // 128), row(128, 0), full(qnw), full(kvnw)] + [full(w) for w in weights],
        out_specs=[row(512, 0), row(512, 0), row(512, 0), row(128, 0),
                   pl.BlockSpec((1, 512, tm), lambda i: (i, 0, 0))],
        out_shape=[jax.ShapeDtypeStruct((n, 512), BF16), jax.ShapeDtypeStruct((n, 512), BF16),
                   jax.ShapeDtypeStruct((n, 512), BF16), jax.ShapeDtypeStruct((n, 128), BF16),
                   jax.ShapeDtypeStruct((n // tm, 512, tm), BF16)],
        compiler_params=_cparams(("parallel",)),
        name="mla_proj",
    )(p, p, p, p, ang, qnw, kvnw, *weights)
    return outs


def _flash_kernel(qn_ref, qp_ref, kn_ref, kp_ref, vt_ref, o_ref, acc_ref, *, tq):
    i = pl.program_id(1)
    heads = range(MLA_HEADS)
    hs = [slice(h * 128, (h + 1) * 128) for h in heads]
    acc_ref[...] = jnp.zeros_like(acc_ref)

    def step(j, stats, mask):
        r0 = pl.multiple_of(j * tq, tq)
        kp = kp_ref[pl.ds(r0, tq), :]
        s = [_dot_nt(jnp.concatenate([kn_ref[pl.ds(r0, tq), hs[h]], kp], axis=-1),
                     jnp.concatenate([qn_ref[:, hs[h]], qp_ref[:, hs[h]]], axis=-1)) for h in heads]
        p, new = [], []
        for h in heads:
            m, l = stats[h]
            sh = s[h] if mask is None else jnp.where(mask, s[h], -1e30)
            m_new = jnp.maximum(m, jnp.max(sh, axis=0, keepdims=True))
            alpha = jnp.exp(m - m_new)
            ph = jnp.exp(sh - m_new)
            p.append(ph.astype(BF16))
            new.append((m_new, alpha * l + jnp.sum(ph, axis=0, keepdims=True), alpha))
        vt = vt_ref[j]
        pv = [_dot(vt[hs[h], :], p[h]) for h in heads]
        for h in heads:
            acc_ref[h] = acc_ref[h] * new[h][2] + pv[h]
        return tuple((new[h][0], new[h][1]) for h in heads)

    init = tuple((jnp.full((1, tq), -1e30, F32), jnp.zeros((1, tq), F32)) for _ in heads)
    stats = lax.fori_loop(0, i, lambda j, c: step(j, c, None), init)
    shift = CHUNK.bit_length() - 1
    kc = lax.shift_right_logical(lax.broadcasted_iota(jnp.int32, (tq, tq), 0), shift)
    qc = lax.shift_right_logical(lax.broadcasted_iota(jnp.int32, (tq, tq), 1), shift)
    stats = step(i, stats, kc <= qc)
    for h in heads:
        o_ref[:, hs[h]] = (acc_ref[h] / stats[h][1]).T.astype(o_ref.dtype)


def _flash(qn, qp, kn, kp, vt, bsz, seq):
    tq = vt.shape[2]
    nq = seq // tq
    width = MLA_HEADS * 128
    qspec = pl.BlockSpec((tq, width), lambda b, i: (b * nq + i, 0))
    kspec = pl.BlockSpec((seq, width), lambda b, i: (b, 0))
    kpspec = pl.BlockSpec((seq, 128), lambda b, i: (b, 0))
    vspec = pl.BlockSpec((nq, width, tq), lambda b, i: (b, 0, 0))
    return pl.pallas_call(
        functools.partial(_flash_kernel, tq=tq),
        grid=(bsz, nq),
        in_specs=[qspec, qspec, kspec, kpspec, vspec],
        out_specs=qspec,
        out_shape=jax.ShapeDtypeStruct((bsz * seq, width), BF16),
        scratch_shapes=[pltpu.VMEM((MLA_HEADS, MLA_V, tq), F32)],
        compiler_params=_cparams(("parallel", "arbitrary")),
        name="flash",
    )(qn, qp, kn, kp, vt)


def _s5_tables(a_re, a_im, log_dt, b_re, b_im, c_re, c_im, d_skip):
    lb = S5_BLOCK
    lam = lax.complex(a_re.astype(F32), a_im.astype(F32))
    dt = jnp.exp(log_dt.astype(F32))[:, None]
    lam_dt = lam * dt
    lam_bar = jnp.exp(lam_dt)
    b_bar = ((lam_bar - 1.0) / lam)[..., None] * lax.complex(b_re.astype(F32), b_im.astype(F32))
    c_mat = lax.complex(c_re.astype(F32), c_im.astype(F32))
    tau = jnp.arange(lb + 1, dtype=F32)
    pw = jnp.exp(lam_dt[None] * tau[:, None, None])
    kern = jnp.einsum('gcn,tgn,gnd->tgcd', c_mat, pw[:lb], b_bar).real
    s_idx = np.arange(lb)[:, None]
    t_idx = np.arange(lb)[None, :]
    lag = np.clip(t_idx - s_idx, 0, lb - 1)
    causal = jnp.asarray((t_idx >= s_idx).astype(np.float32))
    toe = kern[lag] * causal[:, :, None, None, None]
    toe = toe.transpose(2, 0, 4, 1, 3).reshape(S5_GROUPS, lb * S5_GROUP_CH, lb * S5_GROUP_CH)
    cp = c_mat[None] * pw[1:, :, None, :]
    emit = jnp.concatenate([cp.real, -cp.imag], axis=-1)
    emit = emit.transpose(1, 3, 0, 2).reshape(S5_GROUPS, 2 * S5_STATE, lb * S5_GROUP_CH)
    bp = b_bar[None] * pw[:lb][::-1][:, :, :, None]
    fold = jnp.concatenate([bp.real, bp.imag], axis=2)
    fold = fold.transpose(1, 0, 3, 2).reshape(S5_GROUPS, lb * S5_GROUP_CH, 2 * S5_STATE)
    dtile = jnp.tile(d_skip.astype(F32), (1, lb)).reshape(S5_GROUPS, 1, lb * S5_GROUP_CH)
    return toe, emit, fold, dtile, lam_dt


def _s5_kernel(u_ref, toe_hi, toe_lo, emit_hi, emit_lo, fold_hi, fold_lo, d_ref, pw_re, pw_im, y_ref,
               *, bsz, n_rows):
    steps = max(1, int(math.ceil(math.log2(n_rows))))
    for b in range(bsz):
        rs = slice(b * n_rows, (b + 1) * n_rows)
        u = u_ref[0, rs, :]
        x = _dot3(u, fold_hi[0], fold_lo[0])
        for j in range(steps):
            xs = _shift_rows(x, 2 ** j, 0.0)
            x = x + pw_re[0, j:j + 1, :] * xs + pw_im[0, j:j + 1, :] * pltpu.roll(xs, S5_STATE, 1)
        x_prev = _shift_rows(x, 1, 0.0)
        y = _dot3(u, toe_hi[0], toe_lo[0]) + _dot3(x_prev, emit_hi[0], emit_lo[0]) + d_ref[0] * u
        y_ref[0, rs, :] = y


def _s5(p, a_re, a_im, log_dt, b_re, b_im, c_re, c_im, d_skip, bsz, seq):
    n = bsz * seq
    lb = S5_BLOCK
    rows = n // lb
    n_rows = seq // lb
    toe, emit, fold, dtile, lam_dt = _s5_tables(a_re, a_im, log_dt, b_re, b_im, c_re, c_im, d_skip)
    steps = max(1, int(math.ceil(math.log2(n_rows))))
    pw = jnp.exp(lam_dt[:, None, :] * (lb * 2.0 ** jnp.arange(steps, dtype=F32))[None, :, None])
    pw_re = jnp.concatenate([pw.real, pw.real], axis=-1)
    pw_im = jnp.concatenate([-pw.imag, pw.imag], axis=-1)
    su = p[:, COL_SU:COL_SU + S5_WIDTH]
    u = su.reshape(rows, lb, S5_GROUPS, S5_GROUP_CH).transpose(2, 0, 1, 3).reshape(S5_GROUPS, rows, lb * S5_GROUP_CH)
    mats = []
    for m in (toe, emit, fold):
        hi = m.astype(BF16)
        mats += [hi, (m - hi.astype(F32)).astype(BF16)]
    gspec = lambda a: pl.BlockSpec((1,) + a.shape[1:], lambda g: (g, 0, 0))
    args = [u] + mats + [dtile, pw_re, pw_im]
    y = pl.pallas_call(
        functools.partial(_s5_kernel, bsz=bsz, n_rows=n_rows),
        grid=(S5_GROUPS,),
        in_specs=[gspec(a) for a in args],
        out_specs=gspec(u),
        out_shape=jax.ShapeDtypeStruct(u.shape, F32),
        compiler_params=_cparams(("parallel",)),
        name="s5",
    )(*args)
    return y.reshape(S5_GROUPS, rows, lb, S5_GROUP_CH).transpose(1, 2, 0, 3).reshape(n, S5_WIDTH)


def _lru_kernel(lx_ref, lg_ref, cw_ref, cb_ref, wg_ref, bg_ref, sp_ref, o_ref, tail_ref, h_ref, *, ts):
    @pl.when(pl.program_id(1) == 0)
    def _():
        tail_ref[...] = jnp.zeros_like(tail_ref)
        h_ref[...] = jnp.zeros_like(h_ref)

    x = lx_ref[...]
    xe = jnp.concatenate([tail_ref[...], x], axis=0)
    xc = cb_ref[...] + cw_ref[LRU_CONV - 1:LRU_CONV, :] * x
    for j in range(1, LRU_CONV):
        xc = xc + cw_ref[LRU_CONV - 1 - j:LRU_CONV - j, :] * pltpu.roll(xe, j, 0)[8:, :]
    tail_ref[...] = x[ts - 8:, :]
    gates = jax.nn.sigmoid(_dot3(xc, wg_ref[0], wg_ref[1]) + bg_ref[...])
    r = gates[:, :LRU_WIDTH]
    ig = gates[:, LRU_WIDTH:]
    log_a = -LRU_C * r * sp_ref[...]
    a = jnp.exp(log_a)
    b = jnp.sqrt(jnp.maximum(1.0 - jnp.exp(2.0 * log_a), 0.0)) * (ig * xc)
    k = 1
    while k < ts:
        b = b + a * _shift_rows(b, k, 0.0)
        a = a * _shift_rows(a, k, 1.0)
        k *= 2
    h = b + a * h_ref[0:1, :]
    h_ref[0:1, :] = h[ts - 1:ts, :]
    o_ref[...] = h * jax.nn.gelu(lg_ref[...])


def _block_diag(w):
    nb, bw, _ = w.shape
    eye = jnp.eye(nb, dtype=w.dtype)
    return (eye[:, None, :, None] * w[:, :, None, :]).reshape(nb * bw, nb * bw)


def _lru(p, conv_w, conv_b, wa, ba, wx, bx, a_param, bsz, seq):
    ts = min(512, seq)
    nt = seq // ts
    wg = jnp.concatenate([_block_diag(wa), _block_diag(wx)], axis=1)
    wg_hi = wg.astype(BF16)
    wg2 = jnp.stack([wg_hi, (wg - wg_hi.astype(F32)).astype(BF16)])
    bg = jnp.concatenate([ba, bx]).reshape(1, -1)
    sp = jax.nn.softplus(a_param.astype(F32)).reshape(1, -1)
    col = lambda cb: pl.BlockSpec((ts, LRU_WIDTH), lambda b, t: (b * nt + t, cb))
    full = lambda a: pl.BlockSpec(a.shape, lambda b, t: (0,) * a.ndim)
    cb2 = conv_b.reshape(1, -1)
    return pl.pallas_call(
        functools.partial(_lru_kernel, ts=ts),
        grid=(bsz, nt),
        in_specs=[col(COL_LX // 512), col(COL_LG // 512), full(conv_w), full(cb2), full(wg2), full(bg), full(sp)],
        out_specs=pl.BlockSpec((ts, LRU_WIDTH), lambda b, t: (b * nt + t, 0)),
        out_shape=jax.ShapeDtypeStruct((bsz * seq, LRU_WIDTH), F32),
        scratch_shapes=[pltpu.VMEM((8, LRU_WIDTH), F32), pltpu.VMEM((8, LRU_WIDTH), F32)],
        compiler_params=_cparams(("parallel", "arbitrary")),
        name="rglru",
    )(p, p, conv_w, cb2, wg2, bg, sp)


def _layer_norm(x, w, b):
    xc = x - jnp.mean(x, axis=-1, keepdims=True)
    var = jnp.mean(xc * xc, axis=-1, keepdims=True)
    return xc * lax.rsqrt(var + LN_EPS) * w + b


def _merge_kernel(ya_ref, yb_ref, ys_ref, yd_ref, gl_ref, x_ref, wbr_ref, wout_ref, wglu_ref, bglu_ref,
                  bgate_ref, lnw_ref, lnb_ref, rw_ref, rb_ref, h_ref, hb_ref, lg_ref):
    yc = jax.nn.gelu(ys_ref[...])
    yc = yc * jax.nn.sigmoid(_dot(yc.astype(BF16), wglu_ref[...]) + bglu_ref[...])
    branches = (ya_ref[...].astype(BF16), yb_ref[...], yc.astype(BF16), yd_ref[...].astype(BF16))
    mix = None
    for bi, yb in enumerate(branches):
        cs = slice(bi * D_MODEL, (bi + 1) * D_MODEL)
        gate = jax.nn.sigmoid(gl_ref[:, cs] + bgate_ref[:, cs])
        term = gate * _dot(yb, wbr_ref[bi])
        mix = term if mix is None else mix + term
    mo = _dot(mix.astype(BF16), wout_ref[...])
    h = _layer_norm(DEEPNORM_ALPHA * x_ref[...] + mo, lnw_ref[...], lnb_ref[...])
    h_ref[...] = h
    h_hi, h_lo = _split(h)
    hb_ref[...] = h_hi
    hcat = jnp.concatenate([h_hi, h_lo, h_hi], axis=-1)
    lg_ref[...] = _dot_nt(rw_ref[...], hcat) + rb_ref[...]


def _merge(ya, yb, ys, yd, gl, x, w_branch, w_out, w_glu, b_glu, b_gate, ln_w, ln_b, router_w, router_b):
    n = x.shape[0]
    tm = min(256, n)
    rwt = router_w.T.astype(F32)
    rw_hi = rwt.astype(BF16)
    rw_lo = (rwt - rw_hi.astype(F32)).astype(BF16)
    rw3 = jnp.concatenate([rw_hi, rw_hi, rw_lo], axis=1)
    row = lambda w: pl.BlockSpec((tm, w), lambda i: (i, 0))
    full = lambda a: pl.BlockSpec(a.shape, lambda i: (0,) * a.ndim)
    consts = [w_branch.astype(BF16), w_out.astype(BF16), w_glu.astype(BF16), b_glu.reshape(1, -1),
              b_gate.reshape(1, -1), ln_w.reshape(1, -1), ln_b.reshape(1, -1), rw3, router_b.reshape(-1, 1)]
    return pl.pallas_call(
        _merge_kernel,
        grid=(n // tm,),
        in_specs=[row(512), row(512), row(512), row(512), row(N_BRANCH * D_MODEL), row(D_MODEL)]
                 + [full(c) for c in consts],
        out_specs=[row(D_MODEL), row(D_MODEL), pl.BlockSpec((N_EXPERTS, tm), lambda i: (0, i))],
        out_shape=[jax.ShapeDtypeStruct((n, D_MODEL), F32), jax.ShapeDtypeStruct((n, D_MODEL), BF16),
                   jax.ShapeDtypeStruct((N_EXPERTS, n), F32)],
        compiler_params=_cparams(("parallel",)),
        name="merge",
    )(ya, yb, ys, yd, gl, x, *consts)


def _route_kernel(lg_ref, tri_ref, idx_ref, w_ref, rank_ref, cnt_ref, carry_ref, *, tr):
    @pl.when(pl.program_id(0) == 0)
    def _():
        carry_ref[...] = jnp.zeros_like(carry_ref)

    l = lg_ref[...]
    eidx = lax.broadcasted_iota(jnp.int32, l.shape, 0)
    vals, hots = [], []
    for _ in range(TOP_K):
        m = jnp.max(l, axis=0, keepdims=True)
        first = jnp.min(jnp.where(l == m, eidx, N_EXPERTS), axis=0, keepdims=True)
        hot = eidx == first
        l = jnp.where(hot, -jnp.inf, l)
        vals.append(m)
        hots.append(hot)
        idx_ref[len(vals) - 1:len(vals), :] = first
    ex = [jnp.exp(v - vals[0]) for v in vals]
    den = ex[0] + ex[1] + ex[2] + ex[3]
    for k in range(TOP_K):
        w_ref[k:k + 1, :] = ex[k] / den
    member = jnp.zeros(l.shape, F32)
    for hot in hots:
        member = member + hot.astype(F32)
    before = _dot(member.astype(BF16), tri_ref[...]) + carry_ref[:, 0:1]
    for k in range(TOP_K):
        rank = jnp.sum(jnp.where(hots[k], before, 0.0), axis=0, keepdims=True)
        rank_ref[k:k + 1, :] = rank.astype(jnp.int32)
    carry_ref[...] = carry_ref[...] + jnp.sum(member, axis=1, keepdims=True)
    cnt_ref[...] = carry_ref[...]
    idx_ref[TOP_K:, :] = jnp.zeros((8 - TOP_K, tr), jnp.int32)
    w_ref[TOP_K:, :] = jnp.zeros((8 - TOP_K, tr), F32)
    rank_ref[TOP_K:, :] = jnp.zeros((8 - TOP_K, tr), jnp.int32)


def _route(logits_t):
    n = logits_t.shape[1]
    tr = min(512, n)
    tri = jnp.asarray(np.triu(np.ones((tr, tr), np.float32), 1), BF16)
    tok = pl.BlockSpec((8, tr), lambda i: (0, i))
    return pl.pallas_call(
        functools.partial(_route_kernel, tr=tr),
        grid=(n // tr,),
        in_specs=[pl.BlockSpec((N_EXPERTS, tr), lambda i: (0, i)), pl.BlockSpec((tr, tr), lambda i: (0, 0))],
        out_specs=[tok, tok, tok, pl.BlockSpec((N_EXPERTS, 128), lambda i: (0, 0))],
        out_shape=[jax.ShapeDtypeStruct((8, n), jnp.int32), jax.ShapeDtypeStruct((8, n), F32),
                   jax.ShapeDtypeStruct((8, n), jnp.int32), jax.ShapeDtypeStruct((N_EXPERTS, 128), F32)],
        scratch_shapes=[pltpu.VMEM((N_EXPERTS, 128), F32)],
        compiler_params=_cparams(("arbitrary",)),
        name="route",
    )(logits_t, tri)


def _expert_kernel(be_ref, nu_ref, x_ref, w1_ref, b1_ref, w2_ref, b2_ref, o_ref, w1b_ref, w2b_ref):
    i = pl.program_id(0)

    @pl.when(i < nu_ref[0])
    def _():
        @pl.when(jnp.logical_or(i == 0, be_ref[i] != be_ref[jnp.maximum(i - 1, 0)]))
        def _():
            w1b_ref[...] = w1_ref[0].astype(BF16)
            w2b_ref[...] = w2_ref[0].astype(BF16)

        h1 = _dot(x_ref[...], w1b_ref[...]) + b1_ref[0]
        glu = jnp.minimum(h1[:, :EXPERT_FF], SWIGLU_LIMIT)
        lin = jnp.clip(h1[:, EXPERT_FF:], -SWIGLU_LIMIT, SWIGLU_LIMIT)
        act = glu * jax.nn.sigmoid(SWIGLU_ALPHA * glu) * (lin + 1.0)
        o_ref[...] = (_dot(act.astype(BF16), w2b_ref[...]) + b2_ref[0]).astype(o_ref.dtype)

    @pl.when(pl.program_id(0) >= nu_ref[0])
    def _():
        o_ref[...] = jnp.zeros_like(o_ref)


def _experts(x_slots, block_expert, n_used, w1, b1, w2, b2, expert_offset):
    n_slots = x_slots.shape[0]
    tm = EXPERT_TILE
    n_blocks = n_slots // tm
    off = expert_offset
    grid_spec = pltpu.PrefetchScalarGridSpec(
        num_scalar_prefetch=2,
        grid=(n_blocks,),
        in_specs=[pl.BlockSpec((tm, D_MODEL), lambda i, be, nu: (i, 0)),
                  pl.BlockSpec((1, D_MODEL, 2 * EXPERT_FF), lambda i, be, nu: (be[i] + off, 0, 0)),
                  pl.BlockSpec((1, 1, 2 * EXPERT_FF), lambda i, be, nu: (be[i], 0, 0)),
                  pl.BlockSpec((1, EXPERT_FF, D_MODEL), lambda i, be, nu: (be[i] + off, 0, 0)),
                  pl.BlockSpec((1, 1, D_MODEL), lambda i, be, nu: (be[i], 0, 0))],
        out_specs=pl.BlockSpec((tm, D_MODEL), lambda i, be, nu: (i, 0)),
        scratch_shapes=[pltpu.VMEM((D_MODEL, 2 * EXPERT_FF), BF16), pltpu.VMEM((EXPERT_FF, D_MODEL), BF16)],
    )
    return pl.pallas_call(
        _expert_kernel,
        grid_spec=grid_spec,
        out_shape=jax.ShapeDtypeStruct((n_slots, D_MODEL), BF16),
        compiler_params=_cparams(("arbitrary",)),
        name="experts",
    )(block_expert, n_used, x_slots, w1, b1.reshape(N_EXPERTS, 1, -1), w2, b2.reshape(N_EXPERTS, 1, -1))


def _combine_kernel(h_ref, y_ref, w_ref, lnw_ref, lnb_ref, o_ref):
    acc = DEEPNORM_ALPHA * h_ref[...]
    for k in range(TOP_K):
        acc = acc + w_ref[:, k:k + 1] * y_ref[k].astype(F32)
    o_ref[...] = _layer_norm(acc, lnw_ref[...], lnb_ref[...])


def _combine(h, y_sel, w_tok, ln_w, ln_b):
    n = h.shape[0]
    tm = min(256, n)
    lnw = ln_w.reshape(1, -1)
    lnb = ln_b.reshape(1, -1)
    return pl.pallas_call(
        _combine_kernel,
        grid=(n // tm,),
        in_specs=[pl.BlockSpec((tm, D_MODEL), lambda i: (i, 0)),
                  pl.BlockSpec((TOP_K, tm, D_MODEL), lambda i: (0, i, 0)),
                  pl.BlockSpec((tm, TOP_K), lambda i: (i, 0)),
                  pl.BlockSpec(lnw.shape, lambda i: (0, 0)), pl.BlockSpec(lnb.shape, lambda i: (0, 0))],
        out_specs=pl.BlockSpec((tm, D_MODEL), lambda i: (i, 0)),
        out_shape=jax.ShapeDtypeStruct((n, D_MODEL), F32),
        compiler_params=_cparams(("parallel",)),
        name="combine",
    )(h, y_sel, w_tok, lnw, lnb)


def _moe(h, h16, logits_t, w1, b1, w2, b2, ln_w, ln_b, expert_offset):
    n = h.shape[0]
    tm = EXPERT_TILE
    idx8, w8, rank8, cnt = _route(logits_t)
    idx, w_top, rank = idx8[:TOP_K], w8[:TOP_K], rank8[:TOP_K]
    counts = cnt[:, 0].astype(jnp.int32)
    padded = ((counts + tm - 1) // tm) * tm
    p_end = jnp.cumsum(padded)
    p_start = p_end - padded
    dest = p_start[idx] + rank
    n_slots = n * TOP_K + N_EXPERTS * tm
    n_blocks = n_slots // tm
    starts = jnp.arange(n_blocks, dtype=jnp.int32) * tm
    block_expert = jnp.minimum(jnp.sum((p_end[None, :] <= starts[:, None]).astype(jnp.int32), axis=1),
                               N_EXPERTS - 1)
    n_used = (p_end[-1] // tm).astype(jnp.int32).reshape(1)
    tok = jnp.broadcast_to(jnp.arange(n, dtype=jnp.int32)[None, :], (TOP_K, n))
    slot_token = jnp.zeros((n_slots,), jnp.int32).at[dest.reshape(-1)].set(
        tok.reshape(-1), mode='promise_in_bounds', unique_indices=True)
    x_slots = h16.at[slot_token].get(mode='promise_in_bounds')
    y_slots = _experts(x_slots, block_expert, n_used, w1, b1, w2, b2, expert_offset)
    y_sel = y_slots.at[dest].get(mode='promise_in_bounds')
    return _combine(h, y_sel, w_top.T, ln_w, ln_b)


def _mixer_weight(w_in):
    o = np.cumsum((512, 512, 512, 512, MLA_Q_RANK, MLA_KV_RANK + MLA_ROPE, S5_WIDTH, LRU_WIDTH, LRU_WIDTH))
    hgrn, cq, ckv = w_in[:, :o[3]], w_in[:, o[3]:o[4]], w_in[:, o[4]:o[4] + MLA_KV_RANK]
    kpe = w_in[:, o[4] + MLA_KV_RANK:o[5]]
    su, lx, lg = w_in[:, o[5]:o[6]], w_in[:, o[6]:o[7]], w_in[:, o[7]:o[8]]
    pad = jnp.zeros((D_MODEL, 128 - MLA_ROPE), w_in.dtype)
    w_mix = jnp.concatenate([hgrn, su, lx, lg, cq, ckv, kpe, pad, _rot_half_cols(kpe), pad], axis=1)
    return w_mix, w_in[:, o[8]:]


def _layer(x, positions, lb, bsz, seq, w_in, b_gate, hgrn_norm_w, mla_q_norm_w, mla_wq_b, mla_kv_norm_w, mla_wkv_b,
           s5_a_re, s5_a_im, s5_log_dt, s5_b_re, s5_b_im, s5_c_re, s5_c_im, s5_d, s5_w_glu, s5_b_glu,
           lru_conv_w, lru_conv_b, lru_wa, lru_ba, lru_wx, lru_bx, lru_a_param,
           w_branch, w_out, ln1_w, ln1_b, ln2_w, ln2_b, router_w, router_b, moe_w1, moe_b1, moe_w2, moe_b2,
           expert_offset=0):
    w_mix, w_gl = _mixer_weight(w_in)
    p = _matmul(x, w_mix.astype(BF16), 1024, MIX_WIDTH // 3, F32)
    gl = _matmul(x, w_gl.astype(BF16), 1024, 1024, F32)
    y_a = _hgrn(p, lb, hgrn_norm_w, bsz, seq)
    qn, qp, kn, kp, vt = _mla_proj(p, positions, mla_q_norm_w, mla_wq_b, mla_kv_norm_w, mla_wkv_b, min(256, seq))
    y_b = _flash(qn, qp, kn, kp, vt, bsz, seq)
    y_s = _s5(p, s5_a_re, s5_a_im, s5_log_dt, s5_b_re, s5_b_im, s5_c_re, s5_c_im, s5_d, bsz, seq)
    y_d = _lru(p, lru_conv_w, lru_conv_b, lru_wa, lru_ba, lru_wx, lru_bx, lru_a_param, bsz, seq)
    h, h16, logits_t = _merge(y_a, y_b, y_s, y_d, gl, x, w_branch, w_out, s5_w_glu, s5_b_glu, b_gate,
                              ln1_w, ln1_b, router_w, router_b)
    return _moe(h, h16, logits_t, moe_w1, moe_b1, moe_w2, moe_b2, ln2_w, ln2_b, expert_offset)


def kernel(x, positions, w_in, b_gate, hgrn_lb_logits, hgrn_norm_w, mla_q_norm_w, mla_wq_b, mla_kv_norm_w, mla_wkv_b, s5_a_re, s5_a_im, s5_log_dt, s5_b_re, s5_b_im, s5_c_re, s5_c_im, s5_d, s5_w_glu, s5_b_glu, lru_conv_w, lru_conv_b, lru_wa, lru_ba, lru_wx, lru_bx, lru_a_param, w_branch, w_out, ln1_w, ln1_b, ln2_w, ln2_b, router_w, router_b, moe_w1, moe_b1, moe_w2, moe_b2):
    bsz, seq, _ = x.shape
    probs = jax.nn.softmax(hgrn_lb_logits.astype(F32), axis=0)
    lower_bounds = jnp.cumsum(probs, axis=0) - probs[0:1]
    per_layer = (w_in, b_gate, hgrn_norm_w, mla_q_norm_w, mla_wq_b, mla_kv_norm_w, mla_wkv_b,
                 s5_a_re, s5_a_im, s5_log_dt, s5_b_re, s5_b_im, s5_c_re, s5_c_im, s5_d, s5_w_glu, s5_b_glu,
                 lru_conv_w, lru_conv_b, lru_wa, lru_ba, lru_wx, lru_bx, lru_a_param,
                 w_branch, w_out, ln1_w, ln1_b, ln2_w, ln2_b, router_w, router_b, moe_w1, moe_b1, moe_w2, moe_b2)
    xf = x.reshape(bsz * seq, D_MODEL)
    w1_all = moe_w1.reshape((DEPTH * N_EXPERTS,) + moe_w1.shape[2:])
    w2_all = moe_w2.reshape((DEPTH * N_EXPERTS,) + moe_w2.shape[2:])
    for l in range(DEPTH):
        args = [a[l] for a in per_layer]
        args[-4], args[-2] = w1_all, w2_all
        xf = _layer(xf, positions, lower_bounds[l], bsz, seq, *args, expert_offset=l * N_EXPERTS)
    return xf.reshape(bsz, seq, D_MODEL)
```

```python
import functools
import math

import numpy as np
import jax
import jax.numpy as jnp
from jax import lax
from jax.experimental import pallas as pl
from jax.experimental.pallas import tpu as pltpu

F32 = jnp.float32
BF16 = jnp.bfloat16

D_MODEL = 1024
DEPTH = 2
CHUNK = 64

HGRN_HEADS = 4
HGRN_DK = 128
HGRN_WIDTH = 512
HGRN_F_MIN = 1e-30
HGRN_CHUNK = 64
HGRN_LEVELS = (1, 2, 4, 8, 16, 32)

MLA_HEADS = 4
MLA_Q_RANK = 256
MLA_KV_RANK = 128
MLA_NOPE = 128
MLA_ROPE = 64
MLA_V = 128
MLA_QK = MLA_NOPE + MLA_ROPE
MLA_VROWS = MLA_V + 16
ROPE_THETA = 10000.0

S5_GROUPS = 32
S5_GROUP_CH = 16
S5_STATE = 64
S5_WIDTH = 512
S5_BLOCK = 16

LRU_WIDTH = 512
LRU_BLOCKS = 8
LRU_BLOCK_W = 64
LRU_CONV = 4
LRU_C = 8.0

N_BRANCH = 4
BRANCH_WIDTH = 512

N_EXPERTS = 32
TOP_K = 4
EXPERT_FF = 1024
SWIGLU_ALPHA = 1.702
SWIGLU_LIMIT = 7.0
EXPERT_TILE = 512

DEEPNORM_ALPHA = (2.0 * DEPTH) ** 0.25
LN_EPS = 1e-5
RMS_EPS = 1e-6

COL_HQ, COL_HF, COL_HI, COL_HG = 0, 512, 1024, 1536
COL_SU, COL_LX, COL_LG = 2048, 2560, 3072
COL_CQ, COL_CKV, COL_KPE, COL_KPER = 3584, 3840, 3968, 4096
MIX_WIDTH = 4224

VMEM_LIMIT = 56 * 1024 * 1024


def _cparams(sem):
    return pltpu.CompilerParams(dimension_semantics=sem, vmem_limit_bytes=VMEM_LIMIT)


def _dot(a, b):
    return jnp.dot(a, b, preferred_element_type=F32)


def _dot_nt(a, b):
    return lax.dot_general(a, b, (((1,), (1,)), ((), ())), preferred_element_type=F32)


def _dot_tn(a, b):
    return lax.dot_general(a, b, (((0,), (0,)), ((), ())), preferred_element_type=F32)


def _split(x):
    hi = x.astype(BF16)
    lo = (x - hi.astype(F32)).astype(BF16)
    return hi, lo


def _dot3(a, b_hi, b_lo):
    a_hi, a_lo = _split(a)
    return _dot(a_hi, b_hi) + (_dot(a_lo, b_hi) + _dot(a_hi, b_lo))


def _pack_pairs(x):
    w = x.shape[-1] // 2
    lo = lax.bitcast_convert_type(x[:, :w].astype(BF16).astype(F32), jnp.uint32)
    hi = lax.bitcast_convert_type(x[:, w:].astype(BF16).astype(F32), jnp.uint32)
    return lax.shift_right_logical(lo, jnp.uint32(16)) | (hi & jnp.uint32(0xFFFF0000))


def _unpack_pairs(u):
    lo = lax.bitcast_convert_type(lax.shift_left(u, jnp.uint32(16)), F32)
    hi = lax.bitcast_convert_type(u & jnp.uint32(0xFFFF0000), F32)
    return jnp.concatenate([lo, hi], axis=-1)


def _shift_rows(x, k, fill):
    rows = lax.broadcasted_iota(jnp.int32, x.shape, 0)
    return jnp.where(rows >= k, pltpu.roll(x, k, 0), fill)


def _mm_kernel(x_ref, w_ref, o_ref):
    o_ref[...] = _dot(x_ref[...].astype(BF16), w_ref[...]).astype(o_ref.dtype)


def _matmul(x, w, tm, tn, out_dtype):
    n, k = x.shape
    m = w.shape[1]
    tm = min(tm, n)
    return pl.pallas_call(
        _mm_kernel,
        grid=(n // tm, m // tn),
        in_specs=[pl.BlockSpec((tm, k), lambda i, j: (i, 0)),
                  pl.BlockSpec((k, tn), lambda i, j: (0, j))],
        out_specs=pl.BlockSpec((tm, tn), lambda i, j: (i, j)),
        out_shape=jax.ShapeDtypeStruct((n, m), out_dtype),
        compiler_params=_cparams(("parallel", "arbitrary")),
        name="in_proj",
    )(x, w)


def _hgrn_tables():
    c = HGRN_CHUNK
    t = np.arange(c)
    windows = [np.tril(np.ones((c, c), bool)),
               np.triu(np.ones((c, c), bool), 1)]
    masks = [np.eye(c, dtype=bool)]
    for h in HGRN_LEVELS:
        blk = t // h
        odd = blk % 2 == 1
        masks.append(odd[:, None] & (blk[None, :] == blk[:, None] - 1))
        if h > 1:
            windows.append((t[None, :] >= (h * blk)[:, None]) & (t[None, :] <= t[:, None]))
            windows.append((t[None, :] > t[:, None]) & (t[None, :] <= (h * blk + h - 1)[:, None]))
    return np.concatenate(windows, 0).astype(np.float32), np.stack(masks).astype(np.float32)


def _hgrn_kernel(hq_ref, hf_ref, hi_ref, hg_ref, lb_ref, nw_ref, win_ref, mask_ref,
                 o_ref, state_ref, *, n_chunks):
    c = HGRN_CHUNK
    heads = range(HGRN_HEADS)
    hs = [slice(h * HGRN_DK, (h + 1) * HGRN_DK) for h in heads]
    n_lvl = len(HGRN_LEVELS) + 1

    @pl.when(pl.program_id(1) == 0)
    def _():
        state_ref[...] = jnp.zeros_like(state_ref)

    def chunk_body(ci, carry):
        rows = pl.ds(pl.multiple_of(ci * c, c), c)
        lb = lb_ref[...]
        z = hf_ref[rows, :]
        f = jnp.maximum(lb + (1.0 - lb) * jax.nn.sigmoid(z), HGRN_F_MIN)
        k = (1.0 - lb) * jax.nn.sigmoid(-z)
        q = jax.nn.silu(hq_ref[rows, :])
        v16 = hi_ref[rows, :].astype(BF16)
        lf_hi, lf_lo = _split(jnp.log(f))
        win = win_ref[...]
        e = jnp.exp(_dot(win, lf_hi) + _dot(win, lf_lo))
        blk = lambda w: e[w * c:(w + 1) * c]
        k16 = k.astype(BF16)
        qs = [q.astype(BF16), (q * f).astype(BF16)] + [(q * blk(2 * w)).astype(BF16) for w in range(1, n_lvl - 1)]
        ks = [k16, k16] + [(k * blk(2 * w + 1)).astype(BF16) for w in range(1, n_lvl - 1)]
        q_in = (q * blk(0)).astype(BF16)
        k_out = (k * blk(1)).astype(BF16)
        decay = e[c - 1:c, :]
        scores = [None] * HGRN_HEADS
        for lv in range(n_lvl):
            s_l = [_dot_nt(qs[lv][:, hs[h]], ks[lv][:, hs[h]]) for h in heads]
            for h in heads:
                term = mask_ref[lv] * s_l[h]
                scores[h] = term if scores[h] is None else scores[h] + term
        st = [state_ref[h] for h in heads]
        o = [_dot(scores[h].astype(BF16), v16[:, hs[h]]) + _dot_nt(q_in[:, hs[h]], st[h].astype(BF16)) for h in heads]
        upd = [_dot_tn(v16[:, hs[h]], k_out[:, hs[h]]) for h in heads]
        for h in heads:
            state_ref[h] = st[h] * decay[:, hs[h]] + upd[h]
        o = [o[h] * lax.rsqrt(jnp.mean(o[h] * o[h], axis=-1, keepdims=True) + RMS_EPS) for h in heads]
        o_ref[rows, :] = jnp.concatenate(o, axis=-1) * nw_ref[...] * jax.nn.silu(hg_ref[rows, :])
        return carry

    lax.fori_loop(0, n_chunks, chunk_body, 0)


def _hgrn(p, lb, norm_w, bsz, seq):
    t = min(256, seq)
    nt = seq // t
    win, mask = _hgrn_tables()
    col = lambda cb: pl.BlockSpec((t, HGRN_WIDTH), lambda b, i: (b * nt + i, cb))
    const2 = lambda shape: pl.BlockSpec(shape, lambda b, i: (0,) * len(shape))
    return pl.pallas_call(
        functools.partial(_hgrn_kernel, n_chunks=t // HGRN_CHUNK),
        grid=(bsz, nt),
        in_specs=[col(COL_HQ // 512), col(COL_HF // 512), col(COL_HI // 512), col(COL_HG // 512),
                  const2((1, HGRN_WIDTH)), const2((1, HGRN_WIDTH)), const2(win.shape), const2(mask.shape)],
        out_specs=pl.BlockSpec((t, HGRN_WIDTH), lambda b, i: (b * nt + i, 0)),
        out_shape=jax.ShapeDtypeStruct((bsz * seq, HGRN_WIDTH), F32),
        scratch_shapes=[pltpu.VMEM((HGRN_HEADS, HGRN_DK, HGRN_DK), F32)],
        compiler_params=_cparams(("parallel", "arbitrary")),
        name="hgrn",
    )(p, p, p, p, lb.reshape(1, -1), norm_w.reshape(1, -1), jnp.asarray(win, BF16), jnp.asarray(mask))


def _rms(x, w):
    ms = jnp.mean(x * x, axis=-1, keepdims=True)
    return x * lax.rsqrt(ms + RMS_EPS) * w


def _mla_proj_kernel(cq_ref, ckv_ref, kpe_ref, kper_ref, ang_ref, qnw_ref, kvnw_ref,
                     wqn_ref, wqp_ref, wqpr_ref, wkn_ref, wv_ref,
                     qn_ref, qp_ref, kn_ref, kp_ref, v_ref):
    ang = ang_ref[...]
    cos = jnp.cos(ang)
    sin = jnp.sin(ang)
    qn = _rms(cq_ref[...], qnw_ref[...]).astype(BF16)
    cos4 = jnp.concatenate([cos] * MLA_HEADS, axis=-1)
    sin4 = jnp.concatenate([sin] * MLA_HEADS, axis=-1)
    qn_ref[...] = _dot(qn, wqn_ref[...]).astype(BF16)
    qp_ref[...] = (_dot(qn, wqp_ref[...]) * cos4 + _dot(qn, wqpr_ref[...]) * sin4).astype(BF16)
    cn = _rms(ckv_ref[...], kvnw_ref[...]).astype(BF16)
    kn_ref[...] = _dot(cn, wkn_ref[...]).astype(BF16)
    vt = _dot_nt(wv_ref[...], cn)
    ones = jnp.ones((MLA_VROWS - MLA_V, vt.shape[1]), F32)
    v_ref[0] = jnp.concatenate(
        [piece for h in range(MLA_HEADS) for piece in (vt[h * MLA_V:(h + 1) * MLA_V], ones)], axis=0).astype(BF16)
    kp_ref[...] = (kpe_ref[...] * cos + kper_ref[...] * sin).astype(BF16)


def _rot_half_cols(w):
    half = w.shape[-1] // 2
    return jnp.concatenate([-w[..., half:], w[..., :half]], axis=-1)


def _mla_proj(p, positions, q_norm_w, wq_b, kv_norm_w, wkv_b, tm):
    n = p.shape[0]
    scale = MLA_QK ** -0.5 * math.log2(math.e)
    wq = wq_b.reshape(MLA_Q_RANK, MLA_HEADS, MLA_QK) * scale
    w_nope = wq[:, :, :MLA_NOPE].reshape(MLA_Q_RANK, MLA_HEADS * MLA_NOPE)
    w_pe = wq[:, :, MLA_NOPE:]
    pad = jnp.zeros((MLA_Q_RANK, MLA_HEADS, 128 - MLA_ROPE), F32)
    w_pe_p = jnp.concatenate([w_pe, pad], axis=-1).reshape(MLA_Q_RANK, MLA_HEADS * 128)
    w_per_p = jnp.concatenate([_rot_half_cols(w_pe), pad], axis=-1).reshape(MLA_Q_RANK, MLA_HEADS * 128)
    wkv = wkv_b.reshape(MLA_KV_RANK, MLA_HEADS, MLA_NOPE + MLA_V)
    w_kn = wkv[:, :, :MLA_NOPE].reshape(MLA_KV_RANK, MLA_HEADS * MLA_NOPE)
    w_v = wkv[:, :, MLA_NOPE:].reshape(MLA_KV_RANK, MLA_HEADS * MLA_V)
    inv_freq = ROPE_THETA ** (-jnp.arange(0, MLA_ROPE, 2, dtype=F32) / MLA_ROPE)
    freq_lane = jnp.concatenate([inv_freq, inv_freq, jnp.zeros((128 - MLA_ROPE,), F32)])
    ang = positions.reshape(n, 1).astype(F32) * freq_lane[None, :]

    row = lambda w, cb: pl.BlockSpec((tm, w), lambda i: (i, cb))
    full = lambda a: pl.BlockSpec(a.shape, lambda i: (0,) * a.ndim)
    weights = [w_nope.astype(BF16), w_pe_p.astype(BF16), w_per_p.astype(BF16), w_kn.astype(BF16),
               w_v.T.astype(BF16)]
    qnw = q_norm_w.reshape(1, -1)
    kvnw = kv_norm_w.reshape(1, -1)
    outs = pl.pallas_call(
        _mla_proj_kernel,
        grid=(n // tm,),
        in_specs=[row(256, COL_CQ // 256), row(128, COL_CKV // 128), row(128, COL_KPE // 128),
                  row(128, COL_KPER // 128), row(128, 0), full(qnw), full(kvnw)] + [full(w) for w in weights],
        out_specs=[row(512, 0), row(512, 0), row(512, 0), row(128, 0),
                   pl.BlockSpec((1, MLA_HEADS * MLA_VROWS, tm), lambda i: (i, 0, 0))],
        out_shape=[jax.ShapeDtypeStruct((n, 512), BF16), jax.ShapeDtypeStruct((n, 512), BF16),
                   jax.ShapeDtypeStruct((n, 512), BF16), jax.ShapeDtypeStruct((n, 128), BF16),
                   jax.ShapeDtypeStruct((n // tm, MLA_HEADS * MLA_VROWS, tm), BF16)],
        compiler_params=_cparams(("parallel",)),
        name="mla_proj",
    )(p, p, p, p, ang, qnw, kvnw, *weights)
    return outs


def _flash_kernel(qn_ref, qp_ref, kn_ref, kp_ref, vt_ref, o_ref, acc_ref, *, tq):
    i = pl.program_id(1)
    heads = range(MLA_HEADS)
    hs = [slice(h * 128, (h + 1) * 128) for h in heads]
    vs = [slice(h * MLA_VROWS, (h + 1) * MLA_VROWS) for h in heads]
    acc_ref[...] = jnp.zeros_like(acc_ref)

    def scores(j):
        r0 = pl.multiple_of(j * tq, tq)
        kp = kp_ref[pl.ds(r0, tq), :]
        return tuple(_dot_nt(jnp.concatenate([kn_ref[pl.ds(r0, tq), hs[h]], kp], axis=-1),
                             jnp.concatenate([qn_ref[:, hs[h]], qp_ref[:, hs[h]]], axis=-1)) for h in heads)

    def consume(j, s, m, mask):
        p, m_new, alpha = [], [], []
        for h in heads:
            sh = s[h] if mask is None else jnp.where(mask, s[h], -1e30)
            mn = jnp.maximum(m[h], jnp.max(sh, axis=0, keepdims=True))
            alpha.append(jnp.exp2(m[h] - mn))
            p.append(jnp.exp2(sh - mn).astype(BF16))
            m_new.append(mn)
        vt = vt_ref[j]
        pv = [_dot(vt[vs[h], :], p[h]) for h in heads]
        for h in heads:
            acc_ref[h] = acc_ref[h] * alpha[h] + pv[h]
        return tuple(m_new)

    def body(j, carry):
        s, m = carry
        s_next = scores(j + 1)
        return s_next, consume(j, s, m, None)

    m0 = tuple(jnp.full((1, tq), -1e30, F32) for _ in heads)
    s, m = lax.fori_loop(0, i, body, (scores(0), m0))
    shift = CHUNK.bit_length() - 1
    kc = lax.shift_right_logical(lax.broadcasted_iota(jnp.int32, (tq, tq), 0), shift)
    qc = lax.shift_right_logical(lax.broadcasted_iota(jnp.int32, (tq, tq), 1), shift)
    consume(i, s, m, kc <= qc)
    for h in heads:
        acc = acc_ref[h]
        o_ref[:, hs[h]] = (acc[:MLA_V] / acc[MLA_V:MLA_V + 1]).T.astype(o_ref.dtype)


def _flash(qn, qp, kn, kp, vt, bsz, seq):
    tq = vt.shape[2]
    nq = seq // tq
    width = MLA_HEADS * 128
    qspec = pl.BlockSpec((tq, width), lambda b, i: (b * nq + i, 0))
    kspec = pl.BlockSpec((seq, width), lambda b, i: (b, 0))
    kpspec = pl.BlockSpec((seq, 128), lambda b, i: (b, 0))
    vspec = pl.BlockSpec((nq, MLA_HEADS * MLA_VROWS, tq), lambda b, i: (b, 0, 0))
    return pl.pallas_call(
        functools.partial(_flash_kernel, tq=tq),
        grid=(bsz, nq),
        in_specs=[qspec, qspec, kspec, kpspec, vspec],
        out_specs=qspec,
        out_shape=jax.ShapeDtypeStruct((bsz * seq, width), BF16),
        scratch_shapes=[pltpu.VMEM((MLA_HEADS, MLA_VROWS, tq), F32)],
        compiler_params=_cparams(("parallel", "arbitrary")),
        name="flash",
    )(qn, qp, kn, kp, vt)


def _cmul(a, b):
    return a[0] * b[0] - a[1] * b[1], a[0] * b[1] + a[1] * b[0]


def _s5_tables(a_re, a_im, log_dt, b_re, b_im, c_re, c_im, d_skip, steps):
    lb = S5_BLOCK
    f32 = lambda v: v.astype(F32)
    a_re, a_im, b_re, b_im, c_re, c_im = map(f32, (a_re, a_im, b_re, b_im, c_re, c_im))
    dt = jnp.exp(f32(log_dt))[:, None]
    ld = (a_re * dt, a_im * dt)

    def power(tau):
        mag = jnp.exp(ld[0] * tau)
        return mag * jnp.cos(ld[1] * tau), mag * jnp.sin(ld[1] * tau)

    lam_bar = power(1.0)
    inv = a_re * a_re + a_im * a_im
    ratio = _cmul((lam_bar[0] - 1.0, lam_bar[1]), (a_re / inv, -a_im / inv))
    b_bar = _cmul((ratio[0][..., None], ratio[1][..., None]), (b_re, b_im))
    tau = jnp.arange(lb + 1, dtype=F32)[:, None, None]
    pw = power(tau)
    cp = _cmul((c_re[None], c_im[None]), (pw[0][:, :, None, :], pw[1][:, :, None, :]))
    kern = jnp.einsum('tgcn,gnd->tgcd', cp[0][:lb], b_bar[0]) - jnp.einsum('tgcn,gnd->tgcd', cp[1][:lb], b_bar[1])
    s_idx = np.arange(lb)[:, None]
    t_idx = np.arange(lb)[None, :]
    lag = np.clip(t_idx - s_idx, 0, lb - 1)
    causal = jnp.asarray((t_idx >= s_idx).astype(np.float32))
    toe = kern[lag] * causal[:, :, None, None, None]
    skip = jnp.asarray(np.eye(lb, dtype=np.float32))[:, :, None, None, None] * (
        d_skip.astype(F32)[None, None, :, :, None] * jnp.eye(S5_GROUP_CH, dtype=F32)[None, None, None])
    toe = (toe + skip).transpose(2, 1, 3, 0, 4).reshape(S5_GROUPS, lb * S5_GROUP_CH, lb * S5_GROUP_CH)
    emit = jnp.concatenate([cp[0][1:], -cp[1][1:]], axis=-1)
    emit = emit.transpose(1, 0, 2, 3).reshape(S5_GROUPS, lb * S5_GROUP_CH, 2 * S5_STATE)
    rev = (pw[0][:lb][::-1][..., None], pw[1][:lb][::-1][..., None])
    bp = _cmul(rev, (b_bar[0][None], b_bar[1][None]))
    fold = jnp.concatenate([bp[0], bp[1]], axis=2)
    fold = fold.transpose(1, 2, 0, 3).reshape(S5_GROUPS, 2 * S5_STATE, lb * S5_GROUP_CH)
    jump = (lb * 2.0 ** jnp.arange(steps, dtype=F32))[:, None, None]
    pj = power(jump)
    pw_re = jnp.concatenate([pj[0], pj[0]], axis=-1).transpose(1, 2, 0)
    pw_im = jnp.concatenate([-pj[1], pj[1]], axis=-1).transpose(1, 2, 0)
    return toe, emit, fold, pw_re, pw_im


def _s5_kernel(su0_ref, su1_ref, su2_ref, su3_ref, toe_ref, emit_ref, fold_ref, pre_ref, pim_ref, y_ref,
               ut_ref, carry_ref, *, r, steps):
    lb = S5_BLOCK
    gc = S5_GROUP_CH

    @pl.when(pl.program_id(1) == 0)
    def _():
        carry_ref[...] = jnp.zeros_like(carry_ref)

    for s in range(lb):
        for c, su_ref in enumerate((su0_ref, su1_ref, su2_ref, su3_ref)):
            ut_ref[s, c * 128:(c + 1) * 128, :] = su_ref[pl.ds(s, r, stride=lb), :].T
    lane = lax.broadcasted_iota(jnp.int32, (2 * S5_STATE, r), 1)
    swap = lambda v: jnp.concatenate([v[S5_STATE:], v[:S5_STATE]], axis=0)

    def group(g, carry):
        g0 = pl.multiple_of(g * gc, gc)
        ug = jnp.concatenate([ut_ref[s, pl.ds(g0, gc), :] for s in range(lb)], axis=0).astype(BF16)
        pre = pre_ref[g]
        pim = pim_ref[g]
        cmul = lambda j, v: pre[:, j:j + 1] * v + pim[:, j:j + 1] * swap(v)
        car = carry_ref[g]
        x = _dot(fold_ref[g], ug)
        x = x + jnp.where(lane == 0, cmul(0, car), 0.0)
        for j in range(steps):
            xs = jnp.where(lane >= 2 ** j, pltpu.roll(x, 2 ** j, 1), 0.0)
            x = x + cmul(j, xs)
        x_prev = jnp.where(lane == 0, car, pltpu.roll(x, 1, 1))
        carry_ref[g] = jnp.broadcast_to(x[:, r - 1:r], x.shape)
        y = _dot(toe_ref[g], ug) + _dot(emit_ref[g], x_prev.astype(BF16))
        for t in range(lb):
            ut_ref[t, pl.ds(g0, gc), :] = y[t * gc:(t + 1) * gc]
        return carry

    lax.fori_loop(0, S5_GROUPS, group, 0)
    for t in range(lb):
        for c in range(S5_WIDTH // 128):
            y_ref[c, pl.ds(t, r, stride=lb), :] = ut_ref[t, c * 128:(c + 1) * 128, :].T


def _s5(p, a_re, a_im, log_dt, b_re, b_im, c_re, c_im, d_skip, bsz, seq):
    lb = S5_BLOCK
    r = min(128, seq // lb)
    nt = seq // (lb * r)
    ncb = S5_WIDTH // 128
    steps = max(1, int(math.ceil(math.log2(r))))
    toe, emit, fold, pw_re, pw_im = _s5_tables(a_re, a_im, log_dt, b_re, b_im, c_re, c_im, d_skip, steps)
    consts = [toe.astype(BF16), emit.astype(BF16), fold.astype(BF16), pw_re, pw_im]
    full = lambda a: pl.BlockSpec(a.shape, lambda b, t: (0,) * a.ndim)
    return pl.pallas_call(
        functools.partial(_s5_kernel, r=r, steps=steps),
        grid=(bsz, nt),
        in_specs=[pl.BlockSpec((lb * r, 128), functools.partial(lambda b, t, c: (b * nt + t, COL_SU // 128 + c), c=c))
                  for c in range(ncb)] + [full(c) for c in consts],
        out_specs=pl.BlockSpec((ncb, lb * r, 128), lambda b, t: (0, b * nt + t, 0)),
        out_shape=jax.ShapeDtypeStruct((ncb, bsz * seq, 128), F32),
        scratch_shapes=[pltpu.VMEM((lb, S5_WIDTH, r), F32), pltpu.VMEM((S5_GROUPS, 2 * S5_STATE, r), F32)],
        compiler_params=_cparams(("parallel", "arbitrary")),
        name="s5",
    )(*([p] * ncb), *consts)


def _lru_kernel(lx_ref, lg_ref, cw_ref, cb_ref, wg_ref, bg_ref, sp_ref, o_ref, tail_ref, h_ref, *, ts):
    @pl.when(pl.program_id(1) == 0)
    def _():
        tail_ref[...] = jnp.zeros_like(tail_ref)
        h_ref[...] = jnp.zeros_like(h_ref)

    x = lx_ref[...]
    xe = jnp.concatenate([tail_ref[...], x], axis=0)
    xc = cb_ref[...] + cw_ref[LRU_CONV - 1:LRU_CONV, :] * x
    for j in range(1, LRU_CONV):
        xc = xc + cw_ref[LRU_CONV - 1 - j:LRU_CONV - j, :] * pltpu.roll(xe, j, 0)[8:, :]
    tail_ref[...] = x[ts - 8:, :]
    gates = jax.nn.sigmoid(_dot3(xc, wg_ref[0], wg_ref[1]) + bg_ref[...])
    r = gates[:, :LRU_WIDTH]
    ig = gates[:, LRU_WIDTH:]
    log_a = -LRU_C * r * sp_ref[...]
    a = jnp.exp(log_a)
    b = jnp.sqrt(jnp.maximum(1.0 - jnp.exp(2.0 * log_a), 0.0)) * (ig * xc)
    k = 1
    while k < ts:
        b = b + a * _shift_rows(b, k, 0.0)
        a = a * _shift_rows(a, k, 1.0)
        k *= 2
    h = b + a * h_ref[0:1, :]
    h_ref[0:1, :] = h[ts - 1:ts, :]
    o_ref[...] = h * jax.nn.gelu(lg_ref[...])


def _block_diag(w):
    nb, bw, _ = w.shape
    eye = jnp.eye(nb, dtype=w.dtype)
    return (eye[:, None, :, None] * w[:, :, None, :]).reshape(nb * bw, nb * bw)


def _lru(p, conv_w, conv_b, wa, ba, wx, bx, a_param, bsz, seq):
    ts = min(512, seq)
    nt = seq // ts
    wg = jnp.concatenate([_block_diag(wa), _block_diag(wx)], axis=1)
    wg_hi = wg.astype(BF16)
    wg2 = jnp.stack([wg_hi, (wg - wg_hi.astype(F32)).astype(BF16)])
    bg = jnp.concatenate([ba, bx]).reshape(1, -1)
    sp = jax.nn.softplus(a_param.astype(F32)).reshape(1, -1)
    col = lambda cb: pl.BlockSpec((ts, LRU_WIDTH), lambda b, t: (b * nt + t, cb))
    full = lambda a: pl.BlockSpec(a.shape, lambda b, t: (0,) * a.ndim)
    cb2 = conv_b.reshape(1, -1)
    return pl.pallas_call(
        functools.partial(_lru_kernel, ts=ts),
        grid=(bsz, nt),
        in_specs=[col(COL_LX // 512), col(COL_LG // 512), full(conv_w), full(cb2), full(wg2), full(bg), full(sp)],
        out_specs=pl.BlockSpec((ts, LRU_WIDTH), lambda b, t: (b * nt + t, 0)),
        out_shape=jax.ShapeDtypeStruct((bsz * seq, LRU_WIDTH), F32),
        scratch_shapes=[pltpu.VMEM((8, LRU_WIDTH), F32), pltpu.VMEM((8, LRU_WIDTH), F32)],
        compiler_params=_cparams(("parallel", "arbitrary")),
        name="rglru",
    )(p, p, conv_w, cb2, wg2, bg, sp)


def _layer_norm(x, w, b):
    xc = x - jnp.mean(x, axis=-1, keepdims=True)
    var = jnp.mean(xc * xc, axis=-1, keepdims=True)
    return xc * lax.rsqrt(var + LN_EPS) * w + b


def _merge_kernel(ya_ref, yb_ref, ys_ref, yd_ref, gl_ref, x_ref, wbr_ref, wout_ref, wglu_ref, bglu_ref,
                  bgate_ref, lnw_ref, lnb_ref, rw_ref, rb_ref, h_ref, hb_ref, lg_ref):
    yc = jax.nn.gelu(jnp.concatenate([ys_ref[c] for c in range(S5_WIDTH // 128)], axis=-1))
    yc = yc * jax.nn.sigmoid(_dot(yc.astype(BF16), wglu_ref[...]) + bglu_ref[...])
    branches = (ya_ref[...].astype(BF16), yb_ref[...], yc.astype(BF16), yd_ref[...].astype(BF16))
    mix = None
    for bi, yb in enumerate(branches):
        cs = slice(bi * D_MODEL, (bi + 1) * D_MODEL)
        gate = jax.nn.sigmoid(gl_ref[:, cs].astype(F32) + bgate_ref[:, cs])
        term = gate * _dot(yb, wbr_ref[bi])
        mix = term if mix is None else mix + term
    mo = _dot(mix.astype(BF16), wout_ref[...])
    h = _layer_norm(DEEPNORM_ALPHA * x_ref[...] + mo, lnw_ref[...], lnb_ref[...])
    h_ref[...] = h
    h_hi, h_lo = _split(h)
    hb_ref[...] = _pack_pairs(h)
    hcat = jnp.concatenate([h_hi, h_lo, h_hi], axis=-1)
    lg_ref[...] = _dot_nt(rw_ref[...], hcat) + rb_ref[...]


def _merge(ya, yb, ys, yd, gl, x, w_branch, w_out, w_glu, b_glu, b_gate, ln_w, ln_b, router_w, router_b):
    n = x.shape[0]
    tm = min(256, n)
    rwt = router_w.T.astype(F32)
    rw_hi = rwt.astype(BF16)
    rw_lo = (rwt - rw_hi.astype(F32)).astype(BF16)
    rw3 = jnp.concatenate([rw_hi, rw_hi, rw_lo], axis=1)
    row = lambda w: pl.BlockSpec((tm, w), lambda i: (i, 0))
    full = lambda a: pl.BlockSpec(a.shape, lambda i: (0,) * a.ndim)
    consts = [w_branch.astype(BF16), w_out.astype(BF16), w_glu.astype(BF16), b_glu.reshape(1, -1),
              b_gate.reshape(1, -1), ln_w.reshape(1, -1), ln_b.reshape(1, -1), rw3, router_b.reshape(-1, 1)]
    return pl.pallas_call(
        _merge_kernel,
        grid=(n // tm,),
        in_specs=[row(512), row(512), pl.BlockSpec((S5_WIDTH // 128, tm, 128), lambda i: (0, i, 0)), row(512),
                  row(N_BRANCH * D_MODEL), row(D_MODEL)]
                 + [full(c) for c in consts],
        out_specs=[row(D_MODEL), row(D_MODEL // 2), pl.BlockSpec((N_EXPERTS, tm), lambda i: (0, i))],
        out_shape=[jax.ShapeDtypeStruct((n, D_MODEL), F32), jax.ShapeDtypeStruct((n, D_MODEL // 2), jnp.uint32),
                   jax.ShapeDtypeStruct((N_EXPERTS, n), F32)],
        compiler_params=_cparams(("parallel",)),
        name="merge",
    )(ya, yb, ys, yd, gl, x, *consts)


def _route_kernel(lg_ref, tri_ref, idx_ref, w_ref, rank_ref, cnt_ref, carry_ref, *, tr):
    @pl.when(pl.program_id(0) == 0)
    def _():
        carry_ref[...] = jnp.zeros_like(carry_ref)

    l = lg_ref[...]
    eidx = lax.broadcasted_iota(jnp.int32, l.shape, 0)
    vals, hots = [], []
    for _ in range(TOP_K):
        m = jnp.max(l, axis=0, keepdims=True)
        first = jnp.min(jnp.where(l == m, eidx, N_EXPERTS), axis=0, keepdims=True)
        hot = eidx == first
        l = jnp.where(hot, -jnp.inf, l)
        vals.append(m)
        hots.append(hot)
        idx_ref[len(vals) - 1:len(vals), :] = first
    ex = [jnp.exp(v - vals[0]) for v in vals]
    den = ex[0] + ex[1] + ex[2] + ex[3]
    for k in range(TOP_K):
        w_ref[k:k + 1, :] = ex[k] / den
    member = jnp.zeros(l.shape, F32)
    for hot in hots:
        member = member + hot.astype(F32)
    before = _dot(member.astype(BF16), tri_ref[...]) + carry_ref[:, 0:1]
    for k in range(TOP_K):
        rank = jnp.sum(jnp.where(hots[k], before, 0.0), axis=0, keepdims=True)
        rank_ref[k:k + 1, :] = rank.astype(jnp.int32)
    carry_ref[...] = carry_ref[...] + jnp.sum(member, axis=1, keepdims=True)
    cnt_ref[...] = carry_ref[...]
    idx_ref[TOP_K:, :] = jnp.zeros((8 - TOP_K, tr), jnp.int32)
    w_ref[TOP_K:, :] = jnp.zeros((8 - TOP_K, tr), F32)
    rank_ref[TOP_K:, :] = jnp.zeros((8 - TOP_K, tr), jnp.int32)


def _route(logits_t):
    n = logits_t.shape[1]
    tr = min(512, n)
    tri = jnp.asarray(np.triu(np.ones((tr, tr), np.float32), 1), BF16)
    tok = pl.BlockSpec((8, tr), lambda i: (0, i))
    return pl.pallas_call(
        functools.partial(_route_kernel, tr=tr),
        grid=(n // tr,),
        in_specs=[pl.BlockSpec((N_EXPERTS, tr), lambda i: (0, i)), pl.BlockSpec((tr, tr), lambda i: (0, 0))],
        out_specs=[tok, tok, tok, pl.BlockSpec((N_EXPERTS, 128), lambda i: (0, 0))],
        out_shape=[jax.ShapeDtypeStruct((8, n), jnp.int32), jax.ShapeDtypeStruct((8, n), F32),
                   jax.ShapeDtypeStruct((8, n), jnp.int32), jax.ShapeDtypeStruct((N_EXPERTS, 128), F32)],
        scratch_shapes=[pltpu.VMEM((N_EXPERTS, 128), F32)],
        compiler_params=_cparams(("arbitrary",)),
        name="route",
    )(logits_t, tri)


def _expert_kernel(be_ref, nu_ref, x_ref, w1_ref, b1_ref, w2_ref, b2_ref, o_ref, w1b_ref, w2b_ref):
    i = pl.program_id(0)

    @pl.when(i < nu_ref[0])
    def _():
        @pl.when(jnp.logical_or(i == 0, be_ref[i] != be_ref[jnp.maximum(i - 1, 0)]))
        def _():
            w1b_ref[...] = w1_ref[0].astype(BF16)
            w2b_ref[...] = w2_ref[0].astype(BF16)

        h1 = _dot(_unpack_pairs(x_ref[...]).astype(BF16), w1b_ref[...]) + b1_ref[0]
        glu = jnp.minimum(h1[:, :EXPERT_FF], SWIGLU_LIMIT)
        lin = jnp.clip(h1[:, EXPERT_FF:], -SWIGLU_LIMIT, SWIGLU_LIMIT)
        act = glu * jax.nn.sigmoid(SWIGLU_ALPHA * glu) * (lin + 1.0)
        o_ref[...] = _pack_pairs(_dot(act.astype(BF16), w2b_ref[...]) + b2_ref[0])

    @pl.when(pl.program_id(0) >= nu_ref[0])
    def _():
        o_ref[...] = jnp.zeros_like(o_ref)


def _experts(x_slots, block_expert, n_used, w1, b1, w2, b2, expert_offset):
    n_slots = x_slots.shape[0]
    tm = EXPERT_TILE
    n_blocks = n_slots // tm
    off = expert_offset
    grid_spec = pltpu.PrefetchScalarGridSpec(
        num_scalar_prefetch=2,
        grid=(n_blocks,),
        in_specs=[pl.BlockSpec((tm, D_MODEL // 2), lambda i, be, nu: (i, 0)),
                  pl.BlockSpec((1, D_MODEL, 2 * EXPERT_FF), lambda i, be, nu: (be[i] + off, 0, 0)),
                  pl.BlockSpec((1, 1, 2 * EXPERT_FF), lambda i, be, nu: (be[i], 0, 0)),
                  pl.BlockSpec((1, EXPERT_FF, D_MODEL), lambda i, be, nu: (be[i] + off, 0, 0)),
                  pl.BlockSpec((1, 1, D_MODEL), lambda i, be, nu: (be[i], 0, 0))],
        out_specs=pl.BlockSpec((tm, D_MODEL // 2), lambda i, be, nu: (i, 0)),
        scratch_shapes=[pltpu.VMEM((D_MODEL, 2 * EXPERT_FF), BF16), pltpu.VMEM((EXPERT_FF, D_MODEL), BF16)],
    )
    return pl.pallas_call(
        _expert_kernel,
        grid_spec=grid_spec,
        out_shape=jax.ShapeDtypeStruct((n_slots, D_MODEL // 2), jnp.uint32),
        compiler_params=_cparams(("arbitrary",)),
        name="experts",
    )(block_expert, n_used, x_slots, w1, b1.reshape(N_EXPERTS, 1, -1), w2, b2.reshape(N_EXPERTS, 1, -1))


DISPATCH_UNROLL = 8


def _dispatch_kernel(dest_ref, h_ref, init_ref, xs_ref, sem, *, tm):
    del init_ref
    base = pl.program_id(0) * tm

    def row_copy(r, k):
        return pltpu.make_async_copy(h_ref.at[pl.ds(base + r, 1)], xs_ref.at[pl.ds(dest_ref[k, r], 1)], sem)

    def issue(r, carry):
        for k in range(TOP_K):
            row_copy(r, k).start()
        return carry

    lax.fori_loop(0, tm, issue, 0, unroll=DISPATCH_UNROLL)
    for k in range(TOP_K):
        pltpu.make_async_copy(h_ref.at[pl.ds(0, tm)], xs_ref.at[pl.ds(0, tm)], sem).wait()


def _dispatch(h_packed, dest, n_slots):
    n, w = h_packed.shape
    tm = min(256, n)
    return pl.pallas_call(
        functools.partial(_dispatch_kernel, tm=tm),
        grid=(n // tm,),
        in_specs=[pl.BlockSpec((TOP_K, tm), lambda i: (0, i), memory_space=pltpu.SMEM),
                  pl.BlockSpec(memory_space=pl.ANY), pl.BlockSpec(memory_space=pl.ANY)],
        out_specs=pl.BlockSpec(memory_space=pl.ANY),
        out_shape=jax.ShapeDtypeStruct((n_slots, w), jnp.uint32),
        scratch_shapes=[pltpu.SemaphoreType.DMA(())],
        input_output_aliases={2: 0},
        compiler_params=_cparams(("arbitrary",)),
        name="dispatch",
    )(dest, h_packed, jnp.zeros((n_slots, w), jnp.uint32))


def _combine_kernel(dcur_ref, dnxt_ref, h_ref, w_ref, lnw_ref, lnb_ref, ys_ref, o_ref, buf_ref, sem, *, tm):
    i = pl.program_id(0)
    n_steps = pl.num_programs(0)

    def issue(d_ref, slot):
        def body(r, carry):
            for k in range(TOP_K):
                pltpu.make_async_copy(ys_ref.at[pl.ds(d_ref[k, r], 1)], buf_ref.at[slot, k, pl.ds(r, 1)],
                                      sem.at[slot]).start()
            return carry
        lax.fori_loop(0, tm, body, 0, unroll=DISPATCH_UNROLL)

    @pl.when(i == 0)
    def _():
        issue(dcur_ref, 0)

    @pl.when(i + 1 < n_steps)
    def _():
        issue(dnxt_ref, (i + 1) % 2)

    slot = i % 2
    for k in range(TOP_K):
        pltpu.make_async_copy(ys_ref.at[pl.ds(0, tm)], buf_ref.at[slot, k], sem.at[slot]).wait()
    acc = DEEPNORM_ALPHA * h_ref[...]
    for k in range(TOP_K):
        acc = acc + w_ref[:, k:k + 1] * _unpack_pairs(buf_ref[slot, k])
    o_ref[...] = _layer_norm(acc, lnw_ref[...], lnb_ref[...])


def _combine(h, y_slots, dest, w_tok, ln_w, ln_b):
    n = h.shape[0]
    tm = min(256, n)
    n_steps = n // tm
    lnw = ln_w.reshape(1, -1)
    lnb = ln_b.reshape(1, -1)
    return pl.pallas_call(
        functools.partial(_combine_kernel, tm=tm),
        grid=(n_steps,),
        in_specs=[pl.BlockSpec((TOP_K, tm), lambda i: (0, i), memory_space=pltpu.SMEM),
                  pl.BlockSpec((TOP_K, tm), lambda i: (0, jnp.minimum(i + 1, n_steps - 1)),
                               memory_space=pltpu.SMEM),
                  pl.BlockSpec((tm, D_MODEL), lambda i: (i, 0)),
                  pl.BlockSpec((tm, TOP_K), lambda i: (i, 0)),
                  pl.BlockSpec(lnw.shape, lambda i: (0, 0)), pl.BlockSpec(lnb.shape, lambda i: (0, 0)),
                  pl.BlockSpec(memory_space=pl.ANY)],
        out_specs=pl.BlockSpec((tm, D_MODEL), lambda i: (i, 0)),
        out_shape=jax.ShapeDtypeStruct((n, D_MODEL), F32),
        scratch_shapes=[pltpu.VMEM((2, TOP_K, tm, D_MODEL // 2), jnp.uint32), pltpu.SemaphoreType.DMA((2,))],
        compiler_params=_cparams(("arbitrary",)),
        name="combine",
    )(dest, dest, h, w_tok, lnw, lnb, y_slots)


def _moe(h, h16, logits_t, w1, b1, w2, b2, ln_w, ln_b, expert_offset):
    n = h.shape[0]
    tm = EXPERT_TILE
    idx8, w8, rank8, cnt = _route(logits_t)
    idx, w_top, rank = idx8[:TOP_K], w8[:TOP_K], rank8[:TOP_K]
    counts = cnt[:, 0].astype(jnp.int32)
    padded = ((counts + tm - 1) // tm) * tm
    p_end = jnp.cumsum(padded)
    p_start = p_end - padded
    dest = p_start[idx] + rank
    n_slots = n * TOP_K + N_EXPERTS * tm
    n_blocks = n_slots // tm
    starts = jnp.arange(n_blocks, dtype=jnp.int32) * tm
    block_expert = jnp.minimum(jnp.sum((p_end[None, :] <= starts[:, None]).astype(jnp.int32), axis=1),
                               N_EXPERTS - 1)
    n_used = (p_end[-1] // tm).astype(jnp.int32).reshape(1)
    x_slots = _dispatch(h16, dest, n_slots)
    y_slots = _experts(x_slots, block_expert, n_used, w1, b1, w2, b2, expert_offset)
    return _combine(h, y_slots, dest, w_top.T, ln_w, ln_b)


def _mixer_weight(w_in):
    o = np.cumsum((512, 512, 512, 512, MLA_Q_RANK, MLA_KV_RANK + MLA_ROPE, S5_WIDTH, LRU_WIDTH, LRU_WIDTH))
    hgrn, cq, ckv = w_in[:, :o[3]], w_in[:, o[3]:o[4]], w_in[:, o[4]:o[4] + MLA_KV_RANK]
    kpe = w_in[:, o[4] + MLA_KV_RANK:o[5]]
    su, lx, lg = w_in[:, o[5]:o[6]], w_in[:, o[6]:o[7]], w_in[:, o[7]:o[8]]
    pad = jnp.zeros((D_MODEL, 128 - MLA_ROPE), w_in.dtype)
    w_mix = jnp.concatenate([hgrn, su, lx, lg, cq, ckv, kpe, pad, _rot_half_cols(kpe), pad], axis=1)
    return w_mix, w_in[:, o[8]:]


def _layer(x, positions, lb, bsz, seq, w_in, b_gate, hgrn_norm_w, mla_q_norm_w, mla_wq_b, mla_kv_norm_w, mla_wkv_b,
           s5_a_re, s5_a_im, s5_log_dt, s5_b_re, s5_b_im, s5_c_re, s5_c_im, s5_d, s5_w_glu, s5_b_glu,
           lru_conv_w, lru_conv_b, lru_wa, lru_ba, lru_wx, lru_bx, lru_a_param,
           w_branch, w_out, ln1_w, ln1_b, ln2_w, ln2_b, router_w, router_b, moe_w1, moe_b1, moe_w2, moe_b2,
           expert_offset=0):
    w_mix, w_gl = _mixer_weight(w_in)
    p = _matmul(x, w_mix.astype(BF16), 1024, MIX_WIDTH // 3, F32)
    gl = _matmul(x, w_gl.astype(BF16), 1024, 1024, BF16)
    y_a = _hgrn(p, lb, hgrn_norm_w, bsz, seq)
    qn, qp, kn, kp, vt = _mla_proj(p, positions, mla_q_norm_w, mla_wq_b, mla_kv_norm_w, mla_wkv_b, min(256, seq))
    y_b = _flash(qn, qp, kn, kp, vt, bsz, seq)
    y_s = _s5(p, s5_a_re, s5_a_im, s5_log_dt, s5_b_re, s5_b_im, s5_c_re, s5_c_im, s5_d, bsz, seq)
    y_d = _lru(p, lru_conv_w, lru_conv_b, lru_wa, lru_ba, lru_wx, lru_bx, lru_a_param, bsz, seq)
    h, h16, logits_t = _merge(y_a, y_b, y_s, y_d, gl, x, w_branch, w_out, s5_w_glu, s5_b_glu, b_gate,
                              ln1_w, ln1_b, router_w, router_b)
    return _moe(h, h16, logits_t, moe_w1, moe_b1, moe_w2, moe_b2, ln2_w, ln2_b, expert_offset)


def kernel(x, positions, w_in, b_gate, hgrn_lb_logits, hgrn_norm_w, mla_q_norm_w, mla_wq_b, mla_kv_norm_w, mla_wkv_b, s5_a_re, s5_a_im, s5_log_dt, s5_b_re, s5_b_im, s5_c_re, s5_c_im, s5_d, s5_w_glu, s5_b_glu, lru_conv_w, lru_conv_b, lru_wa, lru_ba, lru_wx, lru_bx, lru_a_param, w_branch, w_out, ln1_w, ln1_b, ln2_w, ln2_b, router_w, router_b, moe_w1, moe_b1, moe_w2, moe_b2):
    bsz, seq, _ = x.shape
    probs = jax.nn.softmax(hgrn_lb_logits.astype(F32), axis=0)
    lower_bounds = jnp.cumsum(probs, axis=0) - probs[0:1]
    per_layer = (w_in, b_gate, hgrn_norm_w, mla_q_norm_w, mla_wq_b, mla_kv_norm_w, mla_wkv_b,
                 s5_a_re, s5_a_im, s5_log_dt, s5_b_re, s5_b_im, s5_c_re, s5_c_im, s5_d, s5_w_glu, s5_b_glu,
                 lru_conv_w, lru_conv_b, lru_wa, lru_ba, lru_wx, lru_bx, lru_a_param,
                 w_branch, w_out, ln1_w, ln1_b, ln2_w, ln2_b, router_w, router_b, moe_w1, moe_b1, moe_w2, moe_b2)
    xf = x.reshape(bsz * seq, D_MODEL)
    w1_all = moe_w1.reshape((DEPTH * N_EXPERTS,) + moe_w1.shape[2:])
    w2_all = moe_w2.reshape((DEPTH * N_EXPERTS,) + moe_w2.shape[2:])
    for l in range(DEPTH):
        args = [a[l] for a in per_layer]
        args[-4], args[-2] = w1_all, w2_all
        xf = _layer(xf, positions, lower_bounds[l], bsz, seq, *args, expert_offset=l * N_EXPERTS)
    return xf.reshape(bsz, seq, D_MODEL)
```

```python
import functools
import math

import numpy as np
import jax
import jax.numpy as jnp
from jax import lax
from jax.experimental import pallas as pl
from jax.experimental.pallas import tpu as pltpu

F32 = jnp.float32
BF16 = jnp.bfloat16

D_MODEL = 1024
DEPTH = 2
CHUNK = 64

HGRN_HEADS = 4
HGRN_DK = 128
HGRN_WIDTH = 512
HGRN_F_MIN = 1e-30
HGRN_CHUNK = 64
HGRN_LEVELS = (1, 2, 4, 8, 16, 32)

MLA_HEADS = 4
MLA_Q_RANK = 256
MLA_KV_RANK = 128
MLA_NOPE = 128
MLA_ROPE = 64
MLA_V = 128
MLA_QK = MLA_NOPE + MLA_ROPE
MLA_VROWS = MLA_V + 16
ROPE_THETA = 10000.0

S5_GROUPS = 32
S5_GROUP_CH = 16
S5_STATE = 64
S5_WIDTH = 512
S5_BLOCK = 16

LRU_WIDTH = 512
LRU_BLOCKS = 8
LRU_BLOCK_W = 64
LRU_CONV = 4
LRU_C = 8.0

N_BRANCH = 4
BRANCH_WIDTH = 512

N_EXPERTS = 32
TOP_K = 4
EXPERT_FF = 1024
SWIGLU_ALPHA = 1.702
SWIGLU_LIMIT = 7.0
EXPERT_TILE = 512

DEEPNORM_ALPHA = (2.0 * DEPTH) ** 0.25
LN_EPS = 1e-5
RMS_EPS = 1e-6

COL_HQ, COL_HF, COL_HI, COL_HG = 0, 512, 1024, 1536
COL_SU, COL_LX, COL_LG = 2048, 2560, 3072
COL_CQ, COL_CKV, COL_KPE, COL_KPER = 3584, 3840, 3968, 4096
MIX_WIDTH = 4224

VMEM_LIMIT = 56 * 1024 * 1024


def _cparams(sem):
    return pltpu.CompilerParams(dimension_semantics=sem, vmem_limit_bytes=VMEM_LIMIT)


def _dot(a, b):
    return jnp.dot(a, b, preferred_element_type=F32)


def _dot_nt(a, b):
    return lax.dot_general(a, b, (((1,), (1,)), ((), ())), preferred_element_type=F32)


def _dot_tn(a, b):
    return lax.dot_general(a, b, (((0,), (0,)), ((), ())), preferred_element_type=F32)


def _split(x):
    hi = x.astype(BF16)
    lo = (x - hi.astype(F32)).astype(BF16)
    return hi, lo


def _dot3(a, b_hi, b_lo):
    a_hi, a_lo = _split(a)
    return _dot(a_hi, b_hi) + (_dot(a_lo, b_hi) + _dot(a_hi, b_lo))


def _pack_pairs(x):
    w = x.shape[-1] // 2
    lo = lax.bitcast_convert_type(x[:, :w].astype(BF16).astype(F32), jnp.uint32)
    hi = lax.bitcast_convert_type(x[:, w:].astype(BF16).astype(F32), jnp.uint32)
    return lax.shift_right_logical(lo, jnp.uint32(16)) | (hi & jnp.uint32(0xFFFF0000))


def _unpack_pairs(u):
    lo = lax.bitcast_convert_type(lax.shift_left(u, jnp.uint32(16)), F32)
    hi = lax.bitcast_convert_type(u & jnp.uint32(0xFFFF0000), F32)
    return jnp.concatenate([lo, hi], axis=-1)


def _shift_rows(x, k, fill):
    rows = lax.broadcasted_iota(jnp.int32, x.shape, 0)
    return jnp.where(rows >= k, pltpu.roll(x, k, 0), fill)


def _mm_kernel(x_ref, w_ref, o_ref):
    o_ref[...] = _dot(x_ref[...].astype(BF16), w_ref[...]).astype(o_ref.dtype)


def _matmul(x, w, tm, tn, out_dtype):
    n, k = x.shape
    m = w.shape[1]
    tm = min(tm, n)
    return pl.pallas_call(
        _mm_kernel,
        grid=(n // tm, m // tn),
        in_specs=[pl.BlockSpec((tm, k), lambda i, j: (i, 0)),
                  pl.BlockSpec((k, tn), lambda i, j: (0, j))],
        out_specs=pl.BlockSpec((tm, tn), lambda i, j: (i, j)),
        out_shape=jax.ShapeDtypeStruct((n, m), out_dtype),
        compiler_params=_cparams(("parallel", "arbitrary")),
        name="in_proj",
    )(x, w)


def _hgrn_tables():
    c = HGRN_CHUNK
    t = np.arange(c)
    windows = [np.tril(np.ones((c, c), bool)),
               np.triu(np.ones((c, c), bool), 1)]
    masks = [np.eye(c, dtype=bool)]
    for h in HGRN_LEVELS:
        blk = t // h
        odd = blk % 2 == 1
        masks.append(odd[:, None] & (blk[None, :] == blk[:, None] - 1))
        if h > 1:
            windows.append((t[None, :] >= (h * blk)[:, None]) & (t[None, :] <= t[:, None]))
            windows.append((t[None, :] > t[:, None]) & (t[None, :] <= (h * blk + h - 1)[:, None]))
    return np.concatenate(windows, 0).astype(np.float32), np.stack(masks).astype(np.float32)


def _hgrn_kernel(hq_ref, hf_ref, hi_ref, hg_ref, lb_ref, nw_ref, win_ref, mask_ref,
                 o_ref, state_ref, *, n_chunks):
    c = HGRN_CHUNK
    heads = range(HGRN_HEADS)
    hs = [slice(h * HGRN_DK, (h + 1) * HGRN_DK) for h in heads]
    n_lvl = len(HGRN_LEVELS) + 1

    @pl.when(pl.program_id(1) == 0)
    def _():
        state_ref[...] = jnp.zeros_like(state_ref)

    def chunk_body(ci, carry):
        rows = pl.ds(pl.multiple_of(ci * c, c), c)
        lb = lb_ref[...]
        z = hf_ref[rows, :]
        f = jnp.maximum(lb + (1.0 - lb) * jax.nn.sigmoid(z), HGRN_F_MIN)
        k = (1.0 - lb) * jax.nn.sigmoid(-z)
        q = jax.nn.silu(hq_ref[rows, :])
        v16 = hi_ref[rows, :].astype(BF16)
        lf_hi, lf_lo = _split(jnp.log(f))
        win = win_ref[...]
        e = jnp.exp(_dot(win, lf_hi) + _dot(win, lf_lo))
        blk = lambda w: e[w * c:(w + 1) * c]
        k16 = k.astype(BF16)
        qs = [q.astype(BF16), (q * f).astype(BF16)] + [(q * blk(2 * w)).astype(BF16) for w in range(1, n_lvl - 1)]
        ks = [k16, k16] + [(k * blk(2 * w + 1)).astype(BF16) for w in range(1, n_lvl - 1)]
        q_in = (q * blk(0)).astype(BF16)
        k_out = (k * blk(1)).astype(BF16)
        decay = e[c - 1:c, :]
        scores = [None] * HGRN_HEADS
        for lv in range(n_lvl):
            s_l = [_dot_nt(qs[lv][:, hs[h]], ks[lv][:, hs[h]]) for h in heads]
            for h in heads:
                term = mask_ref[lv] * s_l[h]
                scores[h] = term if scores[h] is None else scores[h] + term
        st = [state_ref[h] for h in heads]
        o = [_dot(scores[h].astype(BF16), v16[:, hs[h]]) + _dot_nt(q_in[:, hs[h]], st[h].astype(BF16)) for h in heads]
        upd = [_dot_tn(v16[:, hs[h]], k_out[:, hs[h]]) for h in heads]
        for h in heads:
            state_ref[h] = st[h] * decay[:, hs[h]] + upd[h]
        o = [o[h] * lax.rsqrt(jnp.mean(o[h] * o[h], axis=-1, keepdims=True) + RMS_EPS) for h in heads]
        o_ref[rows, :] = jnp.concatenate(o, axis=-1) * nw_ref[...] * jax.nn.silu(hg_ref[rows, :])
        return carry

    lax.fori_loop(0, n_chunks, chunk_body, 0)


def _hgrn(p, lb, norm_w, bsz, seq):
    t = min(256, seq)
    nt = seq // t
    win, mask = _hgrn_tables()
    col = lambda cb: pl.BlockSpec((t, HGRN_WIDTH), lambda b, i: (b * nt + i, cb))
    const2 = lambda shape: pl.BlockSpec(shape, lambda b, i: (0,) * len(shape))
    return pl.pallas_call(
        functools.partial(_hgrn_kernel, n_chunks=t // HGRN_CHUNK),
        grid=(bsz, nt),
        in_specs=[col(COL_HQ // 512), col(COL_HF // 512), col(COL_HI // 512), col(COL_HG // 512),
                  const2((1, HGRN_WIDTH)), const2((1, HGRN_WIDTH)), const2(win.shape), const2(mask.shape)],
        out_specs=pl.BlockSpec((t, HGRN_WIDTH), lambda b, i: (b * nt + i, 0)),
        out_shape=jax.ShapeDtypeStruct((bsz * seq, HGRN_WIDTH), F32),
        scratch_shapes=[pltpu.VMEM((HGRN_HEADS, HGRN_DK, HGRN_DK), F32)],
        compiler_params=_cparams(("parallel", "arbitrary")),
        name="hgrn",
    )(p, p, p, p, lb.reshape(1, -1), norm_w.reshape(1, -1), jnp.asarray(win, BF16), jnp.asarray(mask))


def _rms(x, w):
    ms = jnp.mean(x * x, axis=-1, keepdims=True)
    return x * lax.rsqrt(ms + RMS_EPS) * w


def _mla_proj_kernel(cq_ref, ckv_ref, kpe_ref, kper_ref, ang_ref, qnw_ref, kvnw_ref,
                     wqn_ref, wqp_ref, wqpr_ref, wkn_ref, wv_ref,
                     qn_ref, qp_ref, kn_ref, kp_ref, v_ref):
    ang = ang_ref[...]
    cos = jnp.cos(ang)
    sin = jnp.sin(ang)
    qn = _rms(cq_ref[...], qnw_ref[...]).astype(BF16)
    cos4 = jnp.concatenate([cos] * MLA_HEADS, axis=-1)
    sin4 = jnp.concatenate([sin] * MLA_HEADS, axis=-1)
    qn_ref[...] = _dot(qn, wqn_ref[...]).astype(BF16)
    qp_ref[...] = (_dot(qn, wqp_ref[...]) * cos4 + _dot(qn, wqpr_ref[...]) * sin4).astype(BF16)
    cn = _rms(ckv_ref[...], kvnw_ref[...]).astype(BF16)
    kn_ref[...] = _dot(cn, wkn_ref[...]).astype(BF16)
    vt = _dot_nt(wv_ref[...], cn)
    ones = jnp.ones((MLA_VROWS - MLA_V, vt.shape[1]), F32)
    v_ref[0] = jnp.concatenate(
        [piece for h in range(MLA_HEADS) for piece in (vt[h * MLA_V:(h + 1) * MLA_V], ones)], axis=0).astype(BF16)
    kp_ref[...] = (kpe_ref[...] * cos + kper_ref[...] * sin).astype(BF16)


def _rot_half_cols(w):
    half = w.shape[-1] // 2
    return jnp.concatenate([-w[..., half:], w[..., :half]], axis=-1)


def _mla_proj(p, positions, q_norm_w, wq_b, kv_norm_w, wkv_b, tm):
    n = p.shape[0]
    scale = MLA_QK ** -0.5 * math.log2(math.e)
    wq = wq_b.reshape(MLA_Q_RANK, MLA_HEADS, MLA_QK) * scale
    w_nope = wq[:, :, :MLA_NOPE].reshape(MLA_Q_RANK, MLA_HEADS * MLA_NOPE)
    w_pe = wq[:, :, MLA_NOPE:]
    pad = jnp.zeros((MLA_Q_RANK, MLA_HEADS, 128 - MLA_ROPE), F32)
    w_pe_p = jnp.concatenate([w_pe, pad], axis=-1).reshape(MLA_Q_RANK, MLA_HEADS * 128)
    w_per_p = jnp.concatenate([_rot_half_cols(w_pe), pad], axis=-1).reshape(MLA_Q_RANK, MLA_HEADS * 128)
    wkv = wkv_b.reshape(MLA_KV_RANK, MLA_HEADS, MLA_NOPE + MLA_V)
    w_kn = wkv[:, :, :MLA_NOPE].reshape(MLA_KV_RANK, MLA_HEADS * MLA_NOPE)
    w_v = wkv[:, :, MLA_NOPE:].reshape(MLA_KV_RANK, MLA_HEADS * MLA_V)
    inv_freq = ROPE_THETA ** (-jnp.arange(0, MLA_ROPE, 2, dtype=F32) / MLA_ROPE)
    freq_lane = jnp.concatenate([inv_freq, inv_freq, jnp.zeros((128 - MLA_ROPE,), F32)])
    ang = positions.reshape(n, 1).astype(F32) * freq_lane[None, :]

    row = lambda w, cb: pl.BlockSpec((tm, w), lambda i: (i, cb))
    full = lambda a: pl.BlockSpec(a.shape, lambda i: (0,) * a.ndim)
    weights = [w_nope.astype(BF16), w_pe_p.astype(BF16), w_per_p.astype(BF16), w_kn.astype(BF16),
               w_v.T.astype(BF16)]
    qnw = q_norm_w.reshape(1, -1)
    kvnw = kv_norm_w.reshape(1, -1)
    outs = pl.pallas_call(
        _mla_proj_kernel,
        grid=(n // tm,),
        in_specs=[row(256, COL_CQ // 256), row(128, COL_CKV // 128), row(128, COL_KPE // 128),
                  row(128, COL_KPER // 128), row(128, 0), full(qnw), full(kvnw)] + [full(w) for w in weights],
        out_specs=[row(512, 0), row(512, 0), row(512, 0), row(128, 0),
                   pl.BlockSpec((1, MLA_HEADS * MLA_VROWS, tm), lambda i: (i, 0, 0))],
        out_shape=[jax.ShapeDtypeStruct((n, 512), BF16), jax.ShapeDtypeStruct((n, 512), BF16),
                   jax.ShapeDtypeStruct((n, 512), BF16), jax.ShapeDtypeStruct((n, 128), BF16),
                   jax.ShapeDtypeStruct((n // tm, MLA_HEADS * MLA_VROWS, tm), BF16)],
        compiler_params=_cparams(("parallel",)),
        name="mla_proj",
    )(p, p, p, p, ang, qnw, kvnw, *weights)
    return outs


def _flash_kernel(qn_ref, qp_ref, kn_ref, kp_ref, vt_ref, o_ref, acc_ref, *, tq):
    i = pl.program_id(1)
    heads = range(MLA_HEADS)
    hs = [slice(h * 128, (h + 1) * 128) for h in heads]
    vs = [slice(h * MLA_VROWS, (h + 1) * MLA_VROWS) for h in heads]
    acc_ref[...] = jnp.zeros_like(acc_ref)

    def scores(j):
        r0 = pl.multiple_of(j * tq, tq)
        kp = kp_ref[pl.ds(r0, tq), :]
        return tuple(_dot_nt(jnp.concatenate([kn_ref[pl.ds(r0, tq), hs[h]], kp], axis=-1),
                             jnp.concatenate([qn_ref[:, hs[h]], qp_ref[:, hs[h]]], axis=-1)) for h in heads)

    def consume(j, s, m, mask):
        p, m_new, alpha = [], [], []
        for h in heads:
            sh = s[h] if mask is None else jnp.where(mask, s[h], -1e30)
            mn = jnp.maximum(m[h], jnp.max(sh, axis=0, keepdims=True))
            alpha.append(jnp.exp2(m[h] - mn))
            p.append(jnp.exp2(sh - mn).astype(BF16))
            m_new.append(mn)
        vt = vt_ref[j]
        pv = [_dot(vt[vs[h], :], p[h]) for h in heads]
        for h in heads:
            acc_ref[h] = acc_ref[h] * alpha[h] + pv[h]
        return tuple(m_new)

    def body(j, carry):
        s, m = carry
        s_next = scores(j + 1)
        return s_next, consume(j, s, m, None)

    m0 = tuple(jnp.full((1, tq), -1e30, F32) for _ in heads)
    s, m = lax.fori_loop(0, i, body, (scores(0), m0))
    shift = CHUNK.bit_length() - 1
    kc = lax.shift_right_logical(lax.broadcasted_iota(jnp.int32, (tq, tq), 0), shift)
    qc = lax.shift_right_logical(lax.broadcasted_iota(jnp.int32, (tq, tq), 1), shift)
    consume(i, s, m, kc <= qc)
    for h in heads:
        acc = acc_ref[h]
        o_ref[:, hs[h]] = (acc[:MLA_V] / acc[MLA_V:MLA_V + 1]).T.astype(o_ref.dtype)


def _flash(qn, qp, kn, kp, vt, bsz, seq):
    tq = vt.shape[2]
    nq = seq // tq
    width = MLA_HEADS * 128
    qspec = pl.BlockSpec((tq, width), lambda b, i: (b * nq + i, 0))
    kspec = pl.BlockSpec((seq, width), lambda b, i: (b, 0))
    kpspec = pl.BlockSpec((seq, 128), lambda b, i: (b, 0))
    vspec = pl.BlockSpec((nq, MLA_HEADS * MLA_VROWS, tq), lambda b, i: (b, 0, 0))
    return pl.pallas_call(
        functools.partial(_flash_kernel, tq=tq),
        grid=(bsz, nq),
        in_specs=[qspec, qspec, kspec, kpspec, vspec],
        out_specs=qspec,
        out_shape=jax.ShapeDtypeStruct((bsz * seq, width), BF16),
        scratch_shapes=[pltpu.VMEM((MLA_HEADS, MLA_VROWS, tq), F32)],
        compiler_params=_cparams(("parallel", "arbitrary")),
        name="flash",
    )(qn, qp, kn, kp, vt)


def _cmul(a, b):
    return a[0] * b[0] - a[1] * b[1], a[0] * b[1] + a[1] * b[0]


def _s5_tables(a_re, a_im, log_dt, b_re, b_im, c_re, c_im, d_skip, steps):
    lb = S5_BLOCK
    f32 = lambda v: v.astype(F32)
    a_re, a_im, b_re, b_im, c_re, c_im = map(f32, (a_re, a_im, b_re, b_im, c_re, c_im))
    dt = jnp.exp(f32(log_dt))[:, None]
    ld = (a_re * dt, a_im * dt)

    def power(tau):
        mag = jnp.exp(ld[0] * tau)
        return mag * jnp.cos(ld[1] * tau), mag * jnp.sin(ld[1] * tau)

    lam_bar = power(1.0)
    inv = a_re * a_re + a_im * a_im
    ratio = _cmul((lam_bar[0] - 1.0, lam_bar[1]), (a_re / inv, -a_im / inv))
    b_bar = _cmul((ratio[0][..., None], ratio[1][..., None]), (b_re, b_im))
    tau = jnp.arange(lb + 1, dtype=F32)[:, None, None]
    pw = power(tau)
    cp = _cmul((c_re[None], c_im[None]), (pw[0][:, :, None, :], pw[1][:, :, None, :]))
    kern = jnp.einsum('tgcn,gnd->tgcd', cp[0][:lb], b_bar[0]) - jnp.einsum('tgcn,gnd->tgcd', cp[1][:lb], b_bar[1])
    s_idx = np.arange(lb)[:, None]
    t_idx = np.arange(lb)[None, :]
    lag = np.clip(t_idx - s_idx, 0, lb - 1)
    causal = jnp.asarray((t_idx >= s_idx).astype(np.float32))
    toe = kern[lag] * causal[:, :, None, None, None]
    skip = jnp.asarray(np.eye(lb, dtype=np.float32))[:, :, None, None, None] * (
        d_skip.astype(F32)[None, None, :, :, None] * jnp.eye(S5_GROUP_CH, dtype=F32)[None, None, None])
    toe = (toe + skip).transpose(2, 1, 3, 0, 4).reshape(S5_GROUPS, lb * S5_GROUP_CH, lb * S5_GROUP_CH)
    emit = jnp.concatenate([cp[0][1:], -cp[1][1:]], axis=-1)
    emit = emit.transpose(1, 0, 2, 3).reshape(S5_GROUPS, lb * S5_GROUP_CH, 2 * S5_STATE)
    rev = (pw[0][:lb][::-1][..., None], pw[1][:lb][::-1][..., None])
    bp = _cmul(rev, (b_bar[0][None], b_bar[1][None]))
    fold = jnp.concatenate([bp[0], bp[1]], axis=2)
    fold = fold.transpose(1, 2, 0, 3).reshape(S5_GROUPS, 2 * S5_STATE, lb * S5_GROUP_CH)
    jump = (lb * 2.0 ** jnp.arange(steps, dtype=F32))[:, None, None]
    pj = power(jump)
    pw_re = jnp.concatenate([pj[0], pj[0]], axis=-1).transpose(1, 2, 0)
    pw_im = jnp.concatenate([-pj[1], pj[1]], axis=-1).transpose(1, 2, 0)
    return toe, emit, fold, pw_re, pw_im


def _s5_kernel(su0_ref, su1_ref, su2_ref, su3_ref, toe_ref, emit_ref, fold_ref, pre_ref, pim_ref, y_ref,
               ut_ref, carry_ref, *, r, steps):
    lb = S5_BLOCK
    gc = S5_GROUP_CH

    @pl.when(pl.program_id(1) == 0)
    def _():
        carry_ref[...] = jnp.zeros_like(carry_ref)

    for s in range(lb):
        for c, su_ref in enumerate((su0_ref, su1_ref, su2_ref, su3_ref)):
            ut_ref[s, c * 128:(c + 1) * 128, :] = su_ref[pl.ds(s, r, stride=lb), :].T
    lane = lax.broadcasted_iota(jnp.int32, (2 * S5_STATE, r), 1)
    swap = lambda v: jnp.concatenate([v[S5_STATE:], v[:S5_STATE]], axis=0)

    def group(g, carry):
        g0 = pl.multiple_of(g * gc, gc)
        ug = jnp.concatenate([ut_ref[s, pl.ds(g0, gc), :] for s in range(lb)], axis=0).astype(BF16)
        pre = pre_ref[g]
        pim = pim_ref[g]
        cmul = lambda j, v: pre[:, j:j + 1] * v + pim[:, j:j + 1] * swap(v)
        car = carry_ref[g]
        x = _dot(fold_ref[g], ug)
        x = x + jnp.where(lane == 0, cmul(0, car), 0.0)
        for j in range(steps):
            xs = jnp.where(lane >= 2 ** j, pltpu.roll(x, 2 ** j, 1), 0.0)
            x = x + cmul(j, xs)
        x_prev = jnp.where(lane == 0, car, pltpu.roll(x, 1, 1))
        carry_ref[g] = jnp.broadcast_to(x[:, r - 1:r], x.shape)
        y = _dot(toe_ref[g], ug) + _dot(emit_ref[g], x_prev.astype(BF16))
        for t in range(lb):
            ut_ref[t, pl.ds(g0, gc), :] = y[t * gc:(t + 1) * gc]
        return carry

    lax.fori_loop(0, S5_GROUPS, group, 0)
    for t in range(lb):
        for c in range(S5_WIDTH // 128):
            y_ref[c, pl.ds(t, r, stride=lb), :] = ut_ref[t, c * 128:(c + 1) * 128, :].T


def _s5(p, a_re, a_im, log_dt, b_re, b_im, c_re, c_im, d_skip, bsz, seq):
    lb = S5_BLOCK
    r = min(128, seq // lb)
    nt = seq // (lb * r)
    ncb = S5_WIDTH // 128
    steps = max(1, int(math.ceil(math.log2(r))))
    toe, emit, fold, pw_re, pw_im = _s5_tables(a_re, a_im, log_dt, b_re, b_im, c_re, c_im, d_skip, steps)
    consts = [toe.astype(BF16), emit.astype(BF16), fold.astype(BF16), pw_re, pw_im]
    full = lambda a: pl.BlockSpec(a.shape, lambda b, t: (0,) * a.ndim)
    return pl.pallas_call(
        functools.partial(_s5_kernel, r=r, steps=steps),
        grid=(bsz, nt),
        in_specs=[pl.BlockSpec((lb * r, 128), functools.partial(lambda b, t, c: (b * nt + t, COL_SU // 128 + c), c=c))
                  for c in range(ncb)] + [full(c) for c in consts],
        out_specs=pl.BlockSpec((ncb, lb * r, 128), lambda b, t: (0, b * nt + t, 0)),
        out_shape=jax.ShapeDtypeStruct((ncb, bsz * seq, 128), F32),
        scratch_shapes=[pltpu.VMEM((lb, S5_WIDTH, r), F32), pltpu.VMEM((S5_GROUPS, 2 * S5_STATE, r), F32)],
        compiler_params=_cparams(("parallel", "arbitrary")),
        name="s5",
    )(*([p] * ncb), *consts)


def _lru_kernel(lx_ref, lg_ref, cw_ref, cb_ref, wg_ref, bg_ref, sp_ref, o_ref, tail_ref, h_ref, *, ts):
    @pl.when(pl.program_id(1) == 0)
    def _():
        tail_ref[...] = jnp.zeros_like(tail_ref)
        h_ref[...] = jnp.zeros_like(h_ref)

    x = lx_ref[...]
    xe = jnp.concatenate([tail_ref[...], x], axis=0)
    xc = cb_ref[...] + cw_ref[LRU_CONV - 1:LRU_CONV, :] * x
    for j in range(1, LRU_CONV):
        xc = xc + cw_ref[LRU_CONV - 1 - j:LRU_CONV - j, :] * pltpu.roll(xe, j, 0)[8:, :]
    tail_ref[...] = x[ts - 8:, :]
    gates = jax.nn.sigmoid(_dot3(xc, wg_ref[0], wg_ref[1]) + bg_ref[...])
    r = gates[:, :LRU_WIDTH]
    ig = gates[:, LRU_WIDTH:]
    log_a = -LRU_C * r * sp_ref[...]
    a = jnp.exp(log_a)
    b = jnp.sqrt(jnp.maximum(1.0 - jnp.exp(2.0 * log_a), 0.0)) * (ig * xc)
    k = 1
    while k < ts:
        b = b + a * _shift_rows(b, k, 0.0)
        a = a * _shift_rows(a, k, 1.0)
        k *= 2
    h = b + a * h_ref[0:1, :]
    h_ref[0:1, :] = h[ts - 1:ts, :]
    o_ref[...] = h * jax.nn.gelu(lg_ref[...])


def _block_diag(w):
    nb, bw, _ = w.shape
    eye = jnp.eye(nb, dtype=w.dtype)
    return (eye[:, None, :, None] * w[:, :, None, :]).reshape(nb * bw, nb * bw)


def _lru(p, conv_w, conv_b, wa, ba, wx, bx, a_param, bsz, seq):
    ts = min(512, seq)
    nt = seq // ts
    wg = jnp.concatenate([_block_diag(wa), _block_diag(wx)], axis=1)
    wg_hi = wg.astype(BF16)
    wg2 = jnp.stack([wg_hi, (wg - wg_hi.astype(F32)).astype(BF16)])
    bg = jnp.concatenate([ba, bx]).reshape(1, -1)
    sp = jax.nn.softplus(a_param.astype(F32)).reshape(1, -1)
    col = lambda cb: pl.BlockSpec((ts, LRU_WIDTH), lambda b, t: (b * nt + t, cb))
    full = lambda a: pl.BlockSpec(a.shape, lambda b, t: (0,) * a.ndim)
    cb2 = conv_b.reshape(1, -1)
    return pl.pallas_call(
        functools.partial(_lru_kernel, ts=ts),
        grid=(bsz, nt),
        in_specs=[col(COL_LX // 512), col(COL_LG // 512), full(conv_w), full(cb2), full(wg2), full(bg), full(sp)],
        out_specs=pl.BlockSpec((ts, LRU_WIDTH), lambda b, t: (b * nt + t, 0)),
        out_shape=jax.ShapeDtypeStruct((bsz * seq, LRU_WIDTH), F32),
        scratch_shapes=[pltpu.VMEM((8, LRU_WIDTH), F32), pltpu.VMEM((8, LRU_WIDTH), F32)],
        compiler_params=_cparams(("parallel", "arbitrary")),
        name="rglru",
    )(p, p, conv_w, cb2, wg2, bg, sp)


def _layer_norm(x, w, b):
    xc = x - jnp.mean(x, axis=-1, keepdims=True)
    var = jnp.mean(xc * xc, axis=-1, keepdims=True)
    return xc * lax.rsqrt(var + LN_EPS) * w + b


def _merge_kernel(ya_ref, yb_ref, ys_ref, yd_ref, gl_ref, x_ref, wbr_ref, wout_ref, wglu_ref, bglu_ref,
                  bgate_ref, lnw_ref, lnb_ref, rw_ref, rb_ref, h_ref, hb_ref, lg_ref):
    yc = jax.nn.gelu(jnp.concatenate([ys_ref[c] for c in range(S5_WIDTH // 128)], axis=-1))
    yc = yc * jax.nn.sigmoid(_dot(yc.astype(BF16), wglu_ref[...]) + bglu_ref[...])
    branches = (ya_ref[...].astype(BF16), yb_ref[...], yc.astype(BF16), yd_ref[...].astype(BF16))
    mix = None
    for bi, yb in enumerate(branches):
        cs = slice(bi * D_MODEL, (bi + 1) * D_MODEL)
        gate = jax.nn.sigmoid(gl_ref[:, cs].astype(F32) + bgate_ref[:, cs])
        term = gate * _dot(yb, wbr_ref[bi])
        mix = term if mix is None else mix + term
    mo = _dot(mix.astype(BF16), wout_ref[...])
    h = _layer_norm(DEEPNORM_ALPHA * x_ref[...] + mo, lnw_ref[...], lnb_ref[...])
    h_ref[...] = h
    h_hi, h_lo = _split(h)
    hb_ref[...] = _pack_pairs(h)
    hcat = jnp.concatenate([h_hi, h_lo, h_hi], axis=-1)
    lg_ref[...] = _dot_nt(rw_ref[...], hcat) + rb_ref[...]


def _merge(ya, yb, ys, yd, gl, x, w_branch, w_out, w_glu, b_glu, b_gate, ln_w, ln_b, router_w, router_b):
    n = x.shape[0]
    tm = min(512, n)
    rwt = router_w.T.astype(F32)
    rw_hi = rwt.astype(BF16)
    rw_lo = (rwt - rw_hi.astype(F32)).astype(BF16)
    rw3 = jnp.concatenate([rw_hi, rw_hi, rw_lo], axis=1)
    row = lambda w: pl.BlockSpec((tm, w), lambda i: (i, 0))
    full = lambda a: pl.BlockSpec(a.shape, lambda i: (0,) * a.ndim)
    consts = [w_branch.astype(BF16), w_out.astype(BF16), w_glu.astype(BF16), b_glu.reshape(1, -1),
              b_gate.reshape(1, -1), ln_w.reshape(1, -1), ln_b.reshape(1, -1), rw3, router_b.reshape(-1, 1)]
    return pl.pallas_call(
        _merge_kernel,
        grid=(n // tm,),
        in_specs=[row(512), row(512), pl.BlockSpec((S5_WIDTH // 128, tm, 128), lambda i: (0, i, 0)), row(512),
                  row(N_BRANCH * D_MODEL), row(D_MODEL)]
                 + [full(c) for c in consts],
        out_specs=[row(D_MODEL), row(D_MODEL // 2), pl.BlockSpec((N_EXPERTS, tm), lambda i: (0, i))],
        out_shape=[jax.ShapeDtypeStruct((n, D_MODEL), F32), jax.ShapeDtypeStruct((n, D_MODEL // 2), jnp.uint32),
                   jax.ShapeDtypeStruct((N_EXPERTS, n), F32)],
        compiler_params=_cparams(("parallel",)),
        name="merge",
    )(ya, yb, ys, yd, gl, x, *consts)


def _route_kernel(lg_ref, tri_ref, idx_ref, w_ref, rank_ref, cnt_ref, carry_ref, *, tr):
    @pl.when(pl.program_id(0) == 0)
    def _():
        carry_ref[...] = jnp.zeros_like(carry_ref)

    l = lg_ref[...]
    eidx = lax.broadcasted_iota(jnp.int32, l.shape, 0)
    vals, hots = [], []
    for _ in range(TOP_K):
        m = jnp.max(l, axis=0, keepdims=True)
        first = jnp.min(jnp.where(l == m, eidx, N_EXPERTS), axis=0, keepdims=True)
        hot = eidx == first
        l = jnp.where(hot, -jnp.inf, l)
        vals.append(m)
        hots.append(hot)
        idx_ref[len(vals) - 1:len(vals), :] = first
    ex = [jnp.exp(v - vals[0]) for v in vals]
    den = ex[0] + ex[1] + ex[2] + ex[3]
    for k in range(TOP_K):
        w_ref[k:k + 1, :] = ex[k] / den
    member = jnp.zeros(l.shape, F32)
    for hot in hots:
        member = member + hot.astype(F32)
    before = _dot(member.astype(BF16), tri_ref[...]) + carry_ref[:, 0:1]
    for k in range(TOP_K):
        rank = jnp.sum(jnp.where(hots[k], before, 0.0), axis=0, keepdims=True)
        rank_ref[k:k + 1, :] = rank.astype(jnp.int32)
    carry_ref[...] = carry_ref[...] + jnp.sum(member, axis=1, keepdims=True)
    cnt_ref[...] = carry_ref[...]
    idx_ref[TOP_K:, :] = jnp.zeros((8 - TOP_K, tr), jnp.int32)
    w_ref[TOP_K:, :] = jnp.zeros((8 - TOP_K, tr), F32)
    rank_ref[TOP_K:, :] = jnp.zeros((8 - TOP_K, tr), jnp.int32)


def _route(logits_t):
    n = logits_t.shape[1]
    tr = min(512, n)
    tri = jnp.asarray(np.triu(np.ones((tr, tr), np.float32), 1), BF16)
    tok = pl.BlockSpec((8, tr), lambda i: (0, i))
    return pl.pallas_call(
        functools.partial(_route_kernel, tr=tr),
        grid=(n // tr,),
        in_specs=[pl.BlockSpec((N_EXPERTS, tr), lambda i: (0, i)), pl.BlockSpec((tr, tr), lambda i: (0, 0))],
        out_specs=[tok, tok, tok, pl.BlockSpec((N_EXPERTS, 128), lambda i: (0, 0))],
        out_shape=[jax.ShapeDtypeStruct((8, n), jnp.int32), jax.ShapeDtypeStruct((8, n), F32),
                   jax.ShapeDtypeStruct((8, n), jnp.int32), jax.ShapeDtypeStruct((N_EXPERTS, 128), F32)],
        scratch_shapes=[pltpu.VMEM((N_EXPERTS, 128), F32)],
        compiler_params=_cparams(("arbitrary",)),
        name="route",
    )(logits_t, tri)


def _expert_kernel(be_ref, nu_ref, x_ref, w1_ref, b1_ref, w2_ref, b2_ref, o_ref, w1b_ref, w2b_ref):
    i = pl.program_id(0)

    @pl.when(i < nu_ref[0])
    def _():
        @pl.when(jnp.logical_or(i == 0, be_ref[i] != be_ref[jnp.maximum(i - 1, 0)]))
        def _():
            w1b_ref[...] = w1_ref[0].astype(BF16)
            w2b_ref[...] = w2_ref[0].astype(BF16)

        h1 = _dot(_unpack_pairs(x_ref[...]).astype(BF16), w1b_ref[...]) + b1_ref[0]
        glu = jnp.minimum(h1[:, :EXPERT_FF], SWIGLU_LIMIT)
        lin = jnp.clip(h1[:, EXPERT_FF:], -SWIGLU_LIMIT, SWIGLU_LIMIT)
        act = glu * jax.nn.sigmoid(SWIGLU_ALPHA * glu) * (lin + 1.0)
        o_ref[...] = _pack_pairs(_dot(act.astype(BF16), w2b_ref[...]) + b2_ref[0])

    @pl.when(pl.program_id(0) >= nu_ref[0])
    def _():
        o_ref[...] = jnp.zeros_like(o_ref)


def _experts(x_slots, block_expert, n_used, w1, b1, w2, b2, expert_offset):
    n_slots = x_slots.shape[0]
    tm = EXPERT_TILE
    n_blocks = n_slots // tm
    off = expert_offset
    grid_spec = pltpu.PrefetchScalarGridSpec(
        num_scalar_prefetch=2,
        grid=(n_blocks,),
        in_specs=[pl.BlockSpec((tm, D_MODEL // 2), lambda i, be, nu: (i, 0)),
                  pl.BlockSpec((1, D_MODEL, 2 * EXPERT_FF), lambda i, be, nu: (be[i] + off, 0, 0)),
                  pl.BlockSpec((1, 1, 2 * EXPERT_FF), lambda i, be, nu: (be[i], 0, 0)),
                  pl.BlockSpec((1, EXPERT_FF, D_MODEL), lambda i, be, nu: (be[i] + off, 0, 0)),
                  pl.BlockSpec((1, 1, D_MODEL), lambda i, be, nu: (be[i], 0, 0))],
        out_specs=pl.BlockSpec((tm, D_MODEL // 2), lambda i, be, nu: (i, 0)),
        scratch_shapes=[pltpu.VMEM((D_MODEL, 2 * EXPERT_FF), BF16), pltpu.VMEM((EXPERT_FF, D_MODEL), BF16)],
    )
    return pl.pallas_call(
        _expert_kernel,
        grid_spec=grid_spec,
        out_shape=jax.ShapeDtypeStruct((n_slots, D_MODEL // 2), jnp.uint32),
        compiler_params=_cparams(("arbitrary",)),
        name="experts",
    )(block_expert, n_used, x_slots, w1, b1.reshape(N_EXPERTS, 1, -1), w2, b2.reshape(N_EXPERTS, 1, -1))


DISPATCH_UNROLL = 8


def _dispatch_kernel(dest_ref, h_ref, init_ref, xs_ref, sem, *, tm):
    del init_ref

    def issue(r, carry):
        for k in range(TOP_K):
            pltpu.make_async_copy(h_ref.at[pl.ds(r, 1)], xs_ref.at[pl.ds(dest_ref[k, r], 1)], sem).start(
                priority=k % 2)
        return carry

    lax.fori_loop(0, tm, issue, 0, unroll=DISPATCH_UNROLL)
    for k in range(TOP_K):
        pltpu.make_async_copy(h_ref, xs_ref.at[pl.ds(0, tm)], sem).wait()


def _dispatch(h_packed, dest, n_slots):
    n, w = h_packed.shape
    tm = min(512, n)
    return pl.pallas_call(
        functools.partial(_dispatch_kernel, tm=tm),
        grid=(n // tm,),
        in_specs=[pl.BlockSpec((TOP_K, tm), lambda i: (0, i), memory_space=pltpu.SMEM),
                  pl.BlockSpec((tm, w), lambda i: (i, 0)), pl.BlockSpec(memory_space=pl.ANY)],
        out_specs=pl.BlockSpec(memory_space=pl.ANY),
        out_shape=jax.ShapeDtypeStruct((n_slots, w), jnp.uint32),
        scratch_shapes=[pltpu.SemaphoreType.DMA(())],
        input_output_aliases={2: 0},
        compiler_params=_cparams(("arbitrary",)),
        name="dispatch",
    )(dest, h_packed, jnp.zeros((n_slots, w), jnp.uint32))


def _combine_kernel(dcur_ref, dnxt_ref, h_ref, w_ref, lnw_ref, lnb_ref, ys_ref, o_ref, buf_ref, sem, *, tm):
    i = pl.program_id(0)
    n_steps = pl.num_programs(0)

    def issue(d_ref, slot):
        def body(r, carry):
            for k in range(TOP_K):
                pltpu.make_async_copy(ys_ref.at[pl.ds(d_ref[k, r], 1)], buf_ref.at[slot, k, pl.ds(r, 1)],
                                      sem.at[slot]).start(priority=k % 2)
            return carry
        lax.fori_loop(0, tm, body, 0, unroll=DISPATCH_UNROLL)

    @pl.when(i == 0)
    def _():
        issue(dcur_ref, 0)

    @pl.when(i + 1 < n_steps)
    def _():
        issue(dnxt_ref, (i + 1) % 2)

    slot = i % 2
    for k in range(TOP_K):
        pltpu.make_async_copy(ys_ref.at[pl.ds(0, tm)], buf_ref.at[slot, k], sem.at[slot]).wait()
    acc = DEEPNORM_ALPHA * h_ref[...]
    for k in range(TOP_K):
        acc = acc + w_ref[:, k:k + 1] * _unpack_pairs(buf_ref[slot, k])
    o_ref[...] = _layer_norm(acc, lnw_ref[...], lnb_ref[...])


def _combine(h, y_slots, dest, w_tok, ln_w, ln_b):
    n = h.shape[0]
    tm = min(256, n)
    n_steps = n // tm
    lnw = ln_w.reshape(1, -1)
    lnb = ln_b.reshape(1, -1)
    return pl.pallas_call(
        functools.partial(_combine_kernel, tm=tm),
        grid=(n_steps,),
        in_specs=[pl.BlockSpec((TOP_K, tm), lambda i: (0, i), memory_space=pltpu.SMEM),
                  pl.BlockSpec((TOP_K, tm), lambda i: (0, jnp.minimum(i + 1, n_steps - 1)),
                               memory_space=pltpu.SMEM),
                  pl.BlockSpec((tm, D_MODEL), lambda i: (i, 0)),
                  pl.BlockSpec((tm, TOP_K), lambda i: (i, 0)),
                  pl.BlockSpec(lnw.shape, lambda i: (0, 0)), pl.BlockSpec(lnb.shape, lambda i: (0, 0)),
                  pl.BlockSpec(memory_space=pl.ANY)],
        out_specs=pl.BlockSpec((tm, D_MODEL), lambda i: (i, 0)),
        out_shape=jax.ShapeDtypeStruct((n, D_MODEL), F32),
        scratch_shapes=[pltpu.VMEM((2, TOP_K, tm, D_MODEL // 2), jnp.uint32), pltpu.SemaphoreType.DMA((2,))],
        compiler_params=_cparams(("arbitrary",)),
        name="combine",
    )(dest, dest, h, w_tok, lnw, lnb, y_slots)


def _moe(h, h16, logits_t, w1, b1, w2, b2, ln_w, ln_b, expert_offset):
    n = h.shape[0]
    tm = EXPERT_TILE
    idx8, w8, rank8, cnt = _route(logits_t)
    idx, w_top, rank = idx8[:TOP_K], w8[:TOP_K], rank8[:TOP_K]
    counts = cnt[:, 0].astype(jnp.int32)
    padded = ((counts + tm - 1) // tm) * tm
    p_end = jnp.cumsum(padded)
    p_start = p_end - padded
    experts = jnp.arange(N_EXPERTS, dtype=jnp.int32)
    seg_start = jnp.sum(jnp.where(idx[:, :, None] == experts, p_start, 0), axis=-1)
    dest = seg_start + rank
    n_slots = n * TOP_K + N_EXPERTS * tm
    n_blocks = n_slots // tm
    starts = jnp.arange(n_blocks, dtype=jnp.int32) * tm
    block_expert = jnp.minimum(jnp.sum((p_end[None, :] <= starts[:, None]).astype(jnp.int32), axis=1),
                               N_EXPERTS - 1)
    n_used = (p_end[-1] // tm).astype(jnp.int32).reshape(1)
    x_slots = _dispatch(h16, dest, n_slots)
    y_slots = _experts(x_slots, block_expert, n_used, w1, b1, w2, b2, expert_offset)
    return _combine(h, y_slots, dest, w_top.T, ln_w, ln_b)


def _mixer_weight(w_in):
    o = np.cumsum((512, 512, 512, 512, MLA_Q_RANK, MLA_KV_RANK + MLA_ROPE, S5_WIDTH, LRU_WIDTH, LRU_WIDTH))
    hgrn, cq, ckv = w_in[:, :o[3]], w_in[:, o[3]:o[4]], w_in[:, o[4]:o[4] + MLA_KV_RANK]
    kpe = w_in[:, o[4] + MLA_KV_RANK:o[5]]
    su, lx, lg = w_in[:, o[5]:o[6]], w_in[:, o[6]:o[7]], w_in[:, o[7]:o[8]]
    pad = jnp.zeros((D_MODEL, 128 - MLA_ROPE), w_in.dtype)
    w_mix = jnp.concatenate([hgrn, su, lx, lg, cq, ckv, kpe, pad, _rot_half_cols(kpe), pad], axis=1)
    return w_mix, w_in[:, o[8]:]


def _layer(x, positions, lb, bsz, seq, w_in, b_gate, hgrn_norm_w, mla_q_norm_w, mla_wq_b, mla_kv_norm_w, mla_wkv_b,
           s5_a_re, s5_a_im, s5_log_dt, s5_b_re, s5_b_im, s5_c_re, s5_c_im, s5_d, s5_w_glu, s5_b_glu,
           lru_conv_w, lru_conv_b, lru_wa, lru_ba, lru_wx, lru_bx, lru_a_param,
           w_branch, w_out, ln1_w, ln1_b, ln2_w, ln2_b, router_w, router_b, moe_w1, moe_b1, moe_w2, moe_b2,
           expert_offset=0):
    w_mix, w_gl = _mixer_weight(w_in)
    p = _matmul(x, w_mix.astype(BF16), 1024, MIX_WIDTH // 3, F32)
    gl = _matmul(x, w_gl.astype(BF16), 1024, 1024, BF16)
    y_a = _hgrn(p, lb, hgrn_norm_w, bsz, seq)
    qn, qp, kn, kp, vt = _mla_proj(p, positions, mla_q_norm_w, mla_wq_b, mla_kv_norm_w, mla_wkv_b, min(256, seq))
    y_b = _flash(qn, qp, kn, kp, vt, bsz, seq)
    y_s = _s5(p, s5_a_re, s5_a_im, s5_log_dt, s5_b_re, s5_b_im, s5_c_re, s5_c_im, s5_d, bsz, seq)
    y_d = _lru(p, lru_conv_w, lru_conv_b, lru_wa, lru_ba, lru_wx, lru_bx, lru_a_param, bsz, seq)
    h, h16, logits_t = _merge(y_a, y_b, y_s, y_d, gl, x, w_branch, w_out, s5_w_glu, s5_b_glu, b_gate,
                              ln1_w, ln1_b, router_w, router_b)
    return _moe(h, h16, logits_t, moe_w1, moe_b1, moe_w2, moe_b2, ln2_w, ln2_b, expert_offset)


def kernel(x, positions, w_in, b_gate, hgrn_lb_logits, hgrn_norm_w, mla_q_norm_w, mla_wq_b, mla_kv_norm_w, mla_wkv_b, s5_a_re, s5_a_im, s5_log_dt, s5_b_re, s5_b_im, s5_c_re, s5_c_im, s5_d, s5_w_glu, s5_b_glu, lru_conv_w, lru_conv_b, lru_wa, lru_ba, lru_wx, lru_bx, lru_a_param, w_branch, w_out, ln1_w, ln1_b, ln2_w, ln2_b, router_w, router_b, moe_w1, moe_b1, moe_w2, moe_b2):
    bsz, seq, _ = x.shape
    probs = jax.nn.softmax(hgrn_lb_logits.astype(F32), axis=0)
    lower_bounds = jnp.cumsum(probs, axis=0) - probs[0:1]
    per_layer = (w_in, b_gate, hgrn_norm_w, mla_q_norm_w, mla_wq_b, mla_kv_norm_w, mla_wkv_b,
                 s5_a_re, s5_a_im, s5_log_dt, s5_b_re, s5_b_im, s5_c_re, s5_c_im, s5_d, s5_w_glu, s5_b_glu,
                 lru_conv_w, lru_conv_b, lru_wa, lru_ba, lru_wx, lru_bx, lru_a_param,
                 w_branch, w_out, ln1_w, ln1_b, ln2_w, ln2_b, router_w, router_b, moe_w1, moe_b1, moe_w2, moe_b2)
    xf = x.reshape(bsz * seq, D_MODEL)
    w1_all = moe_w1.reshape((DEPTH * N_EXPERTS,) + moe_w1.shape[2:])
    w2_all = moe_w2.reshape((DEPTH * N_EXPERTS,) + moe_w2.shape[2:])
    for l in range(DEPTH):
        args = [a[l] for a in per_layer]
        args[-4], args[-2] = w1_all, w2_all
        xf = _layer(xf, positions, lower_bounds[l], bsz, seq, *args, expert_offset=l * N_EXPERTS)
    return xf.reshape(bsz, seq, D_MODEL)
```

```python
import functools
import math

import numpy as np
import jax
import jax.numpy as jnp
from jax import lax
from jax.experimental import pallas as pl
from jax.experimental.pallas import tpu as pltpu

F32 = jnp.float32
BF16 = jnp.bfloat16

D_MODEL = 1024
DEPTH = 2
CHUNK = 64

HGRN_HEADS = 4
HGRN_DK = 128
HGRN_WIDTH = 512
HGRN_F_MIN = 1e-30
HGRN_CHUNK = 64
HGRN_LEVELS = (1, 2, 4, 8, 16, 32)
HGRN_PAR = 2

MLA_HEADS = 4
MLA_Q_RANK = 256
MLA_KV_RANK = 128
MLA_NOPE = 128
MLA_ROPE = 64
MLA_V = 128
MLA_QK = MLA_NOPE + MLA_ROPE
MLA_VROWS = MLA_V + 16
ROPE_THETA = 10000.0

S5_GROUPS = 32
S5_GROUP_CH = 16
S5_STATE = 64
S5_WIDTH = 512
S5_BLOCK = 16

LRU_WIDTH = 512
LRU_BLOCKS = 8
LRU_BLOCK_W = 64
LRU_CONV = 4
LRU_C = 8.0

N_BRANCH = 4
BRANCH_WIDTH = 512

N_EXPERTS = 32
TOP_K = 4
EXPERT_FF = 1024
SWIGLU_ALPHA = 1.702
SWIGLU_LIMIT = 7.0
EXPERT_TILE = 1024

DEEPNORM_ALPHA = (2.0 * DEPTH) ** 0.25
LN_EPS = 1e-5
RMS_EPS = 1e-6

COL_HQ, COL_HF, COL_HI, COL_HG = 0, 512, 1024, 1536
COL_SU, COL_LX, COL_LG = 2048, 2560, 3072
COL_CQ, COL_CKV, COL_KPE, COL_KPER = 3584, 3840, 3968, 4096
MIX_WIDTH = 4224

VMEM_LIMIT = 56 * 1024 * 1024


def _cparams(sem):
    return pltpu.CompilerParams(dimension_semantics=sem, vmem_limit_bytes=VMEM_LIMIT)


def _dot(a, b):
    return jnp.dot(a, b, preferred_element_type=F32)


def _dot_nt(a, b):
    return lax.dot_general(a, b, (((1,), (1,)), ((), ())), preferred_element_type=F32)


def _dot_tn(a, b):
    return lax.dot_general(a, b, (((0,), (0,)), ((), ())), preferred_element_type=F32)


def _split(x):
    hi = x.astype(BF16)
    lo = (x - hi.astype(F32)).astype(BF16)
    return hi, lo


def _dot3(a, b_hi, b_lo):
    a_hi, a_lo = _split(a)
    return _dot(a_hi, b_hi) + (_dot(a_lo, b_hi) + _dot(a_hi, b_lo))


def _pack_pairs(x):
    w = x.shape[-1] // 2
    lo = lax.bitcast_convert_type(x[:, :w].astype(BF16).astype(F32), jnp.uint32)
    hi = lax.bitcast_convert_type(x[:, w:].astype(BF16).astype(F32), jnp.uint32)
    return lax.shift_right_logical(lo, jnp.uint32(16)) | (hi & jnp.uint32(0xFFFF0000))


def _unpack_pairs(u):
    lo = lax.bitcast_convert_type(lax.shift_left(u, jnp.uint32(16)), F32)
    hi = lax.bitcast_convert_type(u & jnp.uint32(0xFFFF0000), F32)
    return jnp.concatenate([lo, hi], axis=-1)


def _shift_rows(x, k, fill):
    rows = lax.broadcasted_iota(jnp.int32, x.shape, 0)
    return jnp.where(rows >= k, pltpu.roll(x, k, 0), fill)


def _mm_kernel(x_ref, w_ref, o_ref):
    o_ref[...] = _dot(x_ref[...].astype(BF16), w_ref[...]).astype(o_ref.dtype)


def _matmul(x, w, tm, tn, out_dtype):
    n, k = x.shape
    m = w.shape[1]
    tm = min(tm, n)
    return pl.pallas_call(
        _mm_kernel,
        grid=(n // tm, m // tn),
        in_specs=[pl.BlockSpec((tm, k), lambda i, j: (i, 0)),
                  pl.BlockSpec((k, tn), lambda i, j: (0, j))],
        out_specs=pl.BlockSpec((tm, tn), lambda i, j: (i, j)),
        out_shape=jax.ShapeDtypeStruct((n, m), out_dtype),
        compiler_params=_cparams(("parallel", "arbitrary")),
        name="in_proj",
    )(x, w)


def _hgrn_tables():
    c = HGRN_CHUNK
    t = np.arange(c)
    windows = [np.tril(np.ones((c, c), bool)),
               np.triu(np.ones((c, c), bool), 1)]
    masks = [np.eye(c, dtype=bool)]
    for h in HGRN_LEVELS:
        blk = t // h
        odd = blk % 2 == 1
        masks.append(odd[:, None] & (blk[None, :] == blk[:, None] - 1))
        if h > 1:
            windows.append((t[None, :] >= (h * blk)[:, None]) & (t[None, :] <= t[:, None]))
            windows.append((t[None, :] > t[:, None]) & (t[None, :] <= (h * blk + h - 1)[:, None]))
    return np.concatenate(windows, 0).astype(np.float32), np.stack(masks).astype(np.float32)


def _hgrn_kernel(hq_ref, hf_ref, hi_ref, hg_ref, lb_ref, nw_ref, win_ref, mask_ref,
                 o_ref, state_ref, *, n_chunks):
    c = HGRN_CHUNK
    heads = range(HGRN_HEADS)
    hs = [slice(h * HGRN_DK, (h + 1) * HGRN_DK) for h in heads]
    n_lvl = len(HGRN_LEVELS) + 1

    @pl.when(pl.program_id(1) == 0)
    def _():
        state_ref[...] = jnp.zeros_like(state_ref)

    def group_body(gi, carry):
        par = range(HGRN_PAR)
        rows = [pl.ds(pl.multiple_of((gi * HGRN_PAR + cc) * c, c), c) for cc in par]
        lb = lb_ref[...]
        z = [hf_ref[rows[cc], :] for cc in par]
        f = [jnp.maximum(lb + (1.0 - lb) * jax.nn.sigmoid(z[cc]), HGRN_F_MIN) for cc in par]
        k = [(1.0 - lb) * jax.nn.sigmoid(-z[cc]) for cc in par]
        q = [jax.nn.silu(hq_ref[rows[cc], :]) for cc in par]
        v16 = [hi_ref[rows[cc], :].astype(BF16) for cc in par]
        lf = [_split(jnp.log(f[cc])) for cc in par]
        lf_stack = jnp.concatenate([jnp.concatenate([lf[cc][part] for cc in par], axis=-1) for part in range(2)],
                                   axis=0)
        e_all = jnp.exp(_dot(win_ref[...], lf_stack))
        e = [e_all[:, cc * HGRN_WIDTH:(cc + 1) * HGRN_WIDTH] for cc in par]
        blk = lambda cc, w: e[cc][w * c:(w + 1) * c]
        qs, ks, q_in, k_out, decay = [], [], [], [], []
        for cc in par:
            k16 = k[cc].astype(BF16)
            qs.append([q[cc].astype(BF16), (q[cc] * f[cc]).astype(BF16)]
                      + [(q[cc] * blk(cc, 2 * w)).astype(BF16) for w in range(1, n_lvl - 1)])
            ks.append([k16, k16] + [(k[cc] * blk(cc, 2 * w + 1)).astype(BF16) for w in range(1, n_lvl - 1)])
            q_in.append((q[cc] * blk(cc, 0)).astype(BF16))
            k_out.append((k[cc] * blk(cc, 1)).astype(BF16))
            decay.append(e[cc][c - 1:c, :])
        scores = [[None] * HGRN_HEADS for _ in par]
        for lv in range(n_lvl):
            s_l = [[_dot_nt(qs[cc][lv][:, hs[h]], ks[cc][lv][:, hs[h]]) for h in heads] for cc in par]
            for cc in par:
                for h in heads:
                    term = mask_ref[lv] * s_l[cc][h]
                    scores[cc][h] = term if scores[cc][h] is None else scores[cc][h] + term
        intra = [[_dot(scores[cc][h].astype(BF16), v16[cc][:, hs[h]]) for h in heads] for cc in par]
        upd = [[_dot_tn(v16[cc][:, hs[h]], k_out[cc][:, hs[h]]) for h in heads] for cc in par]
        st = [state_ref[h] for h in heads]
        for cc in par:
            o = [intra[cc][h] + _dot_nt(q_in[cc][:, hs[h]], st[h].astype(BF16)) for h in heads]
            st = [st[h] * decay[cc][:, hs[h]] + upd[cc][h] for h in heads]
            o = [o[h] * lax.rsqrt(jnp.mean(o[h] * o[h], axis=-1, keepdims=True) + RMS_EPS) for h in heads]
            o_ref[rows[cc], :] = jnp.concatenate(o, axis=-1) * nw_ref[...] * jax.nn.silu(hg_ref[rows[cc], :])
        for h in heads:
            state_ref[h] = st[h]
        return carry

    lax.fori_loop(0, n_chunks // HGRN_PAR, group_body, 0)


def _hgrn(p, lb, norm_w, bsz, seq):
    t = min(256, seq)
    nt = seq // t
    win, mask = _hgrn_tables()
    col = lambda cb: pl.BlockSpec((t, HGRN_WIDTH), lambda b, i: (b * nt + i, cb))
    const2 = lambda shape: pl.BlockSpec(shape, lambda b, i: (0,) * len(shape))
    return pl.pallas_call(
        functools.partial(_hgrn_kernel, n_chunks=t // HGRN_CHUNK),
        grid=(bsz, nt),
        in_specs=[col(COL_HQ // 512), col(COL_HF // 512), col(COL_HI // 512), col(COL_HG // 512),
                  const2((1, HGRN_WIDTH)), const2((1, HGRN_WIDTH)),
                  const2((win.shape[0], 2 * win.shape[1])), const2(mask.shape)],
        out_specs=pl.BlockSpec((t, HGRN_WIDTH), lambda b, i: (b * nt + i, 0)),
        out_shape=jax.ShapeDtypeStruct((bsz * seq, HGRN_WIDTH), F32),
        scratch_shapes=[pltpu.VMEM((HGRN_HEADS, HGRN_DK, HGRN_DK), F32)],
        compiler_params=_cparams(("parallel", "arbitrary")),
        name="hgrn",
    )(p, p, p, p, lb.reshape(1, -1), norm_w.reshape(1, -1), jnp.asarray(np.concatenate([win, win], 1), BF16),
      jnp.asarray(mask))


def _rms(x, w):
    ms = jnp.mean(x * x, axis=-1, keepdims=True)
    return x * lax.rsqrt(ms + RMS_EPS) * w


def _mla_proj_kernel(cq_ref, ckv_ref, kpe_ref, kper_ref, ang_ref, qnw_ref, kvnw_ref,
                     wqn_ref, wqp_ref, wqpr_ref, wkn_ref, wv_ref,
                     qn_ref, qp_ref, kn_ref, kp_ref, v_ref):
    ang = ang_ref[...]
    cos = jnp.cos(ang)
    sin = jnp.sin(ang)
    qn = _rms(cq_ref[...], qnw_ref[...]).astype(BF16)
    cos4 = jnp.concatenate([cos] * MLA_HEADS, axis=-1)
    sin4 = jnp.concatenate([sin] * MLA_HEADS, axis=-1)
    qn_ref[...] = _dot(qn, wqn_ref[...]).astype(BF16)
    qp_ref[...] = (_dot(qn, wqp_ref[...]) * cos4 + _dot(qn, wqpr_ref[...]) * sin4).astype(BF16)
    cn = _rms(ckv_ref[...], kvnw_ref[...]).astype(BF16)
    kn_ref[...] = _dot(cn, wkn_ref[...]).astype(BF16)
    vt = _dot_nt(wv_ref[...], cn)
    ones = jnp.ones((MLA_VROWS - MLA_V, vt.shape[1]), F32)
    v_ref[0] = jnp.concatenate(
        [piece for h in range(MLA_HEADS) for piece in (vt[h * MLA_V:(h + 1) * MLA_V], ones)], axis=0).astype(BF16)
    kp_ref[...] = (kpe_ref[...] * cos + kper_ref[...] * sin).astype(BF16)


def _rot_half_cols(w):
    half = w.shape[-1] // 2
    return jnp.concatenate([-w[..., half:], w[..., :half]], axis=-1)


def _mla_proj(p, positions, q_norm_w, wq_b, kv_norm_w, wkv_b, tm):
    n = p.shape[0]
    scale = MLA_QK ** -0.5 * math.log2(math.e)
    wq = wq_b.reshape(MLA_Q_RANK, MLA_HEADS, MLA_QK) * scale
    w_nope = wq[:, :, :MLA_NOPE].reshape(MLA_Q_RANK, MLA_HEADS * MLA_NOPE)
    w_pe = wq[:, :, MLA_NOPE:]
    pad = jnp.zeros((MLA_Q_RANK, MLA_HEADS, 128 - MLA_ROPE), F32)
    w_pe_p = jnp.concatenate([w_pe, pad], axis=-1).reshape(MLA_Q_RANK, MLA_HEADS * 128)
    w_per_p = jnp.concatenate([_rot_half_cols(w_pe), pad], axis=-1).reshape(MLA_Q_RANK, MLA_HEADS * 128)
    wkv = wkv_b.reshape(MLA_KV_RANK, MLA_HEADS, MLA_NOPE + MLA_V)
    w_kn = wkv[:, :, :MLA_NOPE].reshape(MLA_KV_RANK, MLA_HEADS * MLA_NOPE)
    w_v = wkv[:, :, MLA_NOPE:].reshape(MLA_KV_RANK, MLA_HEADS * MLA_V)
    inv_freq = ROPE_THETA ** (-jnp.arange(0, MLA_ROPE, 2, dtype=F32) / MLA_ROPE)
    freq_lane = jnp.concatenate([inv_freq, inv_freq, jnp.zeros((128 - MLA_ROPE,), F32)])
    ang = positions.reshape(n, 1).astype(F32) * freq_lane[None, :]

    row = lambda w, cb: pl.BlockSpec((tm, w), lambda i: (i, cb))
    full = lambda a: pl.BlockSpec(a.shape, lambda i: (0,) * a.ndim)
    weights = [w_nope.astype(BF16), w_pe_p.astype(BF16), w_per_p.astype(BF16), w_kn.astype(BF16),
               w_v.T.astype(BF16)]
    qnw = q_norm_w.reshape(1, -1)
    kvnw = kv_norm_w.reshape(1, -1)
    outs = pl.pallas_call(
        _mla_proj_kernel,
        grid=(n // tm,),
        in_specs=[row(256, COL_CQ // 256), row(128, COL_CKV // 128), row(128, COL_KPE // 128),
                  row(128, COL_KPER // 128), row(128, 0), full(qnw), full(kvnw)] + [full(w) for w in weights],
        out_specs=[row(512, 0), row(512, 0), row(512, 0), row(128, 0),
                   pl.BlockSpec((1, MLA_HEADS * MLA_VROWS, tm), lambda i: (i, 0, 0))],
        out_shape=[jax.ShapeDtypeStruct((n, 512), BF16), jax.ShapeDtypeStruct((n, 512), BF16),
                   jax.ShapeDtypeStruct((n, 512), BF16), jax.ShapeDtypeStruct((n, 128), BF16),
                   jax.ShapeDtypeStruct((n // tm, MLA_HEADS * MLA_VROWS, tm), BF16)],
        compiler_params=_cparams(("parallel",)),
        name="mla_proj",
    )(p, p, p, p, ang, qnw, kvnw, *weights)
    return outs


def _flash_kernel(qn_ref, qp_ref, kn_ref, kp_ref, vt_ref, o_ref, acc_ref, *, tq):
    i = pl.program_id(1)
    heads = range(MLA_HEADS)
    hs = [slice(h * 128, (h + 1) * 128) for h in heads]
    vs = [slice(h * MLA_VROWS, (h + 1) * MLA_VROWS) for h in heads]
    acc_ref[...] = jnp.zeros_like(acc_ref)

    def scores(j):
        r0 = pl.multiple_of(j * tq, tq)
        kp = kp_ref[pl.ds(r0, tq), :]
        return tuple(_dot_nt(jnp.concatenate([kn_ref[pl.ds(r0, tq), hs[h]], kp], axis=-1),
                             jnp.concatenate([qn_ref[:, hs[h]], qp_ref[:, hs[h]]], axis=-1)) for h in heads)

    def consume(j, s, m, mask):
        p, m_new, alpha = [], [], []
        for h in heads:
            sh = s[h] if mask is None else jnp.where(mask, s[h], -1e30)
            mn = jnp.maximum(m[h], jnp.max(sh, axis=0, keepdims=True))
            alpha.append(jnp.exp2(m[h] - mn))
            p.append(jnp.exp2(sh - mn).astype(BF16))
            m_new.append(mn)
        vt = vt_ref[j]
        pv = [_dot(vt[vs[h], :], p[h]) for h in heads]
        for h in heads:
            acc_ref[h] = acc_ref[h] * alpha[h] + pv[h]
        return tuple(m_new)

    def body(j, carry):
        s, m = carry
        s_next = scores(j + 1)
        return s_next, consume(j, s, m, None)

    m0 = tuple(jnp.full((1, tq), -1e30, F32) for _ in heads)
    s, m = lax.fori_loop(0, i, body, (scores(0), m0))
    shift = CHUNK.bit_length() - 1
    kc = lax.shift_right_logical(lax.broadcasted_iota(jnp.int32, (tq, tq), 0), shift)
    qc = lax.shift_right_logical(lax.broadcasted_iota(jnp.int32, (tq, tq), 1), shift)
    consume(i, s, m, kc <= qc)
    for h in heads:
        acc = acc_ref[h]
        o_ref[:, hs[h]] = (acc[:MLA_V] / acc[MLA_V:MLA_V + 1]).T.astype(o_ref.dtype)


def _flash(qn, qp, kn, kp, vt, bsz, seq):
    tq = vt.shape[2]
    nq = seq // tq
    width = MLA_HEADS * 128
    qspec = pl.BlockSpec((tq, width), lambda b, i: (b * nq + i, 0))
    kspec = pl.BlockSpec((seq, width), lambda b, i: (b, 0))
    kpspec = pl.BlockSpec((seq, 128), lambda b, i: (b, 0))
    vspec = pl.BlockSpec((nq, MLA_HEADS * MLA_VROWS, tq), lambda b, i: (b, 0, 0))
    return pl.pallas_call(
        functools.partial(_flash_kernel, tq=tq),
        grid=(bsz, nq),
        in_specs=[qspec, qspec, kspec, kpspec, vspec],
        out_specs=qspec,
        out_shape=jax.ShapeDtypeStruct((bsz * seq, width), BF16),
        scratch_shapes=[pltpu.VMEM((MLA_HEADS, MLA_VROWS, tq), F32)],
        compiler_params=_cparams(("parallel", "arbitrary")),
        name="flash",
    )(qn, qp, kn, kp, vt)


def _cmul(a, b):
    return a[0] * b[0] - a[1] * b[1], a[0] * b[1] + a[1] * b[0]


def _s5_tables(a_re, a_im, log_dt, b_re, b_im, c_re, c_im, d_skip, steps):
    lb = S5_BLOCK
    f32 = lambda v: v.astype(F32)
    a_re, a_im, b_re, b_im, c_re, c_im = map(f32, (a_re, a_im, b_re, b_im, c_re, c_im))
    dt = jnp.exp(f32(log_dt))[:, None]
    ld = (a_re * dt, a_im * dt)

    def power(tau):
        mag = jnp.exp(ld[0] * tau)
        return mag * jnp.cos(ld[1] * tau), mag * jnp.sin(ld[1] * tau)

    lam_bar = power(1.0)
    inv = a_re * a_re + a_im * a_im
    ratio = _cmul((lam_bar[0] - 1.0, lam_bar[1]), (a_re / inv, -a_im / inv))
    b_bar = _cmul((ratio[0][..., None], ratio[1][..., None]), (b_re, b_im))
    tau = jnp.arange(lb + 1, dtype=F32)[:, None, None]
    pw = power(tau)
    cp = _cmul((c_re[None], c_im[None]), (pw[0][:, :, None, :], pw[1][:, :, None, :]))
    kern = jnp.einsum('tgcn,gnd->tgcd', cp[0][:lb], b_bar[0]) - jnp.einsum('tgcn,gnd->tgcd', cp[1][:lb], b_bar[1])
    s_idx = np.arange(lb)[:, None]
    t_idx = np.arange(lb)[None, :]
    lag = np.clip(t_idx - s_idx, 0, lb - 1)
    causal = jnp.asarray((t_idx >= s_idx).astype(np.float32))
    toe = kern[lag] * causal[:, :, None, None, None]
    skip = jnp.asarray(np.eye(lb, dtype=np.float32))[:, :, None, None, None] * (
        d_skip.astype(F32)[None, None, :, :, None] * jnp.eye(S5_GROUP_CH, dtype=F32)[None, None, None])
    toe = (toe + skip).transpose(2, 1, 3, 0, 4).reshape(S5_GROUPS, lb * S5_GROUP_CH, lb * S5_GROUP_CH)
    emit = jnp.concatenate([cp[0][1:], -cp[1][1:]], axis=-1)
    emit = emit.transpose(1, 0, 2, 3).reshape(S5_GROUPS, lb * S5_GROUP_CH, 2 * S5_STATE)
    rev = (pw[0][:lb][::-1][..., None], pw[1][:lb][::-1][..., None])
    bp = _cmul(rev, (b_bar[0][None], b_bar[1][None]))
    fold = jnp.concatenate([bp[0], bp[1]], axis=2)
    fold = fold.transpose(1, 2, 0, 3).reshape(S5_GROUPS, 2 * S5_STATE, lb * S5_GROUP_CH)
    jump = (lb * 2.0 ** jnp.arange(steps, dtype=F32))[:, None, None]
    pj = power(jump)
    pw_re = jnp.concatenate([pj[0], pj[0]], axis=-1).transpose(1, 2, 0)
    pw_im = jnp.concatenate([-pj[1], pj[1]], axis=-1).transpose(1, 2, 0)
    return toe, emit, fold, pw_re, pw_im


def _s5_kernel(su0_ref, su1_ref, su2_ref, su3_ref, toe_ref, emit_ref, fold_ref, pre_ref, pim_ref, y_ref,
               ut_ref, carry_ref, *, r, steps):
    lb = S5_BLOCK
    gc = S5_GROUP_CH

    @pl.when(pl.program_id(1) == 0)
    def _():
        carry_ref[...] = jnp.zeros_like(carry_ref)

    for s in range(lb):
        for c, su_ref in enumerate((su0_ref, su1_ref, su2_ref, su3_ref)):
            ut_ref[s, c * 128:(c + 1) * 128, :] = su_ref[pl.ds(s, r, stride=lb), :].T
    lane = lax.broadcasted_iota(jnp.int32, (2 * S5_STATE, r), 1)
    swap = lambda v: jnp.concatenate([v[S5_STATE:], v[:S5_STATE]], axis=0)

    def group(g, carry):
        g0 = pl.multiple_of(g * gc, gc)
        ug = jnp.concatenate([ut_ref[s, pl.ds(g0, gc), :] for s in range(lb)], axis=0).astype(BF16)
        pre = pre_ref[g]
        pim = pim_ref[g]
        cmul = lambda j, v: pre[:, j:j + 1] * v + pim[:, j:j + 1] * swap(v)
        car = carry_ref[g]
        x = _dot(fold_ref[g], ug)
        x = x + jnp.where(lane == 0, cmul(0, car), 0.0)
        for j in range(steps):
            xs = jnp.where(lane >= 2 ** j, pltpu.roll(x, 2 ** j, 1), 0.0)
            x = x + cmul(j, xs)
        x_prev = jnp.where(lane == 0, car, pltpu.roll(x, 1, 1))
        carry_ref[g] = jnp.broadcast_to(x[:, r - 1:r], x.shape)
        y = _dot(toe_ref[g], ug) + _dot(emit_ref[g], x_prev.astype(BF16))
        for t in range(lb):
            ut_ref[t, pl.ds(g0, gc), :] = y[t * gc:(t + 1) * gc]
        return carry

    lax.fori_loop(0, S5_GROUPS, group, 0)
    for t in range(lb):
        for c in range(S5_WIDTH // 128):
            y_ref[c, pl.ds(t, r, stride=lb), :] = ut_ref[t, c * 128:(c + 1) * 128, :].T


def _s5(p, a_re, a_im, log_dt, b_re, b_im, c_re, c_im, d_skip, bsz, seq):
    lb = S5_BLOCK
    r = min(128, seq // lb)
    nt = seq // (lb * r)
    ncb = S5_WIDTH // 128
    steps = max(1, int(math.ceil(math.log2(r))))
    toe, emit, fold, pw_re, pw_im = _s5_tables(a_re, a_im, log_dt, b_re, b_im, c_re, c_im, d_skip, steps)
    consts = [toe.astype(BF16), emit.astype(BF16), fold.astype(BF16), pw_re, pw_im]
    full = lambda a: pl.BlockSpec(a.shape, lambda b, t: (0,) * a.ndim)
    return pl.pallas_call(
        functools.partial(_s5_kernel, r=r, steps=steps),
        grid=(bsz, nt),
        in_specs=[pl.BlockSpec((lb * r, 128), functools.partial(lambda b, t, c: (b * nt + t, COL_SU // 128 + c), c=c))
                  for c in range(ncb)] + [full(c) for c in consts],
        out_specs=pl.BlockSpec((ncb, lb * r, 128), lambda b, t: (0, b * nt + t, 0)),
        out_shape=jax.ShapeDtypeStruct((ncb, bsz * seq, 128), F32),
        scratch_shapes=[pltpu.VMEM((lb, S5_WIDTH, r), F32), pltpu.VMEM((S5_GROUPS, 2 * S5_STATE, r), F32)],
        compiler_params=_cparams(("parallel", "arbitrary")),
        name="s5",
    )(*([p] * ncb), *consts)


def _lru_kernel(lx_ref, lg_ref, cw_ref, cb_ref, wg_ref, bg_ref, sp_ref, o_ref, tail_ref, h_ref, *, ts):
    @pl.when(pl.program_id(1) == 0)
    def _():
        tail_ref[...] = jnp.zeros_like(tail_ref)
        h_ref[...] = jnp.zeros_like(h_ref)

    x = lx_ref[...]
    xe = jnp.concatenate([tail_ref[...], x], axis=0)
    xc = cb_ref[...] + cw_ref[LRU_CONV - 1:LRU_CONV, :] * x
    for j in range(1, LRU_CONV):
        xc = xc + cw_ref[LRU_CONV - 1 - j:LRU_CONV - j, :] * pltpu.roll(xe, j, 0)[8:, :]
    tail_ref[...] = x[ts - 8:, :]
    gates = jax.nn.sigmoid(_dot3(xc, wg_ref[0], wg_ref[1]) + bg_ref[...])
    r = gates[:, :LRU_WIDTH]
    ig = gates[:, LRU_WIDTH:]
    log_a = -LRU_C * r * sp_ref[...]
    a = jnp.exp(log_a)
    b = jnp.sqrt(jnp.maximum(1.0 - jnp.exp(2.0 * log_a), 0.0)) * (ig * xc)
    k = 1
    while k < ts:
        b = b + a * _shift_rows(b, k, 0.0)
        a = a * _shift_rows(a, k, 1.0)
        k *= 2
    h = b + a * h_ref[0:1, :]
    h_ref[0:1, :] = h[ts - 1:ts, :]
    o_ref[...] = h * jax.nn.gelu(lg_ref[...])


def _block_diag(w):
    nb, bw, _ = w.shape
    eye = jnp.eye(nb, dtype=w.dtype)
    return (eye[:, None, :, None] * w[:, :, None, :]).reshape(nb * bw, nb * bw)


def _lru(p, conv_w, conv_b, wa, ba, wx, bx, a_param, bsz, seq):
    ts = min(512, seq)
    nt = seq // ts
    wg = jnp.concatenate([_block_diag(wa), _block_diag(wx)], axis=1)
    wg_hi = wg.astype(BF16)
    wg2 = jnp.stack([wg_hi, (wg - wg_hi.astype(F32)).astype(BF16)])
    bg = jnp.concatenate([ba, bx]).reshape(1, -1)
    sp = jax.nn.softplus(a_param.astype(F32)).reshape(1, -1)
    col = lambda cb: pl.BlockSpec((ts, LRU_WIDTH), lambda b, t: (b * nt + t, cb))
    full = lambda a: pl.BlockSpec(a.shape, lambda b, t: (0,) * a.ndim)
    cb2 = conv_b.reshape(1, -1)
    return pl.pallas_call(
        functools.partial(_lru_kernel, ts=ts),
        grid=(bsz, nt),
        in_specs=[col(COL_LX // 512), col(COL_LG // 512), full(conv_w), full(cb2), full(wg2), full(bg), full(sp)],
        out_specs=pl.BlockSpec((ts, LRU_WIDTH), lambda b, t: (b * nt + t, 0)),
        out_shape=jax.ShapeDtypeStruct((bsz * seq, LRU_WIDTH), F32),
        scratch_shapes=[pltpu.VMEM((8, LRU_WIDTH), F32), pltpu.VMEM((8, LRU_WIDTH), F32)],
        compiler_params=_cparams(("parallel", "arbitrary")),
        name="rglru",
    )(p, p, conv_w, cb2, wg2, bg, sp)


def _layer_norm(x, w, b):
    xc = x - jnp.mean(x, axis=-1, keepdims=True)
    var = jnp.mean(xc * xc, axis=-1, keepdims=True)
    return xc * lax.rsqrt(var + LN_EPS) * w + b


def _merge_kernel(ya_ref, yb_ref, ys_ref, yd_ref, gl_ref, x_ref, wbr_ref, wout_ref, wglu_ref, bglu_ref,
                  bgate_ref, lnw_ref, lnb_ref, rw_ref, rb_ref, h_ref, hb_ref, lg_ref):
    yc = jax.nn.gelu(jnp.concatenate([ys_ref[c] for c in range(S5_WIDTH // 128)], axis=-1))
    yc = yc * jax.nn.sigmoid(_dot(yc.astype(BF16), wglu_ref[...]) + bglu_ref[...])
    branches = (ya_ref[...].astype(BF16), yb_ref[...], yc.astype(BF16), yd_ref[...].astype(BF16))
    mix = None
    for bi, yb in enumerate(branches):
        cs = slice(bi * D_MODEL, (bi + 1) * D_MODEL)
        gate = jax.nn.sigmoid(gl_ref[:, cs].astype(F32) + bgate_ref[:, cs])
        term = gate * _dot(yb, wbr_ref[bi])
        mix = term if mix is None else mix + term
    mo = _dot(mix.astype(BF16), wout_ref[...])
    h = _layer_norm(DEEPNORM_ALPHA * x_ref[...] + mo, lnw_ref[...], lnb_ref[...])
    h_ref[...] = h
    h_hi, h_lo = _split(h)
    hb_ref[...] = _pack_pairs(h)
    hcat = jnp.concatenate([h_hi, h_lo, h_hi], axis=-1)
    lg_ref[...] = _dot_nt(rw_ref[...], hcat) + rb_ref[...]


def _merge(ya, yb, ys, yd, gl, x, w_branch, w_out, w_glu, b_glu, b_gate, ln_w, ln_b, router_w, router_b):
    n = x.shape[0]
    tm = min(512, n)
    rwt = router_w.T.astype(F32)
    rw_hi = rwt.astype(BF16)
    rw_lo = (rwt - rw_hi.astype(F32)).astype(BF16)
    rw3 = jnp.concatenate([rw_hi, rw_hi, rw_lo], axis=1)
    row = lambda w: pl.BlockSpec((tm, w), lambda i: (i, 0))
    full = lambda a: pl.BlockSpec(a.shape, lambda i: (0,) * a.ndim)
    consts = [w_branch.astype(BF16), w_out.astype(BF16), w_glu.astype(BF16), b_glu.reshape(1, -1),
              b_gate.reshape(1, -1), ln_w.reshape(1, -1), ln_b.reshape(1, -1), rw3, router_b.reshape(-1, 1)]
    return pl.pallas_call(
        _merge_kernel,
        grid=(n // tm,),
        in_specs=[row(512), row(512), pl.BlockSpec((S5_WIDTH // 128, tm, 128), lambda i: (0, i, 0)), row(512),
                  row(N_BRANCH * D_MODEL), row(D_MODEL)]
                 + [full(c) for c in consts],
        out_specs=[row(D_MODEL), row(D_MODEL // 2), pl.BlockSpec((N_EXPERTS, tm), lambda i: (0, i))],
        out_shape=[jax.ShapeDtypeStruct((n, D_MODEL), F32), jax.ShapeDtypeStruct((n, D_MODEL // 2), jnp.uint32),
                   jax.ShapeDtypeStruct((N_EXPERTS, n), F32)],
        compiler_params=_cparams(("parallel",)),
        name="merge",
    )(ya, yb, ys, yd, gl, x, *consts)


def _route_kernel(lg_ref, tri_ref, idx_ref, w_ref, rank_ref, cnt_ref, carry_ref, *, tr):
    @pl.when(pl.program_id(0) == 0)
    def _():
        carry_ref[...] = jnp.zeros_like(carry_ref)

    l = lg_ref[...]
    eidx = lax.broadcasted_iota(jnp.int32, l.shape, 0)
    vals, hots = [], []
    for _ in range(TOP_K):
        m = jnp.max(l, axis=0, keepdims=True)
        first = jnp.min(jnp.where(l == m, eidx, N_EXPERTS), axis=0, keepdims=True)
        hot = eidx == first
        l = jnp.where(hot, -jnp.inf, l)
        vals.append(m)
        hots.append(hot)
        idx_ref[len(vals) - 1:len(vals), :] = first
    ex = [jnp.exp(v - vals[0]) for v in vals]
    den = ex[0] + ex[1] + ex[2] + ex[3]
    for k in range(TOP_K):
        w_ref[k:k + 1, :] = ex[k] / den
    member = jnp.zeros(l.shape, F32)
    for hot in hots:
        member = member + hot.astype(F32)
    before = _dot(member.astype(BF16), tri_ref[...]) + carry_ref[:, 0:1]
    for k in range(TOP_K):
        rank = jnp.sum(jnp.where(hots[k], before, 0.0), axis=0, keepdims=True)
        rank_ref[k:k + 1, :] = rank.astype(jnp.int32)
    carry_ref[...] = carry_ref[...] + jnp.sum(member, axis=1, keepdims=True)
    cnt_ref[...] = carry_ref[...]
    idx_ref[TOP_K:, :] = jnp.zeros((8 - TOP_K, tr), jnp.int32)
    w_ref[TOP_K:, :] = jnp.zeros((8 - TOP_K, tr), F32)
    rank_ref[TOP_K:, :] = jnp.zeros((8 - TOP_K, tr), jnp.int32)


def _route(logits_t):
    n = logits_t.shape[1]
    tr = min(512, n)
    tri = jnp.asarray(np.triu(np.ones((tr, tr), np.float32), 1), BF16)
    tok = pl.BlockSpec((8, tr), lambda i: (0, i))
    return pl.pallas_call(
        functools.partial(_route_kernel, tr=tr),
        grid=(n // tr,),
        in_specs=[pl.BlockSpec((N_EXPERTS, tr), lambda i: (0, i)), pl.BlockSpec((tr, tr), lambda i: (0, 0))],
        out_specs=[tok, tok, tok, pl.BlockSpec((N_EXPERTS, 128), lambda i: (0, 0))],
        out_shape=[jax.ShapeDtypeStruct((8, n), jnp.int32), jax.ShapeDtypeStruct((8, n), F32),
                   jax.ShapeDtypeStruct((8, n), jnp.int32), jax.ShapeDtypeStruct((N_EXPERTS, 128), F32)],
        scratch_shapes=[pltpu.VMEM((N_EXPERTS, 128), F32)],
        compiler_params=_cparams(("arbitrary",)),
        name="route",
    )(logits_t, tri)


def _expert_kernel(be_ref, nu_ref, x_ref, w1_ref, b1_ref, w2_ref, b2_ref, o_ref, w1b_ref, w2b_ref):
    i = pl.program_id(0)

    @pl.when(i < nu_ref[0])
    def _():
        @pl.when(jnp.logical_or(i == 0, be_ref[i] != be_ref[jnp.maximum(i - 1, 0)]))
        def _():
            w1b_ref[...] = w1_ref[0].astype(BF16)
            w2b_ref[...] = w2_ref[0].astype(BF16)

        h1 = _dot(_unpack_pairs(x_ref[...]).astype(BF16), w1b_ref[...]) + b1_ref[0]
        glu = jnp.minimum(h1[:, :EXPERT_FF], SWIGLU_LIMIT)
        lin = jnp.clip(h1[:, EXPERT_FF:], -SWIGLU_LIMIT, SWIGLU_LIMIT)
        act = glu * jax.nn.sigmoid(SWIGLU_ALPHA * glu) * (lin + 1.0)
        o_ref[...] = _pack_pairs(_dot(act.astype(BF16), w2b_ref[...]) + b2_ref[0])

    @pl.when(pl.program_id(0) >= nu_ref[0])
    def _():
        o_ref[...] = jnp.zeros_like(o_ref)


def _experts(x_slots, block_expert, n_used, w1, b1, w2, b2, expert_offset):
    n_slots = x_slots.shape[0]
    tm = EXPERT_TILE
    n_blocks = n_slots // tm
    off = expert_offset
    grid_spec = pltpu.PrefetchScalarGridSpec(
        num_scalar_prefetch=2,
        grid=(n_blocks,),
        in_specs=[pl.BlockSpec((tm, D_MODEL // 2), lambda i, be, nu: (i, 0)),
                  pl.BlockSpec((1, D_MODEL, 2 * EXPERT_FF), lambda i, be, nu: (be[i] + off, 0, 0)),
                  pl.BlockSpec((1, 1, 2 * EXPERT_FF), lambda i, be, nu: (be[i], 0, 0)),
                  pl.BlockSpec((1, EXPERT_FF, D_MODEL), lambda i, be, nu: (be[i] + off, 0, 0)),
                  pl.BlockSpec((1, 1, D_MODEL), lambda i, be, nu: (be[i], 0, 0))],
        out_specs=pl.BlockSpec((tm, D_MODEL // 2), lambda i, be, nu: (i, 0)),
        scratch_shapes=[pltpu.VMEM((D_MODEL, 2 * EXPERT_FF), BF16), pltpu.VMEM((EXPERT_FF, D_MODEL), BF16)],
    )
    return pl.pallas_call(
        _expert_kernel,
        grid_spec=grid_spec,
        out_shape=jax.ShapeDtypeStruct((n_slots, D_MODEL // 2), jnp.uint32),
        compiler_params=_cparams(("arbitrary",)),
        name="experts",
    )(block_expert, n_used, x_slots, w1, b1.reshape(N_EXPERTS, 1, -1), w2, b2.reshape(N_EXPERTS, 1, -1))


COMBINE_TILE = 128


def _dispatch_kernel(dest_ref, h_ref, init_ref, xs_ref, sem, *, tm):
    del init_ref

    for r in range(tm):
        for k in range(TOP_K):
            pltpu.make_async_copy(h_ref.at[pl.ds(r, 1)], xs_ref.at[pl.ds(dest_ref[k, r], 1)], sem).start(
                priority=k % 2)
    for k in range(TOP_K):
        pltpu.make_async_copy(h_ref, xs_ref.at[pl.ds(0, tm)], sem).wait()


def _dispatch(h_packed, dest, n_slots):
    n, w = h_packed.shape
    tm = min(COMBINE_TILE, n)
    return pl.pallas_call(
        functools.partial(_dispatch_kernel, tm=tm),
        grid=(n // tm,),
        in_specs=[pl.BlockSpec((TOP_K, tm), lambda i: (0, i), memory_space=pltpu.SMEM),
                  pl.BlockSpec((tm, w), lambda i: (i, 0)), pl.BlockSpec(memory_space=pl.ANY)],
        out_specs=pl.BlockSpec(memory_space=pl.ANY),
        out_shape=jax.ShapeDtypeStruct((n_slots, w), jnp.uint32),
        scratch_shapes=[pltpu.SemaphoreType.DMA(())],
        input_output_aliases={2: 0},
        compiler_params=_cparams(("arbitrary",)),
        name="dispatch",
    )(dest, h_packed, jnp.zeros((n_slots, w), jnp.uint32))


def _combine_kernel(d_ref, h_ref, w_ref, lnw_ref, lnb_ref, ys_ref, o_ref, buf_ref, sem, *, tm):
    i = pl.program_id(0)
    n_tiles = pl.num_programs(0) - 1

    def issue(slot):
        for r in range(tm):
            for k in range(TOP_K):
                pltpu.make_async_copy(ys_ref.at[pl.ds(d_ref[k, r], 1)], buf_ref.at[slot, k, pl.ds(r, 1)],
                                      sem.at[slot]).start(priority=k % 2)

    def finish(slot):
        for k in range(TOP_K):
            pltpu.make_async_copy(ys_ref.at[pl.ds(0, tm)], buf_ref.at[slot, k], sem.at[slot]).wait()
        acc = DEEPNORM_ALPHA * h_ref[...]
        for k in range(TOP_K):
            acc = acc + w_ref[:, k:k + 1] * _unpack_pairs(buf_ref[slot, k])
        o_ref[...] = _layer_norm(acc, lnw_ref[...], lnb_ref[...])

    for parity in range(2):
        @pl.when(jnp.logical_and(i < n_tiles, i % 2 == parity))
        def _():
            issue(parity)

        @pl.when(jnp.logical_and(i > 0, i % 2 == parity))
        def _():
            finish(1 - parity)


def _combine(h, y_slots, dest, w_tok, ln_w, ln_b):
    n = h.shape[0]
    tm = min(COMBINE_TILE, n)
    n_tiles = n // tm
    lnw = ln_w.reshape(1, -1)
    lnb = ln_b.reshape(1, -1)
    prev = lambda i: jnp.maximum(i - 1, 0)
    return pl.pallas_call(
        functools.partial(_combine_kernel, tm=tm),
        grid=(n_tiles + 1,),
        in_specs=[pl.BlockSpec((TOP_K, tm), lambda i: (0, jnp.minimum(i, n_tiles - 1)), memory_space=pltpu.SMEM),
                  pl.BlockSpec((tm, D_MODEL), lambda i: (prev(i), 0)),
                  pl.BlockSpec((tm, TOP_K), lambda i: (prev(i), 0)),
                  pl.BlockSpec(lnw.shape, lambda i: (0, 0)), pl.BlockSpec(lnb.shape, lambda i: (0, 0)),
                  pl.BlockSpec(memory_space=pl.ANY)],
        out_specs=pl.BlockSpec((tm, D_MODEL), lambda i: (prev(i), 0)),
        out_shape=jax.ShapeDtypeStruct((n, D_MODEL), F32),
        scratch_shapes=[pltpu.VMEM((2, TOP_K, tm, D_MODEL // 2), jnp.uint32), pltpu.SemaphoreType.DMA((2,))],
        compiler_params=_cparams(("arbitrary",)),
        name="combine",
    )(dest, h, w_tok, lnw, lnb, y_slots)


def _moe(h, h16, logits_t, w1, b1, w2, b2, ln_w, ln_b, expert_offset):
    n = h.shape[0]
    tm = EXPERT_TILE
    idx8, w8, rank8, cnt = _route(logits_t)
    idx, w_top, rank = idx8[:TOP_K], w8[:TOP_K], rank8[:TOP_K]
    counts = cnt[:, 0].astype(jnp.int32)
    padded = ((counts + tm - 1) // tm) * tm
    p_end = jnp.cumsum(padded)
    p_start = p_end - padded
    experts = jnp.arange(N_EXPERTS, dtype=jnp.int32)
    seg_start = jnp.sum(jnp.where(idx[:, :, None] == experts, p_start, 0), axis=-1)
    dest = seg_start + rank
    n_slots = n * TOP_K + N_EXPERTS * tm
    n_blocks = n_slots // tm
    starts = jnp.arange(n_blocks, dtype=jnp.int32) * tm
    block_expert = jnp.minimum(jnp.sum((p_end[None, :] <= starts[:, None]).astype(jnp.int32), axis=1),
                               N_EXPERTS - 1)
    n_used = (p_end[-1] // tm).astype(jnp.int32).reshape(1)
    x_slots = _dispatch(h16, dest, n_slots)
    y_slots = _experts(x_slots, block_expert, n_used, w1, b1, w2, b2, expert_offset)
    return _combine(h, y_slots, dest, w_top.T, ln_w, ln_b)


def _mixer_weight(w_in):
    o = np.cumsum((512, 512, 512, 512, MLA_Q_RANK, MLA_KV_RANK + MLA_ROPE, S5_WIDTH, LRU_WIDTH, LRU_WIDTH))
    hgrn, cq, ckv = w_in[:, :o[3]], w_in[:, o[3]:o[4]], w_in[:, o[4]:o[4] + MLA_KV_RANK]
    kpe = w_in[:, o[4] + MLA_KV_RANK:o[5]]
    su, lx, lg = w_in[:, o[5]:o[6]], w_in[:, o[6]:o[7]], w_in[:, o[7]:o[8]]
    pad = jnp.zeros((D_MODEL, 128 - MLA_ROPE), w_in.dtype)
    w_mix = jnp.concatenate([hgrn, su, lx, lg, cq, ckv, kpe, pad, _rot_half_cols(kpe), pad], axis=1)
    return w_mix, w_in[:, o[8]:]


def _layer(x, positions, lb, bsz, seq, w_in, b_gate, hgrn_norm_w, mla_q_norm_w, mla_wq_b, mla_kv_norm_w, mla_wkv_b,
           s5_a_re, s5_a_im, s5_log_dt, s5_b_re, s5_b_im, s5_c_re, s5_c_im, s5_d, s5_w_glu, s5_b_glu,
           lru_conv_w, lru_conv_b, lru_wa, lru_ba, lru_wx, lru_bx, lru_a_param,
           w_branch, w_out, ln1_w, ln1_b, ln2_w, ln2_b, router_w, router_b, moe_w1, moe_b1, moe_w2, moe_b2,
           expert_offset=0):
    w_mix, w_gl = _mixer_weight(w_in)
    p = _matmul(x, w_mix.astype(BF16), 2048, MIX_WIDTH // 3, F32)
    gl = _matmul(x, w_gl.astype(BF16), 2048, 1024, BF16)
    y_a = _hgrn(p, lb, hgrn_norm_w, bsz, seq)
    qn, qp, kn, kp, vt = _mla_proj(p, positions, mla_q_norm_w, mla_wq_b, mla_kv_norm_w, mla_wkv_b, min(256, seq))
    y_b = _flash(qn, qp, kn, kp, vt, bsz, seq)
    y_s = _s5(p, s5_a_re, s5_a_im, s5_log_dt, s5_b_re, s5_b_im, s5_c_re, s5_c_im, s5_d, bsz, seq)
    y_d = _lru(p, lru_conv_w, lru_conv_b, lru_wa, lru_ba, lru_wx, lru_bx, lru_a_param, bsz, seq)
    h, h16, logits_t = _merge(y_a, y_b, y_s, y_d, gl, x, w_branch, w_out, s5_w_glu, s5_b_glu, b_gate,
                              ln1_w, ln1_b, router_w, router_b)
    return _moe(h, h16, logits_t, moe_w1, moe_b1, moe_w2, moe_b2, ln2_w, ln2_b, expert_offset)


def kernel(x, positions, w_in, b_gate, hgrn_lb_logits, hgrn_norm_w, mla_q_norm_w, mla_wq_b, mla_kv_norm_w, mla_wkv_b, s5_a_re, s5_a_im, s5_log_dt, s5_b_re, s5_b_im, s5_c_re, s5_c_im, s5_d, s5_w_glu, s5_b_glu, lru_conv_w, lru_conv_b, lru_wa, lru_ba, lru_wx, lru_bx, lru_a_param, w_branch, w_out, ln1_w, ln1_b, ln2_w, ln2_b, router_w, router_b, moe_w1, moe_b1, moe_w2, moe_b2):
    bsz, seq, _ = x.shape
    probs = jax.nn.softmax(hgrn_lb_logits.astype(F32), axis=0)
    lower_bounds = jnp.cumsum(probs, axis=0) - probs[0:1]
    per_layer = (w_in, b_gate, hgrn_norm_w, mla_q_norm_w, mla_wq_b, mla_kv_norm_w, mla_wkv_b,
                 s5_a_re, s5_a_im, s5_log_dt, s5_b_re, s5_b_im, s5_c_re, s5_c_im, s5_d, s5_w_glu, s5_b_glu,
                 lru_conv_w, lru_conv_b, lru_wa, lru_ba, lru_wx, lru_bx, lru_a_param,
                 w_branch, w_out, ln1_w, ln1_b, ln2_w, ln2_b, router_w, router_b, moe_w1, moe_b1, moe_w2, moe_b2)
    xf = x.reshape(bsz * seq, D_MODEL)
    w1_all = moe_w1.reshape((DEPTH * N_EXPERTS,) + moe_w1.shape[2:])
    w2_all = moe_w2.reshape((DEPTH * N_EXPERTS,) + moe_w2.shape[2:])
    for l in range(DEPTH):
        args = [a[l] for a in per_layer]
        args[-4], args[-2] = w1_all, w2_all
        xf = _layer(xf, positions, lower_bounds[l], bsz, seq, *args, expert_offset=l * N_EXPERTS)
    return xf.reshape(bsz, seq, D_MODEL)
```

```python
import functools
import math

import numpy as np
import jax
import jax.numpy as jnp
from jax import lax
from jax.experimental import pallas as pl
from jax.experimental.pallas import tpu as pltpu

F32 = jnp.float32
BF16 = jnp.bfloat16

D_MODEL = 1024
DEPTH = 2
CHUNK = 64

HGRN_HEADS = 4
HGRN_DK = 128
HGRN_WIDTH = 512
HGRN_F_MIN = 1e-30
HGRN_CHUNK = 64
HGRN_LEVELS = (1, 2, 4, 8, 16, 32)
HGRN_PAR = 2

MLA_HEADS = 4
MLA_Q_RANK = 256
MLA_KV_RANK = 128
MLA_NOPE = 128
MLA_ROPE = 64
MLA_V = 128
MLA_QK = MLA_NOPE + MLA_ROPE
MLA_VROWS = MLA_V + 16
ROPE_THETA = 10000.0

S5_GROUPS = 32
S5_GROUP_CH = 16
S5_STATE = 64
S5_WIDTH = 512
S5_BLOCK = 16
S5_PAR = 4

LRU_WIDTH = 512
LRU_BLOCKS = 8
LRU_BLOCK_W = 64
LRU_CONV = 4
LRU_C = 8.0

N_BRANCH = 4
BRANCH_WIDTH = 512

N_EXPERTS = 32
TOP_K = 4
EXPERT_FF = 1024
SWIGLU_ALPHA = 1.702
SWIGLU_LIMIT = 7.0
EXPERT_TILE = 1024

DEEPNORM_ALPHA = (2.0 * DEPTH) ** 0.25
LN_EPS = 1e-5
RMS_EPS = 1e-6

COL_HQ, COL_HF, COL_HI, COL_HG = 0, 512, 1024, 1536
COL_SU, COL_LX, COL_LG = 2048, 2560, 3072
COL_CQ, COL_CKV, COL_KPE, COL_KPER = 3584, 3840, 3968, 4096
MIX_WIDTH = 4224

VMEM_LIMIT = 56 * 1024 * 1024


def _cparams(sem):
    return pltpu.CompilerParams(dimension_semantics=sem, vmem_limit_bytes=VMEM_LIMIT)


def _dot(a, b):
    return jnp.dot(a, b, preferred_element_type=F32)


def _dot_nt(a, b):
    return lax.dot_general(a, b, (((1,), (1,)), ((), ())), preferred_element_type=F32)


def _dot_tn(a, b):
    return lax.dot_general(a, b, (((0,), (0,)), ((), ())), preferred_element_type=F32)


def _split(x):
    hi = x.astype(BF16)
    lo = (x - hi.astype(F32)).astype(BF16)
    return hi, lo


def _dot3(a, b_hi, b_lo):
    a_hi, a_lo = _split(a)
    return _dot(a_hi, b_hi) + (_dot(a_lo, b_hi) + _dot(a_hi, b_lo))


def _pack_pairs(x):
    w = x.shape[-1] // 2
    lo = lax.bitcast_convert_type(x[:, :w].astype(BF16).astype(F32), jnp.uint32)
    hi = lax.bitcast_convert_type(x[:, w:].astype(BF16).astype(F32), jnp.uint32)
    return lax.shift_right_logical(lo, jnp.uint32(16)) | (hi & jnp.uint32(0xFFFF0000))


def _unpack_pairs(u):
    lo = lax.bitcast_convert_type(lax.shift_left(u, jnp.uint32(16)), F32)
    hi = lax.bitcast_convert_type(u & jnp.uint32(0xFFFF0000), F32)
    return jnp.concatenate([lo, hi], axis=-1)


def _sigmoid(x):
    return 0.5 * jnp.tanh(0.5 * x) + 0.5


def _silu(x):
    return x * _sigmoid(x)


def _shift_rows(x, k, fill):
    rows = lax.broadcasted_iota(jnp.int32, x.shape, 0)
    return jnp.where(rows >= k, pltpu.roll(x, k, 0), fill)


def _mm_kernel(x_ref, w_ref, o_ref):
    o_ref[...] = _dot(x_ref[...].astype(BF16), w_ref[...]).astype(o_ref.dtype)


def _matmul(x, w, tm, tn, out_dtype):
    n, k = x.shape
    m = w.shape[1]
    tm = min(tm, n)
    return pl.pallas_call(
        _mm_kernel,
        grid=(n // tm, m // tn),
        in_specs=[pl.BlockSpec((tm, k), lambda i, j: (i, 0)),
                  pl.BlockSpec((k, tn), lambda i, j: (0, j))],
        out_specs=pl.BlockSpec((tm, tn), lambda i, j: (i, j)),
        out_shape=jax.ShapeDtypeStruct((n, m), out_dtype),
        compiler_params=_cparams(("parallel", "arbitrary")),
        name="in_proj",
    )(x, w)


def _hgrn_tables():
    c = HGRN_CHUNK
    t = np.arange(c)
    windows = [np.tril(np.ones((c, c), bool)),
               np.triu(np.ones((c, c), bool), 1)]
    masks = [np.eye(c, dtype=bool)]
    for h in HGRN_LEVELS:
        blk = t // h
        odd = blk % 2 == 1
        masks.append(odd[:, None] & (blk[None, :] == blk[:, None] - 1))
        if h > 1:
            windows.append((t[None, :] >= (h * blk)[:, None]) & (t[None, :] <= t[:, None]))
            windows.append((t[None, :] > t[:, None]) & (t[None, :] <= (h * blk + h - 1)[:, None]))
    return np.concatenate(windows, 0).astype(np.float32), np.stack(masks).astype(np.float32)


def _hgrn_kernel(hq_ref, hf_ref, hi_ref, hg_ref, lb_ref, nw_ref, win_ref, mask_ref,
                 o_ref, state_ref, *, n_chunks):
    c = HGRN_CHUNK
    heads = range(HGRN_HEADS)
    hs = [slice(h * HGRN_DK, (h + 1) * HGRN_DK) for h in heads]
    n_lvl = len(HGRN_LEVELS) + 1

    @pl.when(pl.program_id(1) == 0)
    def _():
        state_ref[...] = jnp.zeros_like(state_ref)

    def group_body(gi, carry):
        par = range(HGRN_PAR)
        rows = [pl.ds(pl.multiple_of((gi * HGRN_PAR + cc) * c, c), c) for cc in par]
        lb = lb_ref[...]
        z = [hf_ref[rows[cc], :] for cc in par]
        th = [jnp.tanh(0.5 * z[cc]) for cc in par]
        f = [jnp.maximum(lb + (1.0 - lb) * (0.5 + 0.5 * th[cc]), HGRN_F_MIN) for cc in par]
        k = [(1.0 - lb) * (0.5 - 0.5 * th[cc]) for cc in par]
        q = [_silu(hq_ref[rows[cc], :]) for cc in par]
        v16 = [hi_ref[rows[cc], :].astype(BF16) for cc in par]
        lf = [_split(jnp.log(f[cc])) for cc in par]
        lf_stack = jnp.concatenate([jnp.concatenate([lf[cc][part] for cc in par], axis=-1) for part in range(2)],
                                   axis=0)
        e_all = jnp.exp(_dot(win_ref[...], lf_stack))
        e = [e_all[:, cc * HGRN_WIDTH:(cc + 1) * HGRN_WIDTH] for cc in par]
        blk = lambda cc, w: e[cc][w * c:(w + 1) * c]
        qs, ks, q_in, k_out, decay = [], [], [], [], []
        for cc in par:
            k16 = k[cc].astype(BF16)
            qs.append([q[cc].astype(BF16), (q[cc] * f[cc]).astype(BF16)]
                      + [(q[cc] * blk(cc, 2 * w)).astype(BF16) for w in range(1, n_lvl - 1)])
            ks.append([k16, k16] + [(k[cc] * blk(cc, 2 * w + 1)).astype(BF16) for w in range(1, n_lvl - 1)])
            q_in.append((q[cc] * blk(cc, 0)).astype(BF16))
            k_out.append((k[cc] * blk(cc, 1)).astype(BF16))
            decay.append(e[cc][c - 1:c, :])
        scores = [[None] * HGRN_HEADS for _ in par]
        for lv in range(n_lvl):
            s_l = [[_dot_nt(qs[cc][lv][:, hs[h]], ks[cc][lv][:, hs[h]]) for h in heads] for cc in par]
            for cc in par:
                for h in heads:
                    term = mask_ref[lv] * s_l[cc][h]
                    scores[cc][h] = term if scores[cc][h] is None else scores[cc][h] + term
        intra = [[_dot(scores[cc][h].astype(BF16), v16[cc][:, hs[h]]) for h in heads] for cc in par]
        upd = [[_dot_tn(v16[cc][:, hs[h]], k_out[cc][:, hs[h]]) for h in heads] for cc in par]
        st = [state_ref[h] for h in heads]
        for cc in par:
            o = [intra[cc][h] + _dot_nt(q_in[cc][:, hs[h]], st[h].astype(BF16)) for h in heads]
            st = [st[h] * decay[cc][:, hs[h]] + upd[cc][h] for h in heads]
            o = [o[h] * lax.rsqrt(jnp.mean(o[h] * o[h], axis=-1, keepdims=True) + RMS_EPS) for h in heads]
            o_ref[rows[cc], :] = jnp.concatenate(o, axis=-1) * nw_ref[...] * _silu(hg_ref[rows[cc], :])
        for h in heads:
            state_ref[h] = st[h]
        return carry

    lax.fori_loop(0, n_chunks // HGRN_PAR, group_body, 0)


def _hgrn(p, lb, norm_w, bsz, seq):
    t = min(256, seq)
    nt = seq // t
    win, mask = _hgrn_tables()
    col = lambda cb: pl.BlockSpec((t, HGRN_WIDTH), lambda b, i: (b * nt + i, cb))
    const2 = lambda shape: pl.BlockSpec(shape, lambda b, i: (0,) * len(shape))
    return pl.pallas_call(
        functools.partial(_hgrn_kernel, n_chunks=t // HGRN_CHUNK),
        grid=(bsz, nt),
        in_specs=[col(COL_HQ // 512), col(COL_HF // 512), col(COL_HI // 512), col(COL_HG // 512),
                  const2((1, HGRN_WIDTH)), const2((1, HGRN_WIDTH)),
                  const2((win.shape[0], 2 * win.shape[1])), const2(mask.shape)],
        out_specs=pl.BlockSpec((t, HGRN_WIDTH), lambda b, i: (b * nt + i, 0)),
        out_shape=jax.ShapeDtypeStruct((bsz * seq, HGRN_WIDTH), F32),
        scratch_shapes=[pltpu.VMEM((HGRN_HEADS, HGRN_DK, HGRN_DK), F32)],
        compiler_params=_cparams(("parallel", "arbitrary")),
        name="hgrn",
    )(p, p, p, p, lb.reshape(1, -1), norm_w.reshape(1, -1), jnp.asarray(np.concatenate([win, win], 1), BF16),
      jnp.asarray(mask))


def _rms(x, w):
    ms = jnp.mean(x * x, axis=-1, keepdims=True)
    return x * lax.rsqrt(ms + RMS_EPS) * w


def _mla_proj_kernel(cq_ref, ckv_ref, kpe_ref, kper_ref, ang_ref, qnw_ref, kvnw_ref,
                     wqn_ref, wqp_ref, wqpr_ref, wkn_ref, wv_ref,
                     qn_ref, qp_ref, kn_ref, kp_ref, v_ref):
    ang = ang_ref[...]
    cos = jnp.cos(ang)
    sin = jnp.sin(ang)
    qn = _rms(cq_ref[...], qnw_ref[...]).astype(BF16)
    cos4 = jnp.concatenate([cos] * MLA_HEADS, axis=-1)
    sin4 = jnp.concatenate([sin] * MLA_HEADS, axis=-1)
    qn_ref[...] = _dot(qn, wqn_ref[...]).astype(BF16)
    qp_ref[...] = (_dot(qn, wqp_ref[...]) * cos4 + _dot(qn, wqpr_ref[...]) * sin4).astype(BF16)
    cn = _rms(ckv_ref[...], kvnw_ref[...]).astype(BF16)
    kn_ref[...] = _dot(cn, wkn_ref[...]).astype(BF16)
    vt = _dot_nt(wv_ref[...], cn)
    ones = jnp.ones((MLA_VROWS - MLA_V, vt.shape[1]), F32)
    v_ref[0] = jnp.concatenate(
        [piece for h in range(MLA_HEADS) for piece in (vt[h * MLA_V:(h + 1) * MLA_V], ones)], axis=0).astype(BF16)
    kp_ref[...] = (kpe_ref[...] * cos + kper_ref[...] * sin).astype(BF16)


def _rot_half_cols(w):
    half = w.shape[-1] // 2
    return jnp.concatenate([-w[..., half:], w[..., :half]], axis=-1)


def _mla_proj(p, positions, q_norm_w, wq_b, kv_norm_w, wkv_b, tm):
    n = p.shape[0]
    scale = MLA_QK ** -0.5 * math.log2(math.e)
    wq = wq_b.reshape(MLA_Q_RANK, MLA_HEADS, MLA_QK) * scale
    w_nope = wq[:, :, :MLA_NOPE].reshape(MLA_Q_RANK, MLA_HEADS * MLA_NOPE)
    w_pe = wq[:, :, MLA_NOPE:]
    pad = jnp.zeros((MLA_Q_RANK, MLA_HEADS, 128 - MLA_ROPE), F32)
    w_pe_p = jnp.concatenate([w_pe, pad], axis=-1).reshape(MLA_Q_RANK, MLA_HEADS * 128)
    w_per_p = jnp.concatenate([_rot_half_cols(w_pe), pad], axis=-1).reshape(MLA_Q_RANK, MLA_HEADS * 128)
    wkv = wkv_b.reshape(MLA_KV_RANK, MLA_HEADS, MLA_NOPE + MLA_V)
    w_kn = wkv[:, :, :MLA_NOPE].reshape(MLA_KV_RANK, MLA_HEADS * MLA_NOPE)
    w_v = wkv[:, :, MLA_NOPE:].reshape(MLA_KV_RANK, MLA_HEADS * MLA_V)
    inv_freq = ROPE_THETA ** (-jnp.arange(0, MLA_ROPE, 2, dtype=F32) / MLA_ROPE)
    freq_lane = jnp.concatenate([inv_freq, inv_freq, jnp.zeros((128 - MLA_ROPE,), F32)])
    ang = positions.reshape(n, 1).astype(F32) * freq_lane[None, :]

    row = lambda w, cb: pl.BlockSpec((tm, w), lambda i: (i, cb))
    full = lambda a: pl.BlockSpec(a.shape, lambda i: (0,) * a.ndim)
    weights = [w_nope.astype(BF16), w_pe_p.astype(BF16), w_per_p.astype(BF16), w_kn.astype(BF16),
               w_v.T.astype(BF16)]
    qnw = q_norm_w.reshape(1, -1)
    kvnw = kv_norm_w.reshape(1, -1)
    outs = pl.pallas_call(
        _mla_proj_kernel,
        grid=(n // tm,),
        in_specs=[row(256, COL_CQ // 256), row(128, COL_CKV // 128), row(128, COL_KPE // 128),
                  row(128, COL_KPER // 128), row(128, 0), full(qnw), full(kvnw)] + [full(w) for w in weights],
        out_specs=[row(512, 0), row(512, 0), row(512, 0), row(128, 0),
                   pl.BlockSpec((1, MLA_HEADS * MLA_VROWS, tm), lambda i: (i, 0, 0))],
        out_shape=[jax.ShapeDtypeStruct((n, 512), BF16), jax.ShapeDtypeStruct((n, 512), BF16),
                   jax.ShapeDtypeStruct((n, 512), BF16), jax.ShapeDtypeStruct((n, 128), BF16),
                   jax.ShapeDtypeStruct((n // tm, MLA_HEADS * MLA_VROWS, tm), BF16)],
        compiler_params=_cparams(("parallel",)),
        name="mla_proj",
    )(p, p, p, p, ang, qnw, kvnw, *weights)
    return outs


def _flash_kernel(qn_ref, qp_ref, kn_ref, kp_ref, vt_ref, o_ref, acc_ref, *, tq):
    i = pl.program_id(1)
    heads = range(MLA_HEADS)
    hs = [slice(h * 128, (h + 1) * 128) for h in heads]
    vs = [slice(h * MLA_VROWS, (h + 1) * MLA_VROWS) for h in heads]
    acc_ref[...] = jnp.zeros_like(acc_ref)

    def scores(j):
        r0 = pl.multiple_of(j * tq, tq)
        kp = kp_ref[pl.ds(r0, tq), :]
        return tuple(_dot_nt(jnp.concatenate([kn_ref[pl.ds(r0, tq), hs[h]], kp], axis=-1),
                             jnp.concatenate([qn_ref[:, hs[h]], qp_ref[:, hs[h]]], axis=-1)) for h in heads)

    def consume(j, s, m, mask):
        p, m_new, alpha = [], [], []
        for h in heads:
            sh = s[h] if mask is None else jnp.where(mask, s[h], -1e30)
            mn = jnp.maximum(m[h], jnp.max(sh, axis=0, keepdims=True))
            alpha.append(jnp.exp2(m[h] - mn))
            p.append(jnp.exp2(sh - mn).astype(BF16))
            m_new.append(mn)
        vt = vt_ref[j]
        pv = [_dot(vt[vs[h], :], p[h]) for h in heads]
        for h in heads:
            acc_ref[h] = acc_ref[h] * alpha[h] + pv[h]
        return tuple(m_new)

    def body(j, carry):
        s, m = carry
        s_next = scores(j + 1)
        return s_next, consume(j, s, m, None)

    m0 = tuple(jnp.full((1, tq), -1e30, F32) for _ in heads)
    s, m = lax.fori_loop(0, i, body, (scores(0), m0))
    shift = CHUNK.bit_length() - 1
    kc = lax.shift_right_logical(lax.broadcasted_iota(jnp.int32, (tq, tq), 0), shift)
    qc = lax.shift_right_logical(lax.broadcasted_iota(jnp.int32, (tq, tq), 1), shift)
    consume(i, s, m, kc <= qc)
    for h in heads:
        acc = acc_ref[h]
        o_ref[:, hs[h]] = (acc[:MLA_V] / acc[MLA_V:MLA_V + 1]).T.astype(o_ref.dtype)


def _flash(qn, qp, kn, kp, vt, bsz, seq):
    tq = vt.shape[2]
    nq = seq // tq
    width = MLA_HEADS * 128
    qspec = pl.BlockSpec((tq, width), lambda b, i: (b * nq + i, 0))
    kspec = pl.BlockSpec((seq, width), lambda b, i: (b, 0))
    kpspec = pl.BlockSpec((seq, 128), lambda b, i: (b, 0))
    vspec = pl.BlockSpec((nq, MLA_HEADS * MLA_VROWS, tq), lambda b, i: (b, 0, 0))
    return pl.pallas_call(
        functools.partial(_flash_kernel, tq=tq),
        grid=(bsz, nq),
        in_specs=[qspec, qspec, kspec, kpspec, vspec],
        out_specs=qspec,
        out_shape=jax.ShapeDtypeStruct((bsz * seq, width), BF16),
        scratch_shapes=[pltpu.VMEM((MLA_HEADS, MLA_VROWS, tq), F32)],
        compiler_params=_cparams(("parallel", "arbitrary")),
        name="flash",
    )(qn, qp, kn, kp, vt)


def _cmul(a, b):
    return a[0] * b[0] - a[1] * b[1], a[0] * b[1] + a[1] * b[0]


def _s5_tables(a_re, a_im, log_dt, b_re, b_im, c_re, c_im, d_skip, steps):
    lb = S5_BLOCK
    f32 = lambda v: v.astype(F32)
    a_re, a_im, b_re, b_im, c_re, c_im = map(f32, (a_re, a_im, b_re, b_im, c_re, c_im))
    dt = jnp.exp(f32(log_dt))[:, None]
    ld = (a_re * dt, a_im * dt)

    def power(tau):
        mag = jnp.exp(ld[0] * tau)
        return mag * jnp.cos(ld[1] * tau), mag * jnp.sin(ld[1] * tau)

    lam_bar = power(1.0)
    inv = a_re * a_re + a_im * a_im
    ratio = _cmul((lam_bar[0] - 1.0, lam_bar[1]), (a_re / inv, -a_im / inv))
    b_bar = _cmul((ratio[0][..., None], ratio[1][..., None]), (b_re, b_im))
    tau = jnp.arange(lb + 1, dtype=F32)[:, None, None]
    pw = power(tau)
    cp = _cmul((c_re[None], c_im[None]), (pw[0][:, :, None, :], pw[1][:, :, None, :]))
    kern = jnp.einsum('tgcn,gnd->tgcd', cp[0][:lb], b_bar[0]) - jnp.einsum('tgcn,gnd->tgcd', cp[1][:lb], b_bar[1])
    s_idx = np.arange(lb)[:, None]
    t_idx = np.arange(lb)[None, :]
    lag = np.clip(t_idx - s_idx, 0, lb - 1)
    causal = jnp.asarray((t_idx >= s_idx).astype(np.float32))
    toe = kern[lag] * causal[:, :, None, None, None]
    skip = jnp.asarray(np.eye(lb, dtype=np.float32))[:, :, None, None, None] * (
        d_skip.astype(F32)[None, None, :, :, None] * jnp.eye(S5_GROUP_CH, dtype=F32)[None, None, None])
    toe = (toe + skip).transpose(2, 1, 3, 0, 4).reshape(S5_GROUPS, lb * S5_GROUP_CH, lb * S5_GROUP_CH)
    emit = jnp.concatenate([cp[0][1:], -cp[1][1:]], axis=-1)
    emit = emit.transpose(1, 0, 2, 3).reshape(S5_GROUPS, lb * S5_GROUP_CH, 2 * S5_STATE)
    rev = (pw[0][:lb][::-1][..., None], pw[1][:lb][::-1][..., None])
    bp = _cmul(rev, (b_bar[0][None], b_bar[1][None]))
    fold = jnp.concatenate([bp[0], bp[1]], axis=2)
    fold = fold.transpose(1, 2, 0, 3).reshape(S5_GROUPS, 2 * S5_STATE, lb * S5_GROUP_CH)
    jump = (lb * 2.0 ** jnp.arange(steps, dtype=F32))[:, None, None]
    pj = power(jump)
    pw_re = jnp.concatenate([pj[0], pj[0]], axis=-1).transpose(1, 2, 0)
    pw_im = jnp.concatenate([-pj[1], pj[1]], axis=-1).transpose(1, 2, 0)
    return toe, emit, fold, pw_re, pw_im


def _s5_kernel(su0_ref, su1_ref, su2_ref, su3_ref, toe_ref, emit_ref, fold_ref, pre_ref, pim_ref, y_ref,
               ut_ref, carry_ref, *, r, steps):
    lb = S5_BLOCK
    gc = S5_GROUP_CH

    @pl.when(pl.program_id(1) == 0)
    def _():
        carry_ref[...] = jnp.zeros_like(carry_ref)

    for s in range(lb):
        for c, su_ref in enumerate((su0_ref, su1_ref, su2_ref, su3_ref)):
            ut_ref[s, c * 128:(c + 1) * 128, :] = su_ref[pl.ds(s, r, stride=lb), :].T
    lane = lax.broadcasted_iota(jnp.int32, (2 * S5_STATE, r), 1)
    swap = lambda v: jnp.concatenate([v[S5_STATE:], v[:S5_STATE]], axis=0)

    def groups(gi, carry):
        par = range(S5_PAR)
        g = [gi * S5_PAR + a for a in par]
        g0 = [pl.multiple_of(g[a] * gc, gc) for a in par]
        ug = [jnp.concatenate([ut_ref[s, pl.ds(g0[a], gc), :] for s in range(lb)], axis=0).astype(BF16)
              for a in par]
        pre = [pre_ref[g[a]] for a in par]
        pim = [pim_ref[g[a]] for a in par]
        cmul = lambda a, j, v: pre[a][:, j:j + 1] * v + pim[a][:, j:j + 1] * swap(v)
        car = [carry_ref[g[a]] for a in par]
        x = [_dot(fold_ref[g[a]], ug[a]) for a in par]
        x = [x[a] + jnp.where(lane == 0, cmul(a, 0, car[a]), 0.0) for a in par]
        for j in range(steps):
            xs = [jnp.where(lane >= 2 ** j, pltpu.roll(x[a], 2 ** j, 1), 0.0) for a in par]
            x = [x[a] + cmul(a, j, xs[a]) for a in par]
        x_prev = [jnp.where(lane == 0, car[a], pltpu.roll(x[a], 1, 1)) for a in par]
        y = [_dot(toe_ref[g[a]], ug[a]) + _dot(emit_ref[g[a]], x_prev[a].astype(BF16)) for a in par]
        for a in par:
            carry_ref[g[a]] = jnp.broadcast_to(x[a][:, r - 1:r], x[a].shape)
            for t in range(lb):
                ut_ref[t, pl.ds(g0[a], gc), :] = y[a][t * gc:(t + 1) * gc]
        return carry

    lax.fori_loop(0, S5_GROUPS // S5_PAR, groups, 0)
    for t in range(lb):
        for c in range(S5_WIDTH // 128):
            y_ref[c, pl.ds(t, r, stride=lb), :] = ut_ref[t, c * 128:(c + 1) * 128, :].T


def _s5(p, a_re, a_im, log_dt, b_re, b_im, c_re, c_im, d_skip, bsz, seq):
    lb = S5_BLOCK
    r = min(128, seq // lb)
    nt = seq // (lb * r)
    ncb = S5_WIDTH // 128
    steps = max(1, int(math.ceil(math.log2(r))))
    toe, emit, fold, pw_re, pw_im = _s5_tables(a_re, a_im, log_dt, b_re, b_im, c_re, c_im, d_skip, steps)
    consts = [toe.astype(BF16), emit.astype(BF16), fold.astype(BF16), pw_re, pw_im]
    full = lambda a: pl.BlockSpec(a.shape, lambda b, t: (0,) * a.ndim)
    return pl.pallas_call(
        functools.partial(_s5_kernel, r=r, steps=steps),
        grid=(bsz, nt),
        in_specs=[pl.BlockSpec((lb * r, 128), functools.partial(lambda b, t, c: (b * nt + t, COL_SU // 128 + c), c=c))
                  for c in range(ncb)] + [full(c) for c in consts],
        out_specs=pl.BlockSpec((ncb, lb * r, 128), lambda b, t: (0, b * nt + t, 0)),
        out_shape=jax.ShapeDtypeStruct((ncb, bsz * seq, 128), F32),
        scratch_shapes=[pltpu.VMEM((lb, S5_WIDTH, r), F32), pltpu.VMEM((S5_GROUPS, 2 * S5_STATE, r), F32)],
        compiler_params=_cparams(("parallel", "arbitrary")),
        name="s5",
    )(*([p] * ncb), *consts)


def _lru_kernel(lx_ref, lg_ref, cw_ref, cb_ref, wg_ref, bg_ref, sp_ref, o_ref, tail_ref, h_ref, *, ts):
    @pl.when(pl.program_id(1) == 0)
    def _():
        tail_ref[...] = jnp.zeros_like(tail_ref)
        h_ref[...] = jnp.zeros_like(h_ref)

    x = lx_ref[...]
    tail_ref[8:, :] = x
    xc = cb_ref[...] + cw_ref[LRU_CONV - 1:LRU_CONV, :] * x
    for j in range(1, LRU_CONV):
        xc = xc + cw_ref[LRU_CONV - 1 - j:LRU_CONV - j, :] * tail_ref[8 - j:8 - j + ts, :]
    tail_ref[0:8, :] = x[ts - 8:, :]
    gates = _sigmoid(_dot3(xc, wg_ref[0], wg_ref[1]) + bg_ref[...])
    r = gates[:, :LRU_WIDTH]
    ig = gates[:, LRU_WIDTH:]
    log_a = -LRU_C * r * sp_ref[...]
    a = jnp.exp(log_a)
    gap = jnp.maximum(1.0 - a * a, 0.0)
    b = jnp.where(gap > 0.0, gap * lax.rsqrt(gap), 0.0) * (ig * xc)
    k = 1
    while k < ts:
        b = b + a * _shift_rows(b, k, 0.0)
        a = a * _shift_rows(a, k, 1.0)
        k *= 2
    h = b + a * h_ref[0:1, :]
    h_ref[0:1, :] = h[ts - 1:ts, :]
    o_ref[...] = h * jax.nn.gelu(lg_ref[...])


def _block_diag(w):
    nb, bw, _ = w.shape
    eye = jnp.eye(nb, dtype=w.dtype)
    return (eye[:, None, :, None] * w[:, :, None, :]).reshape(nb * bw, nb * bw)


def _lru(p, conv_w, conv_b, wa, ba, wx, bx, a_param, bsz, seq):
    ts = min(512, seq)
    nt = seq // ts
    wg = jnp.concatenate([_block_diag(wa), _block_diag(wx)], axis=1)
    wg_hi = wg.astype(BF16)
    wg2 = jnp.stack([wg_hi, (wg - wg_hi.astype(F32)).astype(BF16)])
    bg = jnp.concatenate([ba, bx]).reshape(1, -1)
    sp = jax.nn.softplus(a_param.astype(F32)).reshape(1, -1)
    col = lambda cb: pl.BlockSpec((ts, LRU_WIDTH), lambda b, t: (b * nt + t, cb))
    full = lambda a: pl.BlockSpec(a.shape, lambda b, t: (0,) * a.ndim)
    cb2 = conv_b.reshape(1, -1)
    return pl.pallas_call(
        functools.partial(_lru_kernel, ts=ts),
        grid=(bsz, nt),
        in_specs=[col(COL_LX // 512), col(COL_LG // 512), full(conv_w), full(cb2), full(wg2), full(bg), full(sp)],
        out_specs=pl.BlockSpec((ts, LRU_WIDTH), lambda b, t: (b * nt + t, 0)),
        out_shape=jax.ShapeDtypeStruct((bsz * seq, LRU_WIDTH), F32),
        scratch_shapes=[pltpu.VMEM((8 + ts, LRU_WIDTH), F32), pltpu.VMEM((8, LRU_WIDTH), F32)],
        compiler_params=_cparams(("parallel", "arbitrary")),
        name="rglru",
    )(p, p, conv_w, cb2, wg2, bg, sp)


def _layer_norm(x, w, b):
    xc = x - jnp.mean(x, axis=-1, keepdims=True)
    var = jnp.mean(xc * xc, axis=-1, keepdims=True)
    return xc * lax.rsqrt(var + LN_EPS) * w + b


def _merge_kernel(ya_ref, yb_ref, ys_ref, yd_ref, gl_ref, x_ref, wbr_ref, wout_ref, wglu_ref, bglu_ref,
                  bgate_ref, lnw_ref, lnb_ref, rw_ref, rb_ref, h_ref, hb_ref, lg_ref):
    yc = jax.nn.gelu(jnp.concatenate([ys_ref[c] for c in range(S5_WIDTH // 128)], axis=-1))
    yc = yc * _sigmoid(_dot(yc.astype(BF16), wglu_ref[...]) + bglu_ref[...])
    branches = (ya_ref[...].astype(BF16), yb_ref[...], yc.astype(BF16), yd_ref[...].astype(BF16))
    mix = None
    for bi, yb in enumerate(branches):
        cs = slice(bi * D_MODEL, (bi + 1) * D_MODEL)
        gate = _sigmoid(gl_ref[:, cs].astype(F32) + bgate_ref[:, cs])
        term = gate * _dot(yb, wbr_ref[bi])
        mix = term if mix is None else mix + term
    mo = _dot(mix.astype(BF16), wout_ref[...])
    h = _layer_norm(DEEPNORM_ALPHA * x_ref[...] + mo, lnw_ref[...], lnb_ref[...])
    h_ref[...] = h
    h_hi, h_lo = _split(h)
    hb_ref[...] = _pack_pairs(h)
    hcat = jnp.concatenate([h_hi, h_lo, h_hi], axis=-1)
    lg_ref[...] = _dot_nt(rw_ref[...], hcat) + rb_ref[...]


def _merge(ya, yb, ys, yd, gl, x, w_branch, w_out, w_glu, b_glu, b_gate, ln_w, ln_b, router_w, router_b):
    n = x.shape[0]
    tm = min(512, n)
    rwt = router_w.T.astype(F32)
    rw_hi = rwt.astype(BF16)
    rw_lo = (rwt - rw_hi.astype(F32)).astype(BF16)
    rw3 = jnp.concatenate([rw_hi, rw_hi, rw_lo], axis=1)
    row = lambda w: pl.BlockSpec((tm, w), lambda i: (i, 0))
    full = lambda a: pl.BlockSpec(a.shape, lambda i: (0,) * a.ndim)
    consts = [w_branch.astype(BF16), w_out.astype(BF16), w_glu.astype(BF16), b_glu.reshape(1, -1),
              b_gate.reshape(1, -1), ln_w.reshape(1, -1), ln_b.reshape(1, -1), rw3, router_b.reshape(-1, 1)]
    return pl.pallas_call(
        _merge_kernel,
        grid=(n // tm,),
        in_specs=[row(512), row(512), pl.BlockSpec((S5_WIDTH // 128, tm, 128), lambda i: (0, i, 0)), row(512),
                  row(N_BRANCH * D_MODEL), row(D_MODEL)]
                 + [full(c) for c in consts],
        out_specs=[row(D_MODEL), row(D_MODEL // 2), pl.BlockSpec((N_EXPERTS, tm), lambda i: (0, i))],
        out_shape=[jax.ShapeDtypeStruct((n, D_MODEL), F32), jax.ShapeDtypeStruct((n, D_MODEL // 2), jnp.uint32),
                   jax.ShapeDtypeStruct((N_EXPERTS, n), F32)],
        compiler_params=_cparams(("parallel",)),
        name="merge",
    )(ya, yb, ys, yd, gl, x, *consts)


def _route_kernel(lg_ref, tri_ref, idx_ref, w_ref, rank_ref, cnt_ref, carry_ref, *, tr):
    @pl.when(pl.program_id(0) == 0)
    def _():
        carry_ref[...] = jnp.zeros_like(carry_ref)

    l = lg_ref[...]
    eidx = lax.broadcasted_iota(jnp.int32, l.shape, 0)
    vals, hots = [], []
    for _ in range(TOP_K):
        m = jnp.max(l, axis=0, keepdims=True)
        first = jnp.min(jnp.where(l == m, eidx, N_EXPERTS), axis=0, keepdims=True)
        hot = eidx == first
        l = jnp.where(hot, -jnp.inf, l)
        vals.append(m)
        hots.append(hot)
        idx_ref[len(vals) - 1:len(vals), :] = first
    ex = [jnp.exp(v - vals[0]) for v in vals]
    den = ex[0] + ex[1] + ex[2] + ex[3]
    for k in range(TOP_K):
        w_ref[k:k + 1, :] = ex[k] / den
    member = jnp.zeros(l.shape, F32)
    for hot in hots:
        member = member + hot.astype(F32)
    before = _dot(member.astype(BF16), tri_ref[...]) + carry_ref[:, 0:1]
    for k in range(TOP_K):
        rank = jnp.sum(jnp.where(hots[k], before, 0.0), axis=0, keepdims=True)
        rank_ref[k:k + 1, :] = rank.astype(jnp.int32)
    carry_ref[...] = carry_ref[...] + jnp.sum(member, axis=1, keepdims=True)
    cnt_ref[...] = carry_ref[...]
    idx_ref[TOP_K:, :] = jnp.zeros((8 - TOP_K, tr), jnp.int32)
    w_ref[TOP_K:, :] = jnp.zeros((8 - TOP_K, tr), F32)
    rank_ref[TOP_K:, :] = jnp.zeros((8 - TOP_K, tr), jnp.int32)


def _route(logits_t):
    n = logits_t.shape[1]
    tr = min(512, n)
    tri = jnp.asarray(np.triu(np.ones((tr, tr), np.float32), 1), BF16)
    tok = pl.BlockSpec((8, tr), lambda i: (0, i))
    return pl.pallas_call(
        functools.partial(_route_kernel, tr=tr),
        grid=(n // tr,),
        in_specs=[pl.BlockSpec((N_EXPERTS, tr), lambda i: (0, i)), pl.BlockSpec((tr, tr), lambda i: (0, 0))],
        out_specs=[tok, tok, tok, pl.BlockSpec((N_EXPERTS, 128), lambda i: (0, 0))],
        out_shape=[jax.ShapeDtypeStruct((8, n), jnp.int32), jax.ShapeDtypeStruct((8, n), F32),
                   jax.ShapeDtypeStruct((8, n), jnp.int32), jax.ShapeDtypeStruct((N_EXPERTS, 128), F32)],
        scratch_shapes=[pltpu.VMEM((N_EXPERTS, 128), F32)],
        compiler_params=_cparams(("arbitrary",)),
        name="route",
    )(logits_t, tri)


def _expert_kernel(be_ref, nu_ref, x_ref, w1_ref, b1_ref, w2_ref, b2_ref, o_ref, w1b_ref, w2b_ref):
    i = pl.program_id(0)

    @pl.when(i < nu_ref[0])
    def _():
        @pl.when(jnp.logical_or(i == 0, be_ref[i] != be_ref[jnp.maximum(i - 1, 0)]))
        def _():
            w1b_ref[...] = w1_ref[0].astype(BF16)
            w2b_ref[...] = w2_ref[0].astype(BF16)

        h1 = _dot(_unpack_pairs(x_ref[...]).astype(BF16), w1b_ref[...]) + b1_ref[0]
        glu = jnp.minimum(h1[:, :EXPERT_FF], SWIGLU_LIMIT)
        lin = jnp.clip(h1[:, EXPERT_FF:], -SWIGLU_LIMIT, SWIGLU_LIMIT)
        act = glu * _sigmoid(SWIGLU_ALPHA * glu) * (lin + 1.0)
        o_ref[...] = _pack_pairs(_dot(act.astype(BF16), w2b_ref[...]) + b2_ref[0])

    @pl.when(pl.program_id(0) >= nu_ref[0])
    def _():
        o_ref[...] = jnp.zeros_like(o_ref)


def _experts(x_slots, block_expert, n_used, w1, b1, w2, b2, expert_offset):
    n_slots = x_slots.shape[0]
    tm = EXPERT_TILE
    n_blocks = n_slots // tm
    off = expert_offset
    grid_spec = pltpu.PrefetchScalarGridSpec(
        num_scalar_prefetch=2,
        grid=(n_blocks,),
        in_specs=[pl.BlockSpec((tm, D_MODEL // 2), lambda i, be, nu: (i, 0)),
                  pl.BlockSpec((1, D_MODEL, 2 * EXPERT_FF), lambda i, be, nu: (be[i] + off, 0, 0)),
                  pl.BlockSpec((1, 1, 2 * EXPERT_FF), lambda i, be, nu: (be[i], 0, 0)),
                  pl.BlockSpec((1, EXPERT_FF, D_MODEL), lambda i, be, nu: (be[i] + off, 0, 0)),
                  pl.BlockSpec((1, 1, D_MODEL), lambda i, be, nu: (be[i], 0, 0))],
        out_specs=pl.BlockSpec((tm, D_MODEL // 2), lambda i, be, nu: (i, 0)),
        scratch_shapes=[pltpu.VMEM((D_MODEL, 2 * EXPERT_FF), BF16), pltpu.VMEM((EXPERT_FF, D_MODEL), BF16)],
    )
    return pl.pallas_call(
        _expert_kernel,
        grid_spec=grid_spec,
        out_shape=jax.ShapeDtypeStruct((n_slots, D_MODEL // 2), jnp.uint32),
        compiler_params=_cparams(("arbitrary",)),
        name="experts",
    )(block_expert, n_used, x_slots, w1, b1.reshape(N_EXPERTS, 1, -1), w2, b2.reshape(N_EXPERTS, 1, -1))


COMBINE_TILE = 128


def _dispatch_kernel(dest_ref, h_ref, init_ref, xs_ref, sem, *, tm):
    del init_ref

    for r in range(tm):
        for k in range(TOP_K):
            pltpu.make_async_copy(h_ref.at[pl.ds(r, 1)], xs_ref.at[pl.ds(dest_ref[k, r], 1)], sem).start(
                priority=k % 2)
    for k in range(TOP_K):
        pltpu.make_async_copy(h_ref, xs_ref.at[pl.ds(0, tm)], sem).wait()


def _dispatch(h_packed, dest, n_slots, slots_init):
    n, w = h_packed.shape
    if slots_init is None:
        slots_init = jnp.zeros((n_slots, w), jnp.uint32)
    tm = min(COMBINE_TILE, n)
    return pl.pallas_call(
        functools.partial(_dispatch_kernel, tm=tm),
        grid=(n // tm,),
        in_specs=[pl.BlockSpec((TOP_K, tm), lambda i: (0, i), memory_space=pltpu.SMEM),
                  pl.BlockSpec((tm, w), lambda i: (i, 0)), pl.BlockSpec(memory_space=pl.ANY)],
        out_specs=pl.BlockSpec(memory_space=pl.ANY),
        out_shape=jax.ShapeDtypeStruct((n_slots, w), jnp.uint32),
        scratch_shapes=[pltpu.SemaphoreType.DMA(())],
        input_output_aliases={2: 0},
        compiler_params=_cparams(("arbitrary",)),
        name="dispatch",
    )(dest, h_packed, slots_init)


def _combine_kernel(d_ref, h_ref, w_ref, lnw_ref, lnb_ref, ys_ref, o_ref, buf_ref, sem, *, tm):
    i = pl.program_id(0)
    n_tiles = pl.num_programs(0) - 1

    def issue(slot):
        for r in range(tm):
            for k in range(TOP_K):
                pltpu.make_async_copy(ys_ref.at[pl.ds(d_ref[k, r], 1)], buf_ref.at[slot, k, pl.ds(r, 1)],
                                      sem.at[slot]).start(priority=k % 2)

    def finish(slot):
        for k in range(TOP_K):
            pltpu.make_async_copy(ys_ref.at[pl.ds(0, tm)], buf_ref.at[slot, k], sem.at[slot]).wait()
        acc = DEEPNORM_ALPHA * h_ref[...]
        for k in range(TOP_K):
            acc = acc + w_ref[:, k:k + 1] * _unpack_pairs(buf_ref[slot, k])
        o_ref[...] = _layer_norm(acc, lnw_ref[...], lnb_ref[...])

    for parity in range(2):
        @pl.when(jnp.logical_and(i < n_tiles, i % 2 == parity))
        def _():
            issue(parity)

        @pl.when(jnp.logical_and(i > 0, i % 2 == parity))
        def _():
            finish(1 - parity)


def _combine(h, y_slots, dest, w_tok, ln_w, ln_b):
    n = h.shape[0]
    tm = min(COMBINE_TILE, n)
    n_tiles = n // tm
    lnw = ln_w.reshape(1, -1)
    lnb = ln_b.reshape(1, -1)
    prev = lambda i: jnp.maximum(i - 1, 0)
    return pl.pallas_call(
        functools.partial(_combine_kernel, tm=tm),
        grid=(n_tiles + 1,),
        in_specs=[pl.BlockSpec((TOP_K, tm), lambda i: (0, jnp.minimum(i, n_tiles - 1)), memory_space=pltpu.SMEM),
                  pl.BlockSpec((tm, D_MODEL), lambda i: (prev(i), 0)),
                  pl.BlockSpec((tm, TOP_K), lambda i: (prev(i), 0)),
                  pl.BlockSpec(lnw.shape, lambda i: (0, 0)), pl.BlockSpec(lnb.shape, lambda i: (0, 0)),
                  pl.BlockSpec(memory_space=pl.ANY)],
        out_specs=pl.BlockSpec((tm, D_MODEL), lambda i: (prev(i), 0)),
        out_shape=jax.ShapeDtypeStruct((n, D_MODEL), F32),
        scratch_shapes=[pltpu.VMEM((2, TOP_K, tm, D_MODEL // 2), jnp.uint32), pltpu.SemaphoreType.DMA((2,))],
        compiler_params=_cparams(("arbitrary",)),
        name="combine",
    )(dest, h, w_tok, lnw, lnb, y_slots)


def _moe(h, h16, logits_t, w1, b1, w2, b2, ln_w, ln_b, expert_offset, slots_init):
    n = h.shape[0]
    tm = EXPERT_TILE
    idx8, w8, rank8, cnt = _route(logits_t)
    idx, w_top, rank = idx8[:TOP_K], w8[:TOP_K], rank8[:TOP_K]
    counts = cnt[:, 0].astype(jnp.int32)
    padded = ((counts + tm - 1) // tm) * tm
    p_end = jnp.cumsum(padded)
    p_start = p_end - padded
    experts = jnp.arange(N_EXPERTS, dtype=jnp.int32)
    seg_start = jnp.sum(jnp.where(idx[:, :, None] == experts, p_start, 0), axis=-1)
    dest = seg_start + rank
    n_slots = n * TOP_K + N_EXPERTS * tm
    n_blocks = n_slots // tm
    starts = jnp.arange(n_blocks, dtype=jnp.int32) * tm
    block_expert = jnp.minimum(jnp.sum((p_end[None, :] <= starts[:, None]).astype(jnp.int32), axis=1),
                               N_EXPERTS - 1)
    n_used = (p_end[-1] // tm).astype(jnp.int32).reshape(1)
    x_slots = _dispatch(h16, dest, n_slots, slots_init)
    y_slots = _experts(x_slots, block_expert, n_used, w1, b1, w2, b2, expert_offset)
    return _combine(h, y_slots, dest, w_top.T, ln_w, ln_b), x_slots


def _mixer_weight(w_in):
    o = np.cumsum((512, 512, 512, 512, MLA_Q_RANK, MLA_KV_RANK + MLA_ROPE, S5_WIDTH, LRU_WIDTH, LRU_WIDTH))
    hgrn, cq, ckv = w_in[:, :o[3]], w_in[:, o[3]:o[4]], w_in[:, o[4]:o[4] + MLA_KV_RANK]
    kpe = w_in[:, o[4] + MLA_KV_RANK:o[5]]
    su, lx, lg = w_in[:, o[5]:o[6]], w_in[:, o[6]:o[7]], w_in[:, o[7]:o[8]]
    pad = jnp.zeros((D_MODEL, 128 - MLA_ROPE), w_in.dtype)
    w_mix = jnp.concatenate([hgrn, su, lx, lg, cq, ckv, kpe, pad, _rot_half_cols(kpe), pad], axis=1)
    return w_mix, w_in[:, o[8]:]


def _layer(x, positions, lb, bsz, seq, w_in, b_gate, hgrn_norm_w, mla_q_norm_w, mla_wq_b, mla_kv_norm_w, mla_wkv_b,
           s5_a_re, s5_a_im, s5_log_dt, s5_b_re, s5_b_im, s5_c_re, s5_c_im, s5_d, s5_w_glu, s5_b_glu,
           lru_conv_w, lru_conv_b, lru_wa, lru_ba, lru_wx, lru_bx, lru_a_param,
           w_branch, w_out, ln1_w, ln1_b, ln2_w, ln2_b, router_w, router_b, moe_w1, moe_b1, moe_w2, moe_b2,
           expert_offset=0, slots_init=None):
    w_mix, w_gl = _mixer_weight(w_in)
    p = _matmul(x, w_mix.astype(BF16), 2048, MIX_WIDTH // 3, F32)
    gl = _matmul(x, w_gl.astype(BF16), 2048, 1024, BF16)
    y_a = _hgrn(p, lb, hgrn_norm_w, bsz, seq)
    qn, qp, kn, kp, vt = _mla_proj(p, positions, mla_q_norm_w, mla_wq_b, mla_kv_norm_w, mla_wkv_b, min(256, seq))
    y_b = _flash(qn, qp, kn, kp, vt, bsz, seq)
    y_s = _s5(p, s5_a_re, s5_a_im, s5_log_dt, s5_b_re, s5_b_im, s5_c_re, s5_c_im, s5_d, bsz, seq)
    y_d = _lru(p, lru_conv_w, lru_conv_b, lru_wa, lru_ba, lru_wx, lru_bx, lru_a_param, bsz, seq)
    h, h16, logits_t = _merge(y_a, y_b, y_s, y_d, gl, x, w_branch, w_out, s5_w_glu, s5_b_glu, b_gate,
                              ln1_w, ln1_b, router_w, router_b)
    return _moe(h, h16, logits_t, moe_w1, moe_b1, moe_w2, moe_b2, ln2_w, ln2_b, expert_offset, slots_init)


def kernel(x, positions, w_in, b_gate, hgrn_lb_logits, hgrn_norm_w, mla_q_norm_w, mla_wq_b, mla_kv_norm_w, mla_wkv_b, s5_a_re, s5_a_im, s5_log_dt, s5_b_re, s5_b_im, s5_c_re, s5_c_im, s5_d, s5_w_glu, s5_b_glu, lru_conv_w, lru_conv_b, lru_wa, lru_ba, lru_wx, lru_bx, lru_a_param, w_branch, w_out, ln1_w, ln1_b, ln2_w, ln2_b, router_w, router_b, moe_w1, moe_b1, moe_w2, moe_b2):
    bsz, seq, _ = x.shape
    probs = jax.nn.softmax(hgrn_lb_logits.astype(F32), axis=0)
    lower_bounds = jnp.cumsum(probs, axis=0) - probs[0:1]
    per_layer = (w_in, b_gate, hgrn_norm_w, mla_q_norm_w, mla_wq_b, mla_kv_norm_w, mla_wkv_b,
                 s5_a_re, s5_a_im, s5_log_dt, s5_b_re, s5_b_im, s5_c_re, s5_c_im, s5_d, s5_w_glu, s5_b_glu,
                 lru_conv_w, lru_conv_b, lru_wa, lru_ba, lru_wx, lru_bx, lru_a_param,
                 w_branch, w_out, ln1_w, ln1_b, ln2_w, ln2_b, router_w, router_b, moe_w1, moe_b1, moe_w2, moe_b2)
    xf = x.reshape(bsz * seq, D_MODEL)
    w1_all = moe_w1.reshape((DEPTH * N_EXPERTS,) + moe_w1.shape[2:])
    w2_all = moe_w2.reshape((DEPTH * N_EXPERTS,) + moe_w2.shape[2:])
    slots = None
    for l in range(DEPTH):
        args = [a[l] for a in per_layer]
        args[-4], args[-2] = w1_all, w2_all
        xf, slots = _layer(xf, positions, lower_bounds[l], bsz, seq, *args, expert_offset=l * N_EXPERTS,
                           slots_init=slots)
    return xf.reshape(bsz, seq, D_MODEL)
```

```python
import functools
import math

import numpy as np
import jax
import jax.numpy as jnp
from jax import lax
from jax.experimental import pallas as pl
from jax.experimental.pallas import tpu as pltpu

F32 = jnp.float32
BF16 = jnp.bfloat16

D_MODEL = 1024
DEPTH = 2
CHUNK = 64

HGRN_HEADS = 4
HGRN_DK = 128
HGRN_WIDTH = 512
HGRN_F_MIN = 1e-30
HGRN_CHUNK = 64
HGRN_LEVELS = (1, 2, 4, 8, 16, 32)
HGRN_PAR = 4

MLA_HEADS = 4
MLA_Q_RANK = 256
MLA_KV_RANK = 128
MLA_NOPE = 128
MLA_ROPE = 64
MLA_V = 128
MLA_QK = MLA_NOPE + MLA_ROPE
MLA_VROWS = MLA_V + 16
ROPE_THETA = 10000.0

S5_GROUPS = 32
S5_GROUP_CH = 16
S5_STATE = 64
S5_WIDTH = 512
S5_BLOCK = 16
S5_PAR = 8

LRU_WIDTH = 512
LRU_BLOCKS = 8
LRU_BLOCK_W = 64
LRU_CONV = 4
LRU_C = 8.0

N_BRANCH = 4
BRANCH_WIDTH = 512

N_EXPERTS = 32
TOP_K = 4
EXPERT_FF = 1024
SWIGLU_ALPHA = 1.702
SWIGLU_LIMIT = 7.0
EXPERT_TILE = 1024

DEEPNORM_ALPHA = (2.0 * DEPTH) ** 0.25
LN_EPS = 1e-5
RMS_EPS = 1e-6

COL_HQ, COL_HF, COL_HI, COL_HG = 0, 512, 1024, 1536
COL_SU, COL_LX, COL_LG = 2048, 2560, 3072
COL_CQ, COL_CKV, COL_KPE, COL_KPER = 3584, 3840, 3968, 4096
MIX_WIDTH = 4224

VMEM_LIMIT = 56 * 1024 * 1024


def _cparams(sem):
    return pltpu.CompilerParams(dimension_semantics=sem, vmem_limit_bytes=VMEM_LIMIT)


def _dot(a, b):
    return jnp.dot(a, b, preferred_element_type=F32)


def _dot_nt(a, b):
    return lax.dot_general(a, b, (((1,), (1,)), ((), ())), preferred_element_type=F32)


def _dot_tn(a, b):
    return lax.dot_general(a, b, (((0,), (0,)), ((), ())), preferred_element_type=F32)


def _split(x):
    hi = x.astype(BF16)
    lo = (x - hi.astype(F32)).astype(BF16)
    return hi, lo


def _dot3(a, b_hi, b_lo):
    a_hi, a_lo = _split(a)
    return _dot(a_hi, b_hi) + (_dot(a_lo, b_hi) + _dot(a_hi, b_lo))


def _pack_pairs(x):
    w = x.shape[-1] // 2
    lo = lax.bitcast_convert_type(x[:, :w].astype(BF16).astype(F32), jnp.uint32)
    hi = lax.bitcast_convert_type(x[:, w:].astype(BF16).astype(F32), jnp.uint32)
    return lax.shift_right_logical(lo, jnp.uint32(16)) | (hi & jnp.uint32(0xFFFF0000))


def _unpack_pairs(u):
    lo = lax.bitcast_convert_type(lax.shift_left(u, jnp.uint32(16)), F32)
    hi = lax.bitcast_convert_type(u & jnp.uint32(0xFFFF0000), F32)
    return jnp.concatenate([lo, hi], axis=-1)


def _sigmoid(x):
    return 0.5 * jnp.tanh(0.5 * x) + 0.5


def _silu(x):
    return x * _sigmoid(x)


def _shift_rows(x, k, fill):
    rows = lax.broadcasted_iota(jnp.int32, x.shape, 0)
    return jnp.where(rows >= k, pltpu.roll(x, k, 0), fill)


def _mm_kernel(x_ref, w_ref, o_ref):
    o_ref[...] = _dot(x_ref[...].astype(BF16), w_ref[...]).astype(o_ref.dtype)


def _matmul(x, w, tm, tn, out_dtype):
    n, k = x.shape
    m = w.shape[1]
    tm = min(tm, n)
    return pl.pallas_call(
        _mm_kernel,
        grid=(n // tm, m // tn),
        in_specs=[pl.BlockSpec((tm, k), lambda i, j: (i, 0)),
                  pl.BlockSpec((k, tn), lambda i, j: (0, j))],
        out_specs=pl.BlockSpec((tm, tn), lambda i, j: (i, j)),
        out_shape=jax.ShapeDtypeStruct((n, m), out_dtype),
        compiler_params=_cparams(("parallel", "arbitrary")),
        name="in_proj",
    )(x, w)


def _hgrn_tables():
    c = HGRN_CHUNK
    t = np.arange(c)
    windows = [np.tril(np.ones((c, c), bool)),
               np.triu(np.ones((c, c), bool), 1)]
    masks = [np.eye(c, dtype=bool)]
    for h in HGRN_LEVELS:
        blk = t // h
        odd = blk % 2 == 1
        masks.append(odd[:, None] & (blk[None, :] == blk[:, None] - 1))
        if h > 1:
            windows.append((t[None, :] >= (h * blk)[:, None]) & (t[None, :] <= t[:, None]))
            windows.append((t[None, :] > t[:, None]) & (t[None, :] <= (h * blk + h - 1)[:, None]))
    return np.concatenate(windows, 0).astype(np.float32), np.stack(masks).astype(np.float32)


def _hgrn_kernel(hq_ref, hf_ref, hi_ref, hg_ref, lb_ref, nw_ref, win_ref, mask_ref,
                 o_ref, state_ref, *, n_chunks):
    c = HGRN_CHUNK
    heads = range(HGRN_HEADS)
    hs = [slice(h * HGRN_DK, (h + 1) * HGRN_DK) for h in heads]
    n_lvl = len(HGRN_LEVELS) + 1

    @pl.when(pl.program_id(1) == 0)
    def _():
        state_ref[...] = jnp.zeros_like(state_ref)

    def group_body(gi, carry):
        par = range(HGRN_PAR)
        rows = [pl.ds(pl.multiple_of((gi * HGRN_PAR + cc) * c, c), c) for cc in par]
        lb = lb_ref[...]
        z = [hf_ref[rows[cc], :] for cc in par]
        th = [jnp.tanh(0.5 * z[cc]) for cc in par]
        f = [jnp.maximum(lb + (1.0 - lb) * (0.5 + 0.5 * th[cc]), HGRN_F_MIN) for cc in par]
        k = [(1.0 - lb) * (0.5 - 0.5 * th[cc]) for cc in par]
        q = [_silu(hq_ref[rows[cc], :]) for cc in par]
        v16 = [hi_ref[rows[cc], :].astype(BF16) for cc in par]
        lf = [_split(jnp.log(f[cc])) for cc in par]
        lf_stack = jnp.concatenate([jnp.concatenate([lf[cc][part] for cc in par], axis=-1) for part in range(2)],
                                   axis=0)
        e_all = jnp.exp(_dot(win_ref[...], lf_stack))
        e = [e_all[:, cc * HGRN_WIDTH:(cc + 1) * HGRN_WIDTH] for cc in par]
        blk = lambda cc, w: e[cc][w * c:(w + 1) * c]
        qs, ks, q_in, k_out, decay = [], [], [], [], []
        for cc in par:
            k16 = k[cc].astype(BF16)
            qs.append([q[cc].astype(BF16), (q[cc] * f[cc]).astype(BF16)]
                      + [(q[cc] * blk(cc, 2 * w)).astype(BF16) for w in range(1, n_lvl - 1)])
            ks.append([k16, k16] + [(k[cc] * blk(cc, 2 * w + 1)).astype(BF16) for w in range(1, n_lvl - 1)])
            q_in.append((q[cc] * blk(cc, 0)).astype(BF16))
            k_out.append((k[cc] * blk(cc, 1)).astype(BF16))
            decay.append(e[cc][c - 1:c, :])
        scores = [[None] * HGRN_HEADS for _ in par]
        for lv in range(n_lvl):
            s_l = [[_dot_nt(qs[cc][lv][:, hs[h]], ks[cc][lv][:, hs[h]]) for h in heads] for cc in par]
            for cc in par:
                for h in heads:
                    term = mask_ref[lv] * s_l[cc][h]
                    scores[cc][h] = term if scores[cc][h] is None else scores[cc][h] + term
        intra = [[_dot(scores[cc][h].astype(BF16), v16[cc][:, hs[h]]) for h in heads] for cc in par]
        upd = [[_dot_tn(v16[cc][:, hs[h]], k_out[cc][:, hs[h]]) for h in heads] for cc in par]
        st = [state_ref[h] for h in heads]
        for cc in par:
            o = [intra[cc][h] + _dot_nt(q_in[cc][:, hs[h]], st[h].astype(BF16)) for h in heads]
            st = [st[h] * decay[cc][:, hs[h]] + upd[cc][h] for h in heads]
            o = [o[h] * lax.rsqrt(jnp.mean(o[h] * o[h], axis=-1, keepdims=True) + RMS_EPS) for h in heads]
            o_ref[rows[cc], :] = jnp.concatenate(o, axis=-1) * nw_ref[...] * _silu(hg_ref[rows[cc], :])
        for h in heads:
            state_ref[h] = st[h]
        return carry

    lax.fori_loop(0, n_chunks // HGRN_PAR, group_body, 0)


def _hgrn(p, lb, norm_w, bsz, seq):
    t = min(256, seq)
    nt = seq // t
    win, mask = _hgrn_tables()
    col = lambda cb: pl.BlockSpec((t, HGRN_WIDTH), lambda b, i: (b * nt + i, cb))
    const2 = lambda shape: pl.BlockSpec(shape, lambda b, i: (0,) * len(shape))
    return pl.pallas_call(
        functools.partial(_hgrn_kernel, n_chunks=t // HGRN_CHUNK),
        grid=(bsz, nt),
        in_specs=[col(COL_HQ // 512), col(COL_HF // 512), col(COL_HI // 512), col(COL_HG // 512),
                  const2((1, HGRN_WIDTH)), const2((1, HGRN_WIDTH)),
                  const2((win.shape[0], 2 * win.shape[1])), const2(mask.shape)],
        out_specs=pl.BlockSpec((t, HGRN_WIDTH), lambda b, i: (b * nt + i, 0)),
        out_shape=jax.ShapeDtypeStruct((bsz * seq, HGRN_WIDTH), F32),
        scratch_shapes=[pltpu.VMEM((HGRN_HEADS, HGRN_DK, HGRN_DK), F32)],
        compiler_params=_cparams(("parallel", "arbitrary")),
        name="hgrn",
    )(p, p, p, p, lb.reshape(1, -1), norm_w.reshape(1, -1), jnp.asarray(np.concatenate([win, win], 1), BF16),
      jnp.asarray(mask))


def _rms(x, w):
    ms = jnp.mean(x * x, axis=-1, keepdims=True)
    return x * lax.rsqrt(ms + RMS_EPS) * w


def _mla_proj_kernel(cq_ref, ckv_ref, kpe_ref, kper_ref, ang_ref, qnw_ref, kvnw_ref,
                     wqn_ref, wqp_ref, wqpr_ref, wkn_ref, wv_ref,
                     qn_ref, qp_ref, kn_ref, kp_ref, v_ref):
    ang = ang_ref[...]
    cos = jnp.cos(ang)
    sin = jnp.sin(ang)
    qn = _rms(cq_ref[...], qnw_ref[...]).astype(BF16)
    cos4 = jnp.concatenate([cos] * MLA_HEADS, axis=-1)
    sin4 = jnp.concatenate([sin] * MLA_HEADS, axis=-1)
    qn_ref[...] = _dot(qn, wqn_ref[...]).astype(BF16)
    qp_ref[...] = (_dot(qn, wqp_ref[...]) * cos4 + _dot(qn, wqpr_ref[...]) * sin4).astype(BF16)
    cn = _rms(ckv_ref[...], kvnw_ref[...]).astype(BF16)
    kn_ref[...] = _dot(cn, wkn_ref[...]).astype(BF16)
    vt = _dot_nt(wv_ref[...], cn)
    ones = jnp.ones((MLA_VROWS - MLA_V, vt.shape[1]), F32)
    v_ref[0] = jnp.concatenate(
        [piece for h in range(MLA_HEADS) for piece in (vt[h * MLA_V:(h + 1) * MLA_V], ones)], axis=0).astype(BF16)
    kp_ref[...] = (kpe_ref[...] * cos + kper_ref[...] * sin).astype(BF16)


def _rot_half_cols(w):
    half = w.shape[-1] // 2
    return jnp.concatenate([-w[..., half:], w[..., :half]], axis=-1)


def _mla_proj(p, positions, q_norm_w, wq_b, kv_norm_w, wkv_b, tm):
    n = p.shape[0]
    scale = MLA_QK ** -0.5 * math.log2(math.e)
    wq = wq_b.reshape(MLA_Q_RANK, MLA_HEADS, MLA_QK) * scale
    w_nope = wq[:, :, :MLA_NOPE].reshape(MLA_Q_RANK, MLA_HEADS * MLA_NOPE)
    w_pe = wq[:, :, MLA_NOPE:]
    pad = jnp.zeros((MLA_Q_RANK, MLA_HEADS, 128 - MLA_ROPE), F32)
    w_pe_p = jnp.concatenate([w_pe, pad], axis=-1).reshape(MLA_Q_RANK, MLA_HEADS * 128)
    w_per_p = jnp.concatenate([_rot_half_cols(w_pe), pad], axis=-1).reshape(MLA_Q_RANK, MLA_HEADS * 128)
    wkv = wkv_b.reshape(MLA_KV_RANK, MLA_HEADS, MLA_NOPE + MLA_V)
    w_kn = wkv[:, :, :MLA_NOPE].reshape(MLA_KV_RANK, MLA_HEADS * MLA_NOPE)
    w_v = wkv[:, :, MLA_NOPE:].reshape(MLA_KV_RANK, MLA_HEADS * MLA_V)
    inv_freq = ROPE_THETA ** (-jnp.arange(0, MLA_ROPE, 2, dtype=F32) / MLA_ROPE)
    freq_lane = jnp.concatenate([inv_freq, inv_freq, jnp.zeros((128 - MLA_ROPE,), F32)])
    ang = positions.reshape(n, 1).astype(F32) * freq_lane[None, :]

    row = lambda w, cb: pl.BlockSpec((tm, w), lambda i: (i, cb))
    full = lambda a: pl.BlockSpec(a.shape, lambda i: (0,) * a.ndim)
    weights = [w_nope.astype(BF16), w_pe_p.astype(BF16), w_per_p.astype(BF16), w_kn.astype(BF16),
               w_v.T.astype(BF16)]
    qnw = q_norm_w.reshape(1, -1)
    kvnw = kv_norm_w.reshape(1, -1)
    outs = pl.pallas_call(
        _mla_proj_kernel,
        grid=(n // tm,),
        in_specs=[row(256, COL_CQ // 256), row(128, COL_CKV // 128), row(128, COL_KPE // 128),
                  row(128, COL_KPER // 128), row(128, 0), full(qnw), full(kvnw)] + [full(w) for w in weights],
        out_specs=[row(512, 0), row(512, 0), row(512, 0), row(128, 0),
                   pl.BlockSpec((1, MLA_HEADS * MLA_VROWS, tm), lambda i: (i, 0, 0))],
        out_shape=[jax.ShapeDtypeStruct((n, 512), BF16), jax.ShapeDtypeStruct((n, 512), BF16),
                   jax.ShapeDtypeStruct((n, 512), BF16), jax.ShapeDtypeStruct((n, 128), BF16),
                   jax.ShapeDtypeStruct((n // tm, MLA_HEADS * MLA_VROWS, tm), BF16)],
        compiler_params=_cparams(("parallel",)),
        name="mla_proj",
    )(p, p, p, p, ang, qnw, kvnw, *weights)
    return outs


def _flash_kernel(qn_ref, qp_ref, kn_ref, kp_ref, vt_ref, o_ref, acc_ref, *, tq):
    i = pl.program_id(1)
    heads = range(MLA_HEADS)
    hs = [slice(h * 128, (h + 1) * 128) for h in heads]
    vs = [slice(h * MLA_VROWS, (h + 1) * MLA_VROWS) for h in heads]
    acc_ref[...] = jnp.zeros_like(acc_ref)

    def scores(j):
        r0 = pl.multiple_of(j * tq, tq)
        kp = kp_ref[pl.ds(r0, tq), :]
        return tuple(_dot_nt(jnp.concatenate([kn_ref[pl.ds(r0, tq), hs[h]], kp], axis=-1),
                             jnp.concatenate([qn_ref[:, hs[h]], qp_ref[:, hs[h]]], axis=-1)) for h in heads)

    def consume(j, s, m, mask):
        p, m_new, alpha = [], [], []
        for h in heads:
            sh = s[h] if mask is None else jnp.where(mask, s[h], -1e30)
            mn = jnp.maximum(m[h], jnp.max(sh, axis=0, keepdims=True))
            alpha.append(jnp.exp2(m[h] - mn))
            p.append(jnp.exp2(sh - mn).astype(BF16))
            m_new.append(mn)
        vt = vt_ref[j]
        pv = [_dot(vt[vs[h], :], p[h]) for h in heads]
        for h in heads:
            acc_ref[h] = acc_ref[h] * alpha[h] + pv[h]
        return tuple(m_new)

    def body(j, carry):
        s, m = carry
        s_next = scores(j + 1)
        return s_next, consume(j, s, m, None)

    m0 = tuple(jnp.full((1, tq), -1e30, F32) for _ in heads)
    s, m = lax.fori_loop(0, i, body, (scores(0), m0))
    shift = CHUNK.bit_length() - 1
    kc = lax.shift_right_logical(lax.broadcasted_iota(jnp.int32, (tq, tq), 0), shift)
    qc = lax.shift_right_logical(lax.broadcasted_iota(jnp.int32, (tq, tq), 1), shift)
    consume(i, s, m, kc <= qc)
    for h in heads:
        acc = acc_ref[h]
        o_ref[:, hs[h]] = (acc[:MLA_V] / acc[MLA_V:MLA_V + 1]).T.astype(o_ref.dtype)


def _flash(qn, qp, kn, kp, vt, bsz, seq):
    tq = vt.shape[2]
    nq = seq // tq
    width = MLA_HEADS * 128
    qspec = pl.BlockSpec((tq, width), lambda b, i: (b * nq + i, 0))
    kspec = pl.BlockSpec((seq, width), lambda b, i: (b, 0))
    kpspec = pl.BlockSpec((seq, 128), lambda b, i: (b, 0))
    vspec = pl.BlockSpec((nq, MLA_HEADS * MLA_VROWS, tq), lambda b, i: (b, 0, 0))
    return pl.pallas_call(
        functools.partial(_flash_kernel, tq=tq),
        grid=(bsz, nq),
        in_specs=[qspec, qspec, kspec, kpspec, vspec],
        out_specs=qspec,
        out_shape=jax.ShapeDtypeStruct((bsz * seq, width), BF16),
        scratch_shapes=[pltpu.VMEM((MLA_HEADS, MLA_VROWS, tq), F32)],
        compiler_params=_cparams(("parallel", "arbitrary")),
        name="flash",
    )(qn, qp, kn, kp, vt)


def _cmul(a, b):
    return a[0] * b[0] - a[1] * b[1], a[0] * b[1] + a[1] * b[0]


def _s5_tables(a_re, a_im, log_dt, b_re, b_im, c_re, c_im, d_skip, steps):
    lb = S5_BLOCK
    f32 = lambda v: v.astype(F32)
    a_re, a_im, b_re, b_im, c_re, c_im = map(f32, (a_re, a_im, b_re, b_im, c_re, c_im))
    dt = jnp.exp(f32(log_dt))[:, None]
    ld = (a_re * dt, a_im * dt)

    def power(tau):
        mag = jnp.exp(ld[0] * tau)
        return mag * jnp.cos(ld[1] * tau), mag * jnp.sin(ld[1] * tau)

    lam_bar = power(1.0)
    inv = a_re * a_re + a_im * a_im
    ratio = _cmul((lam_bar[0] - 1.0, lam_bar[1]), (a_re / inv, -a_im / inv))
    b_bar = _cmul((ratio[0][..., None], ratio[1][..., None]), (b_re, b_im))
    tau = jnp.arange(lb + 1, dtype=F32)[:, None, None]
    pw = power(tau)
    cp = _cmul((c_re[None], c_im[None]), (pw[0][:, :, None, :], pw[1][:, :, None, :]))
    kern = jnp.einsum('tgcn,gnd->tgcd', cp[0][:lb], b_bar[0]) - jnp.einsum('tgcn,gnd->tgcd', cp[1][:lb], b_bar[1])
    s_idx = np.arange(lb)[:, None]
    t_idx = np.arange(lb)[None, :]
    lag = np.clip(t_idx - s_idx, 0, lb - 1)
    causal = jnp.asarray((t_idx >= s_idx).astype(np.float32))
    toe = kern[lag] * causal[:, :, None, None, None]
    skip = jnp.asarray(np.eye(lb, dtype=np.float32))[:, :, None, None, None] * (
        d_skip.astype(F32)[None, None, :, :, None] * jnp.eye(S5_GROUP_CH, dtype=F32)[None, None, None])
    toe = (toe + skip).transpose(2, 1, 3, 0, 4).reshape(S5_GROUPS, lb * S5_GROUP_CH, lb * S5_GROUP_CH)
    emit = jnp.concatenate([cp[0][1:], -cp[1][1:]], axis=-1)
    emit = emit.transpose(1, 0, 2, 3).reshape(S5_GROUPS, lb * S5_GROUP_CH, 2 * S5_STATE)
    rev = (pw[0][:lb][::-1][..., None], pw[1][:lb][::-1][..., None])
    bp = _cmul(rev, (b_bar[0][None], b_bar[1][None]))
    fold = jnp.concatenate([bp[0], bp[1]], axis=2)
    fold = fold.transpose(1, 2, 0, 3).reshape(S5_GROUPS, 2 * S5_STATE, lb * S5_GROUP_CH)
    jump = (lb * 2.0 ** jnp.arange(steps, dtype=F32))[:, None, None]
    pj = power(jump)
    pw_re = jnp.concatenate([pj[0], pj[0]], axis=-1).transpose(1, 2, 0)
    pw_im = jnp.concatenate([-pj[1], pj[1]], axis=-1).transpose(1, 2, 0)
    return toe, emit, fold, pw_re, pw_im


def _s5_kernel(su0_ref, su1_ref, su2_ref, su3_ref, toe_ref, emit_ref, fold_ref, pre_ref, pim_ref, y_ref,
               ut_ref, carry_ref, *, r, steps):
    lb = S5_BLOCK
    gc = S5_GROUP_CH

    @pl.when(pl.program_id(1) == 0)
    def _():
        carry_ref[...] = jnp.zeros_like(carry_ref)

    for s in range(lb):
        for c, su_ref in enumerate((su0_ref, su1_ref, su2_ref, su3_ref)):
            ut_ref[s, c * 128:(c + 1) * 128, :] = su_ref[pl.ds(s, r, stride=lb), :].T
    lane = lax.broadcasted_iota(jnp.int32, (2 * S5_STATE, r), 1)
    swap = lambda v: jnp.concatenate([v[S5_STATE:], v[:S5_STATE]], axis=0)

    def groups(gi, carry):
        par = range(S5_PAR)
        g = [gi * S5_PAR + a for a in par]
        g0 = [pl.multiple_of(g[a] * gc, gc) for a in par]
        ug = [jnp.concatenate([ut_ref[s, pl.ds(g0[a], gc), :] for s in range(lb)], axis=0).astype(BF16)
              for a in par]
        pre = [pre_ref[g[a]] for a in par]
        pim = [pim_ref[g[a]] for a in par]
        cmul = lambda a, j, v: pre[a][:, j:j + 1] * v + pim[a][:, j:j + 1] * swap(v)
        car = [carry_ref[g[a]] for a in par]
        x = [_dot(fold_ref[g[a]], ug[a]) for a in par]
        x = [x[a] + jnp.where(lane == 0, cmul(a, 0, car[a]), 0.0) for a in par]
        for j in range(steps):
            xs = [jnp.where(lane >= 2 ** j, pltpu.roll(x[a], 2 ** j, 1), 0.0) for a in par]
            x = [x[a] + cmul(a, j, xs[a]) for a in par]
        x_prev = [jnp.where(lane == 0, car[a], pltpu.roll(x[a], 1, 1)) for a in par]
        y = [_dot(toe_ref[g[a]], ug[a]) + _dot(emit_ref[g[a]], x_prev[a].astype(BF16)) for a in par]
        for a in par:
            carry_ref[g[a]] = jnp.broadcast_to(x[a][:, r - 1:r], x[a].shape)
            for t in range(lb):
                ut_ref[t, pl.ds(g0[a], gc), :] = y[a][t * gc:(t + 1) * gc]
        return carry

    lax.fori_loop(0, S5_GROUPS // S5_PAR, groups, 0)
    for t in range(lb):
        for c in range(S5_WIDTH // 128):
            y_ref[c, pl.ds(t, r, stride=lb), :] = ut_ref[t, c * 128:(c + 1) * 128, :].T


def _s5(p, a_re, a_im, log_dt, b_re, b_im, c_re, c_im, d_skip, bsz, seq):
    lb = S5_BLOCK
    r = min(128, seq // lb)
    nt = seq // (lb * r)
    ncb = S5_WIDTH // 128
    steps = max(1, int(math.ceil(math.log2(r))))
    toe, emit, fold, pw_re, pw_im = _s5_tables(a_re, a_im, log_dt, b_re, b_im, c_re, c_im, d_skip, steps)
    consts = [toe.astype(BF16), emit.astype(BF16), fold.astype(BF16), pw_re, pw_im]
    full = lambda a: pl.BlockSpec(a.shape, lambda b, t: (0,) * a.ndim)
    return pl.pallas_call(
        functools.partial(_s5_kernel, r=r, steps=steps),
        grid=(bsz, nt),
        in_specs=[pl.BlockSpec((lb * r, 128), functools.partial(lambda b, t, c: (b * nt + t, COL_SU // 128 + c), c=c))
                  for c in range(ncb)] + [full(c) for c in consts],
        out_specs=pl.BlockSpec((ncb, lb * r, 128), lambda b, t: (0, b * nt + t, 0)),
        out_shape=jax.ShapeDtypeStruct((ncb, bsz * seq, 128), F32),
        scratch_shapes=[pltpu.VMEM((lb, S5_WIDTH, r), F32), pltpu.VMEM((S5_GROUPS, 2 * S5_STATE, r), F32)],
        compiler_params=_cparams(("parallel", "arbitrary")),
        name="s5",
    )(*([p] * ncb), *consts)


def _lru_kernel(lx_ref, lg_ref, cw_ref, cb_ref, wg_ref, bg_ref, sp_ref, o_ref, tail_ref, h_ref, *, ts):
    @pl.when(pl.program_id(1) == 0)
    def _():
        tail_ref[...] = jnp.zeros_like(tail_ref)
        h_ref[...] = jnp.zeros_like(h_ref)

    x = lx_ref[...]
    tail_ref[8:, :] = x
    xc = cb_ref[...] + cw_ref[LRU_CONV - 1:LRU_CONV, :] * x
    for j in range(1, LRU_CONV):
        xc = xc + cw_ref[LRU_CONV - 1 - j:LRU_CONV - j, :] * tail_ref[8 - j:8 - j + ts, :]
    tail_ref[0:8, :] = x[ts - 8:, :]
    gates = _sigmoid(_dot3(xc, wg_ref[0], wg_ref[1]) + bg_ref[...])
    r = gates[:, :LRU_WIDTH]
    ig = gates[:, LRU_WIDTH:]
    log_a = -LRU_C * r * sp_ref[...]
    a = jnp.exp(log_a)
    gap = jnp.maximum(1.0 - a * a, 0.0)
    b = jnp.where(gap > 0.0, gap * lax.rsqrt(gap), 0.0) * (ig * xc)
    sub = lax.broadcasted_iota(jnp.int32, a.shape, 0) & 7
    for k in (1, 2, 4):
        inside = sub >= k
        b = b + a * jnp.where(inside, pltpu.roll(b, k, 0), 0.0)
        a = a * jnp.where(inside, pltpu.roll(a, k, 0), 1.0)
    gate = jax.nn.gelu(lg_ref[...])
    h_in = h_ref[0:1, :]
    for g in range(ts // 8):
        rows = slice(8 * g, 8 * g + 8)
        h = b[rows] + a[rows] * h_in
        o_ref[rows, :] = h * gate[rows]
        h_in = h[7:8, :]
    h_ref[0:1, :] = h_in


def _block_diag(w):
    nb, bw, _ = w.shape
    eye = jnp.eye(nb, dtype=w.dtype)
    return (eye[:, None, :, None] * w[:, :, None, :]).reshape(nb * bw, nb * bw)


def _lru(p, conv_w, conv_b, wa, ba, wx, bx, a_param, bsz, seq):
    ts = min(512, seq)
    nt = seq // ts
    wg = jnp.concatenate([_block_diag(wa), _block_diag(wx)], axis=1)
    wg_hi = wg.astype(BF16)
    wg2 = jnp.stack([wg_hi, (wg - wg_hi.astype(F32)).astype(BF16)])
    bg = jnp.concatenate([ba, bx]).reshape(1, -1)
    sp = jax.nn.softplus(a_param.astype(F32)).reshape(1, -1)
    col = lambda cb: pl.BlockSpec((ts, LRU_WIDTH), lambda b, t: (b * nt + t, cb))
    full = lambda a: pl.BlockSpec(a.shape, lambda b, t: (0,) * a.ndim)
    cb2 = conv_b.reshape(1, -1)
    return pl.pallas_call(
        functools.partial(_lru_kernel, ts=ts),
        grid=(bsz, nt),
        in_specs=[col(COL_LX // 512), col(COL_LG // 512), full(conv_w), full(cb2), full(wg2), full(bg), full(sp)],
        out_specs=pl.BlockSpec((ts, LRU_WIDTH), lambda b, t: (b * nt + t, 0)),
        out_shape=jax.ShapeDtypeStruct((bsz * seq, LRU_WIDTH), F32),
        scratch_shapes=[pltpu.VMEM((8 + ts, LRU_WIDTH), F32), pltpu.VMEM((8, LRU_WIDTH), F32)],
        compiler_params=_cparams(("parallel", "arbitrary")),
        name="rglru",
    )(p, p, conv_w, cb2, wg2, bg, sp)


def _layer_norm(x, w, b):
    xc = x - jnp.mean(x, axis=-1, keepdims=True)
    var = jnp.mean(xc * xc, axis=-1, keepdims=True)
    return xc * lax.rsqrt(var + LN_EPS) * w + b


def _merge_kernel(ya_ref, yb_ref, ys_ref, yd_ref, gl_ref, x_ref, wbr_ref, wout_ref, wglu_ref, bglu_ref,
                  bgate_ref, lnw_ref, lnb_ref, rw_ref, rb_ref, h_ref, hb_ref, lg_ref):
    yc = jax.nn.gelu(jnp.concatenate([ys_ref[c] for c in range(S5_WIDTH // 128)], axis=-1))
    yc = yc * _sigmoid(_dot(yc.astype(BF16), wglu_ref[...]) + bglu_ref[...])
    branches = (ya_ref[...].astype(BF16), yb_ref[...], yc.astype(BF16), yd_ref[...].astype(BF16))
    mix = None
    for bi, yb in enumerate(branches):
        cs = slice(bi * D_MODEL, (bi + 1) * D_MODEL)
        gate = _sigmoid(gl_ref[:, cs].astype(F32) + bgate_ref[:, cs])
        term = gate * _dot(yb, wbr_ref[bi])
        mix = term if mix is None else mix + term
    mo = _dot(mix.astype(BF16), wout_ref[...])
    h = _layer_norm(DEEPNORM_ALPHA * x_ref[...] + mo, lnw_ref[...], lnb_ref[...])
    h_ref[...] = h
    h_hi, h_lo = _split(h)
    hb_ref[...] = _pack_pairs(h)
    hcat = jnp.concatenate([h_hi, h_lo, h_hi], axis=-1)
    lg_ref[...] = _dot_nt(rw_ref[...], hcat) + rb_ref[...]


def _merge(ya, yb, ys, yd, gl, x, w_branch, w_out, w_glu, b_glu, b_gate, ln_w, ln_b, router_w, router_b):
    n = x.shape[0]
    tm = min(512, n)
    rwt = router_w.T.astype(F32)
    rw_hi = rwt.astype(BF16)
    rw_lo = (rwt - rw_hi.astype(F32)).astype(BF16)
    rw3 = jnp.concatenate([rw_hi, rw_hi, rw_lo], axis=1)
    row = lambda w: pl.BlockSpec((tm, w), lambda i: (i, 0))
    full = lambda a: pl.BlockSpec(a.shape, lambda i: (0,) * a.ndim)
    consts = [w_branch.astype(BF16), w_out.astype(BF16), w_glu.astype(BF16), b_glu.reshape(1, -1),
              b_gate.reshape(1, -1), ln_w.reshape(1, -1), ln_b.reshape(1, -1), rw3, router_b.reshape(-1, 1)]
    return pl.pallas_call(
        _merge_kernel,
        grid=(n // tm,),
        in_specs=[row(512), row(512), pl.BlockSpec((S5_WIDTH // 128, tm, 128), lambda i: (0, i, 0)), row(512),
                  row(N_BRANCH * D_MODEL), row(D_MODEL)]
                 + [full(c) for c in consts],
        out_specs=[row(D_MODEL), row(D_MODEL // 2), pl.BlockSpec((N_EXPERTS, tm), lambda i: (0, i))],
        out_shape=[jax.ShapeDtypeStruct((n, D_MODEL), F32), jax.ShapeDtypeStruct((n, D_MODEL // 2), jnp.uint32),
                   jax.ShapeDtypeStruct((N_EXPERTS, n), F32)],
        compiler_params=_cparams(("parallel",)),
        name="merge",
    )(ya, yb, ys, yd, gl, x, *consts)


def _route_kernel(lg_ref, tri_ref, idx_ref, w_ref, rank_ref, cnt_ref, carry_ref, *, tr):
    @pl.when(pl.program_id(0) == 0)
    def _():
        carry_ref[...] = jnp.zeros_like(carry_ref)

    l = lg_ref[...]
    eidx = lax.broadcasted_iota(jnp.int32, l.shape, 0)
    vals, hots = [], []
    for _ in range(TOP_K):
        m = jnp.max(l, axis=0, keepdims=True)
        first = jnp.min(jnp.where(l == m, eidx, N_EXPERTS), axis=0, keepdims=True)
        hot = eidx == first
        l = jnp.where(hot, -jnp.inf, l)
        vals.append(m)
        hots.append(hot)
        idx_ref[len(vals) - 1:len(vals), :] = first
    ex = [jnp.exp(v - vals[0]) for v in vals]
    den = ex[0] + ex[1] + ex[2] + ex[3]
    for k in range(TOP_K):
        w_ref[k:k + 1, :] = ex[k] / den
    member = jnp.zeros(l.shape, F32)
    for hot in hots:
        member = member + hot.astype(F32)
    before = _dot(member.astype(BF16), tri_ref[...]) + carry_ref[:, 0:1]
    for k in range(TOP_K):
        rank = jnp.sum(jnp.where(hots[k], before, 0.0), axis=0, keepdims=True)
        rank_ref[k:k + 1, :] = rank.astype(jnp.int32)
    carry_ref[...] = carry_ref[...] + jnp.sum(member, axis=1, keepdims=True)
    cnt_ref[...] = carry_ref[...]
    idx_ref[TOP_K:, :] = jnp.zeros((8 - TOP_K, tr), jnp.int32)
    w_ref[TOP_K:, :] = jnp.zeros((8 - TOP_K, tr), F32)
    rank_ref[TOP_K:, :] = jnp.zeros((8 - TOP_K, tr), jnp.int32)


def _route(logits_t):
    n = logits_t.shape[1]
    tr = min(512, n)
    tri = jnp.asarray(np.triu(np.ones((tr, tr), np.float32), 1), BF16)
    tok = pl.BlockSpec((8, tr), lambda i: (0, i))
    return pl.pallas_call(
        functools.partial(_route_kernel, tr=tr),
        grid=(n // tr,),
        in_specs=[pl.BlockSpec((N_EXPERTS, tr), lambda i: (0, i)), pl.BlockSpec((tr, tr), lambda i: (0, 0))],
        out_specs=[tok, tok, tok, pl.BlockSpec((N_EXPERTS, 128), lambda i: (0, 0))],
        out_shape=[jax.ShapeDtypeStruct((8, n), jnp.int32), jax.ShapeDtypeStruct((8, n), F32),
                   jax.ShapeDtypeStruct((8, n), jnp.int32), jax.ShapeDtypeStruct((N_EXPERTS, 128), F32)],
        scratch_shapes=[pltpu.VMEM((N_EXPERTS, 128), F32)],
        compiler_params=_cparams(("arbitrary",)),
        name="route",
    )(logits_t, tri)


def _expert_kernel(be_ref, rows_ref, x_ref, w1_ref, b1_ref, w2_ref, b2_ref, o_ref, w1b_ref, w2b_ref, *, tm):
    i = pl.program_id(0)
    rows = rows_ref[i]
    half = tm // 2

    @pl.when(jnp.logical_and(rows > 0, jnp.logical_or(i == 0, be_ref[i] != be_ref[jnp.maximum(i - 1, 0)])))
    def _():
        w1b_ref[...] = w1_ref[0].astype(BF16)
        w2b_ref[...] = w2_ref[0].astype(BF16)

    def mlp(n):
        h1 = _dot(_unpack_pairs(x_ref[0:n, :]).astype(BF16), w1b_ref[...]) + b1_ref[0]
        glu = jnp.minimum(h1[:, :EXPERT_FF], SWIGLU_LIMIT)
        lin = jnp.clip(h1[:, EXPERT_FF:], -SWIGLU_LIMIT, SWIGLU_LIMIT)
        act = glu * _sigmoid(SWIGLU_ALPHA * glu) * (lin + 1.0)
        o_ref[0:n, :] = _pack_pairs(_dot(act.astype(BF16), w2b_ref[...]) + b2_ref[0])

    @pl.when(rows > half)
    def _():
        mlp(tm)

    @pl.when(jnp.logical_and(rows > 0, rows <= half))
    def _():
        mlp(half)
        o_ref[half:, :] = jnp.zeros((tm - half, o_ref.shape[1]), o_ref.dtype)

    @pl.when(rows == 0)
    def _():
        o_ref[...] = jnp.zeros_like(o_ref)


def _experts(x_slots, block_expert, block_rows, w1, b1, w2, b2, expert_offset):
    n_slots = x_slots.shape[0]
    tm = EXPERT_TILE
    n_blocks = n_slots // tm
    off = expert_offset
    grid_spec = pltpu.PrefetchScalarGridSpec(
        num_scalar_prefetch=2,
        grid=(n_blocks,),
        in_specs=[pl.BlockSpec((tm, D_MODEL // 2), lambda i, be, br: (i, 0)),
                  pl.BlockSpec((1, D_MODEL, 2 * EXPERT_FF), lambda i, be, br: (be[i] + off, 0, 0)),
                  pl.BlockSpec((1, 1, 2 * EXPERT_FF), lambda i, be, br: (be[i], 0, 0)),
                  pl.BlockSpec((1, EXPERT_FF, D_MODEL), lambda i, be, br: (be[i] + off, 0, 0)),
                  pl.BlockSpec((1, 1, D_MODEL), lambda i, be, br: (be[i], 0, 0))],
        out_specs=pl.BlockSpec((tm, D_MODEL // 2), lambda i, be, br: (i, 0)),
        scratch_shapes=[pltpu.VMEM((D_MODEL, 2 * EXPERT_FF), BF16), pltpu.VMEM((EXPERT_FF, D_MODEL), BF16)],
    )
    return pl.pallas_call(
        functools.partial(_expert_kernel, tm=tm),
        grid_spec=grid_spec,
        out_shape=jax.ShapeDtypeStruct((n_slots, D_MODEL // 2), jnp.uint32),
        compiler_params=_cparams(("arbitrary",)),
        name="experts",
    )(block_expert, block_rows, x_slots, w1, b1.reshape(N_EXPERTS, 1, -1), w2, b2.reshape(N_EXPERTS, 1, -1))


COMBINE_TILE = 128


def _dispatch_kernel(dest_ref, h_ref, init_ref, xs_ref, sem, *, tm):
    del init_ref

    for r in range(tm):
        for k in range(TOP_K):
            pltpu.make_async_copy(h_ref.at[pl.ds(r, 1)], xs_ref.at[pl.ds(dest_ref[k, r], 1)], sem).start(
                priority=k % 2)
    for k in range(TOP_K):
        pltpu.make_async_copy(h_ref, xs_ref.at[pl.ds(0, tm)], sem).wait()


def _dispatch(h_packed, dest, n_slots, slots_init):
    n, w = h_packed.shape
    if slots_init is None:
        slots_init = jnp.zeros((n_slots, w), jnp.uint32)
    tm = min(COMBINE_TILE, n)
    return pl.pallas_call(
        functools.partial(_dispatch_kernel, tm=tm),
        grid=(n // tm,),
        in_specs=[pl.BlockSpec((TOP_K, tm), lambda i: (0, i), memory_space=pltpu.SMEM),
                  pl.BlockSpec((tm, w), lambda i: (i, 0)), pl.BlockSpec(memory_space=pl.ANY)],
        out_specs=pl.BlockSpec(memory_space=pl.ANY),
        out_shape=jax.ShapeDtypeStruct((n_slots, w), jnp.uint32),
        scratch_shapes=[pltpu.SemaphoreType.DMA(())],
        input_output_aliases={2: 0},
        compiler_params=_cparams(("arbitrary",)),
        name="dispatch",
    )(dest, h_packed, slots_init)


def _combine_kernel(d_ref, h_ref, w_ref, lnw_ref, lnb_ref, ys_ref, o_ref, buf_ref, sem, *, tm):
    i = pl.program_id(0)
    n_tiles = pl.num_programs(0) - 1

    def issue(slot):
        for r in range(tm):
            for k in range(TOP_K):
                pltpu.make_async_copy(ys_ref.at[pl.ds(d_ref[k, r], 1)], buf_ref.at[slot, k, pl.ds(r, 1)],
                                      sem.at[slot]).start(priority=k % 2)

    def finish(slot):
        for k in range(TOP_K):
            pltpu.make_async_copy(ys_ref.at[pl.ds(0, tm)], buf_ref.at[slot, k], sem.at[slot]).wait()
        acc = DEEPNORM_ALPHA * h_ref[...]
        for k in range(TOP_K):
            acc = acc + w_ref[:, k:k + 1] * _unpack_pairs(buf_ref[slot, k])
        o_ref[...] = _layer_norm(acc, lnw_ref[...], lnb_ref[...])

    for parity in range(2):
        @pl.when(jnp.logical_and(i < n_tiles, i % 2 == parity))
        def _():
            issue(parity)

        @pl.when(jnp.logical_and(i > 0, i % 2 == parity))
        def _():
            finish(1 - parity)


def _combine(h, y_slots, dest, w_tok, ln_w, ln_b):
    n = h.shape[0]
    tm = min(COMBINE_TILE, n)
    n_tiles = n // tm
    lnw = ln_w.reshape(1, -1)
    lnb = ln_b.reshape(1, -1)
    prev = lambda i: jnp.maximum(i - 1, 0)
    return pl.pallas_call(
        functools.partial(_combine_kernel, tm=tm),
        grid=(n_tiles + 1,),
        in_specs=[pl.BlockSpec((TOP_K, tm), lambda i: (0, jnp.minimum(i, n_tiles - 1)), memory_space=pltpu.SMEM),
                  pl.BlockSpec((tm, D_MODEL), lambda i: (prev(i), 0)),
                  pl.BlockSpec((tm, TOP_K), lambda i: (prev(i), 0)),
                  pl.BlockSpec(lnw.shape, lambda i: (0, 0)), pl.BlockSpec(lnb.shape, lambda i: (0, 0)),
                  pl.BlockSpec(memory_space=pl.ANY)],
        out_specs=pl.BlockSpec((tm, D_MODEL), lambda i: (prev(i), 0)),
        out_shape=jax.ShapeDtypeStruct((n, D_MODEL), F32),
        scratch_shapes=[pltpu.VMEM((2, TOP_K, tm, D_MODEL // 2), jnp.uint32), pltpu.SemaphoreType.DMA((2,))],
        compiler_params=_cparams(("arbitrary",)),
        name="combine",
    )(dest, h, w_tok, lnw, lnb, y_slots)


def _moe(h, h16, logits_t, w1, b1, w2, b2, ln_w, ln_b, expert_offset, slots_init):
    n = h.shape[0]
    tm = EXPERT_TILE
    idx8, w8, rank8, cnt = _route(logits_t)
    idx, w_top, rank = idx8[:TOP_K], w8[:TOP_K], rank8[:TOP_K]
    counts = cnt[:, 0].astype(jnp.int32)
    padded = ((counts + tm - 1) // tm) * tm
    p_end = jnp.cumsum(padded)
    p_start = p_end - padded
    experts = jnp.arange(N_EXPERTS, dtype=jnp.int32)
    seg_start = jnp.sum(jnp.where(idx[:, :, None] == experts, p_start, 0), axis=-1)
    dest = seg_start + rank
    n_slots = n * TOP_K + N_EXPERTS * tm
    n_blocks = n_slots // tm
    starts = jnp.arange(n_blocks, dtype=jnp.int32) * tm
    block_expert = jnp.minimum(jnp.sum((p_end[None, :] <= starts[:, None]).astype(jnp.int32), axis=1),
                               N_EXPERTS - 1)
    seg_stop = p_start + counts
    block_rows = jnp.clip(seg_stop[block_expert] - starts, 0, tm).astype(jnp.int32)
    x_slots = _dispatch(h16, dest, n_slots, slots_init)
    y_slots = _experts(x_slots, block_expert, block_rows, w1, b1, w2, b2, expert_offset)
    return _combine(h, y_slots, dest, w_top.T, ln_w, ln_b), x_slots


def _mixer_weight(w_in):
    o = np.cumsum((512, 512, 512, 512, MLA_Q_RANK, MLA_KV_RANK + MLA_ROPE, S5_WIDTH, LRU_WIDTH, LRU_WIDTH))
    hgrn, cq, ckv = w_in[:, :o[3]], w_in[:, o[3]:o[4]], w_in[:, o[4]:o[4] + MLA_KV_RANK]
    kpe = w_in[:, o[4] + MLA_KV_RANK:o[5]]
    su, lx, lg = w_in[:, o[5]:o[6]], w_in[:, o[6]:o[7]], w_in[:, o[7]:o[8]]
    pad = jnp.zeros((D_MODEL, 128 - MLA_ROPE), w_in.dtype)
    w_mix = jnp.concatenate([hgrn, su, lx, lg, cq, ckv, kpe, pad, _rot_half_cols(kpe), pad], axis=1)
    return w_mix, w_in[:, o[8]:]


def _layer(x, positions, lb, bsz, seq, w_in, b_gate, hgrn_norm_w, mla_q_norm_w, mla_wq_b, mla_kv_norm_w, mla_wkv_b,
           s5_a_re, s5_a_im, s5_log_dt, s5_b_re, s5_b_im, s5_c_re, s5_c_im, s5_d, s5_w_glu, s5_b_glu,
           lru_conv_w, lru_conv_b, lru_wa, lru_ba, lru_wx, lru_bx, lru_a_param,
           w_branch, w_out, ln1_w, ln1_b, ln2_w, ln2_b, router_w, router_b, moe_w1, moe_b1, moe_w2, moe_b2,
           expert_offset=0, slots_init=None):
    w_mix, w_gl = _mixer_weight(w_in)
    p = _matmul(x, w_mix.astype(BF16), 2048, MIX_WIDTH // 3, F32)
    gl = _matmul(x, w_gl.astype(BF16), 2048, 1024, BF16)
    y_a = _hgrn(p, lb, hgrn_norm_w, bsz, seq)
    qn, qp, kn, kp, vt = _mla_proj(p, positions, mla_q_norm_w, mla_wq_b, mla_kv_norm_w, mla_wkv_b, min(256, seq))
    y_b = _flash(qn, qp, kn, kp, vt, bsz, seq)
    y_s = _s5(p, s5_a_re, s5_a_im, s5_log_dt, s5_b_re, s5_b_im, s5_c_re, s5_c_im, s5_d, bsz, seq)
    y_d = _lru(p, lru_conv_w, lru_conv_b, lru_wa, lru_ba, lru_wx, lru_bx, lru_a_param, bsz, seq)
    h, h16, logits_t = _merge(y_a, y_b, y_s, y_d, gl, x, w_branch, w_out, s5_w_glu, s5_b_glu, b_gate,
                              ln1_w, ln1_b, router_w, router_b)
    return _moe(h, h16, logits_t, moe_w1, moe_b1, moe_w2, moe_b2, ln2_w, ln2_b, expert_offset, slots_init)


def kernel(x, positions, w_in, b_gate, hgrn_lb_logits, hgrn_norm_w, mla_q_norm_w, mla_wq_b, mla_kv_norm_w, mla_wkv_b, s5_a_re, s5_a_im, s5_log_dt, s5_b_re, s5_b_im, s5_c_re, s5_c_im, s5_d, s5_w_glu, s5_b_glu, lru_conv_w, lru_conv_b, lru_wa, lru_ba, lru_wx, lru_bx, lru_a_param, w_branch, w_out, ln1_w, ln1_b, ln2_w, ln2_b, router_w, router_b, moe_w1, moe_b1, moe_w2, moe_b2):
    bsz, seq, _ = x.shape
    probs = jax.nn.softmax(hgrn_lb_logits.astype(F32), axis=0)
    lower_bounds = jnp.cumsum(probs, axis=0) - probs[0:1]
    per_layer = (w_in, b_gate, hgrn_norm_w, mla_q_norm_w, mla_wq_b, mla_kv_norm_w, mla_wkv_b,
                 s5_a_re, s5_a_im, s5_log_dt, s5_b_re, s5_b_im, s5_c_re, s5_c_im, s5_d, s5_w_glu, s5_b_glu,
                 lru_conv_w, lru_conv_b, lru_wa, lru_ba, lru_wx, lru_bx, lru_a_param,
                 w_branch, w_out, ln1_w, ln1_b, ln2_w, ln2_b, router_w, router_b, moe_w1, moe_b1, moe_w2, moe_b2)
    xf = x.reshape(bsz * seq, D_MODEL)
    w1_all = moe_w1.reshape((DEPTH * N_EXPERTS,) + moe_w1.shape[2:])
    w2_all = moe_w2.reshape((DEPTH * N_EXPERTS,) + moe_w2.shape[2:])
    slots = None
    for l in range(DEPTH):
        args = [a[l] for a in per_layer]
        args[-4], args[-2] = w1_all, w2_all
        xf, slots = _layer(xf, positions, lower_bounds[l], bsz, seq, *args, expert_offset=l * N_EXPERTS,
                           slots_init=slots)
    return xf.reshape(bsz, seq, D_MODEL)
```

```python
import functools
import math

import numpy as np
import jax
import jax.numpy as jnp
from jax import lax
from jax.experimental import pallas as pl
from jax.experimental.pallas import tpu as pltpu

F32 = jnp.float32
BF16 = jnp.bfloat16

D_MODEL = 1024
DEPTH = 2
CHUNK = 64

HGRN_HEADS = 4
HGRN_DK = 128
HGRN_WIDTH = 512
HGRN_F_MIN = 1e-30
HGRN_CHUNK = 64
HGRN_LEVELS = (1, 2, 4, 8, 16, 32)
HGRN_PAR = 4

MLA_HEADS = 4
MLA_Q_RANK = 256
MLA_KV_RANK = 128
MLA_NOPE = 128
MLA_ROPE = 64
MLA_V = 128
MLA_QK = MLA_NOPE + MLA_ROPE
MLA_VROWS = MLA_V + 16
ROPE_THETA = 10000.0

S5_GROUPS = 32
S5_GROUP_CH = 16
S5_STATE = 64
S5_WIDTH = 512
S5_BLOCK = 16
S5_PAR = 8

LRU_WIDTH = 512
LRU_BLOCKS = 8
LRU_BLOCK_W = 64
LRU_CONV = 4
LRU_C = 8.0

N_BRANCH = 4
BRANCH_WIDTH = 512

N_EXPERTS = 32
TOP_K = 4
EXPERT_FF = 1024
SWIGLU_ALPHA = 1.702
SWIGLU_LIMIT = 7.0
EXPERT_TILE = 1024

DEEPNORM_ALPHA = (2.0 * DEPTH) ** 0.25
LN_EPS = 1e-5
RMS_EPS = 1e-6

COL_HQ, COL_HF, COL_HI, COL_HG = 0, 512, 1024, 1536
COL_SU, COL_LX, COL_LG = 2048, 2560, 3072
COL_CQ, COL_CKV, COL_KPE, COL_KPER = 3584, 3840, 3968, 4096
MIX_WIDTH = 4224

VMEM_LIMIT = 56 * 1024 * 1024


def _cparams(sem):
    return pltpu.CompilerParams(dimension_semantics=sem, vmem_limit_bytes=VMEM_LIMIT)


def _dot(a, b):
    return jnp.dot(a, b, preferred_element_type=F32)


def _dot_nt(a, b):
    return lax.dot_general(a, b, (((1,), (1,)), ((), ())), preferred_element_type=F32)


def _dot_tn(a, b):
    return lax.dot_general(a, b, (((0,), (0,)), ((), ())), preferred_element_type=F32)


def _split(x):
    hi = x.astype(BF16)
    lo = (x - hi.astype(F32)).astype(BF16)
    return hi, lo


def _dot3(a, b_hi, b_lo):
    a_hi, a_lo = _split(a)
    return _dot(a_hi, b_hi) + (_dot(a_lo, b_hi) + _dot(a_hi, b_lo))


def _pack_pairs(x):
    w = x.shape[-1] // 2
    lo = lax.bitcast_convert_type(x[:, :w].astype(BF16).astype(F32), jnp.uint32)
    hi = lax.bitcast_convert_type(x[:, w:].astype(BF16).astype(F32), jnp.uint32)
    return lax.shift_right_logical(lo, jnp.uint32(16)) | (hi & jnp.uint32(0xFFFF0000))


def _unpack_pairs(u):
    lo = lax.bitcast_convert_type(lax.shift_left(u, jnp.uint32(16)), F32)
    hi = lax.bitcast_convert_type(u & jnp.uint32(0xFFFF0000), F32)
    return jnp.concatenate([lo, hi], axis=-1)


def _sigmoid(x):
    return 0.5 * jnp.tanh(0.5 * x) + 0.5


def _silu(x):
    return x * _sigmoid(x)


def _shift_rows(x, k, fill):
    rows = lax.broadcasted_iota(jnp.int32, x.shape, 0)
    return jnp.where(rows >= k, pltpu.roll(x, k, 0), fill)


def _mm_kernel(x_ref, w_ref, o_ref):
    o_ref[...] = _dot(x_ref[...].astype(BF16), w_ref[...]).astype(o_ref.dtype)


def _matmul(x, w, tm, tn, out_dtype):
    n, k = x.shape
    m = w.shape[1]
    tm = min(tm, n)
    return pl.pallas_call(
        _mm_kernel,
        grid=(n // tm, m // tn),
        in_specs=[pl.BlockSpec((tm, k), lambda i, j: (i, 0)),
                  pl.BlockSpec((k, tn), lambda i, j: (0, j))],
        out_specs=pl.BlockSpec((tm, tn), lambda i, j: (i, j)),
        out_shape=jax.ShapeDtypeStruct((n, m), out_dtype),
        compiler_params=_cparams(("parallel", "arbitrary")),
        name="in_proj",
    )(x, w)


def _hgrn_tables():
    c = HGRN_CHUNK
    t = np.arange(c)
    windows = [np.tril(np.ones((c, c), bool)),
               np.triu(np.ones((c, c), bool), 1)]
    masks = [np.eye(c, dtype=bool)]
    for h in HGRN_LEVELS:
        blk = t // h
        odd = blk % 2 == 1
        masks.append(odd[:, None] & (blk[None, :] == blk[:, None] - 1))
        if h > 1:
            windows.append((t[None, :] >= (h * blk)[:, None]) & (t[None, :] <= t[:, None]))
            windows.append((t[None, :] > t[:, None]) & (t[None, :] <= (h * blk + h - 1)[:, None]))
    return np.concatenate(windows, 0).astype(np.float32), np.stack(masks).astype(np.float32)


def _hgrn_kernel(hq_ref, hf_ref, hi_ref, hg_ref, lb_ref, nw_ref, win_ref, mask_ref,
                 o_ref, state_ref, *, n_chunks):
    c = HGRN_CHUNK
    heads = range(HGRN_HEADS)
    hs = [slice(h * HGRN_DK, (h + 1) * HGRN_DK) for h in heads]
    n_lvl = len(HGRN_LEVELS) + 1

    @pl.when(pl.program_id(1) == 0)
    def _():
        state_ref[...] = jnp.zeros_like(state_ref)

    def group_body(gi, carry):
        par = range(HGRN_PAR)
        rows = [pl.ds(pl.multiple_of((gi * HGRN_PAR + cc) * c, c), c) for cc in par]
        lb = lb_ref[...]
        z = [hf_ref[rows[cc], :] for cc in par]
        th = [jnp.tanh(0.5 * z[cc]) for cc in par]
        f = [jnp.maximum(lb + (1.0 - lb) * (0.5 + 0.5 * th[cc]), HGRN_F_MIN) for cc in par]
        k = [(1.0 - lb) * (0.5 - 0.5 * th[cc]) for cc in par]
        q = [_silu(hq_ref[rows[cc], :]) for cc in par]
        v16 = [hi_ref[rows[cc], :].astype(BF16) for cc in par]
        lf = [_split(jnp.log(f[cc])) for cc in par]
        lf_stack = jnp.concatenate([jnp.concatenate([lf[cc][part] for cc in par], axis=-1) for part in range(2)],
                                   axis=0)
        e_all = jnp.exp(_dot(win_ref[...], lf_stack))
        e = [e_all[:, cc * HGRN_WIDTH:(cc + 1) * HGRN_WIDTH] for cc in par]
        blk = lambda cc, w: e[cc][w * c:(w + 1) * c]
        qs, ks, q_in, k_out, decay = [], [], [], [], []
        for cc in par:
            k16 = k[cc].astype(BF16)
            qs.append([q[cc].astype(BF16), (q[cc] * f[cc]).astype(BF16)]
                      + [(q[cc] * blk(cc, 2 * w)).astype(BF16) for w in range(1, n_lvl - 1)])
            ks.append([k16, k16] + [(k[cc] * blk(cc, 2 * w + 1)).astype(BF16) for w in range(1, n_lvl - 1)])
            q_in.append((q[cc] * blk(cc, 0)).astype(BF16))
            k_out.append((k[cc] * blk(cc, 1)).astype(BF16))
            decay.append(e[cc][c - 1:c, :])
        scores = [[None] * HGRN_HEADS for _ in par]
        for lv in range(n_lvl):
            s_l = [[_dot_nt(qs[cc][lv][:, hs[h]], ks[cc][lv][:, hs[h]]) for h in heads] for cc in par]
            for cc in par:
                for h in heads:
                    term = mask_ref[lv] * s_l[cc][h]
                    scores[cc][h] = term if scores[cc][h] is None else scores[cc][h] + term
        intra = [[_dot(scores[cc][h].astype(BF16), v16[cc][:, hs[h]]) for h in heads] for cc in par]
        upd = [[_dot_tn(v16[cc][:, hs[h]], k_out[cc][:, hs[h]]) for h in heads] for cc in par]
        st = [state_ref[h] for h in heads]
        for cc in par:
            o = [intra[cc][h] + _dot_nt(q_in[cc][:, hs[h]], st[h].astype(BF16)) for h in heads]
            st = [st[h] * decay[cc][:, hs[h]] + upd[cc][h] for h in heads]
            o = [o[h] * lax.rsqrt(jnp.mean(o[h] * o[h], axis=-1, keepdims=True) + RMS_EPS) for h in heads]
            o_ref[rows[cc], :] = jnp.concatenate(o, axis=-1) * nw_ref[...] * _silu(hg_ref[rows[cc], :])
        for h in heads:
            state_ref[h] = st[h]
        return carry

    lax.fori_loop(0, n_chunks // HGRN_PAR, group_body, 0)


def _hgrn(p, lb, norm_w, bsz, seq):
    t = min(512, seq)
    nt = seq // t
    win, mask = _hgrn_tables()
    col = lambda cb: pl.BlockSpec((t, HGRN_WIDTH), lambda b, i: (b * nt + i, cb))
    const2 = lambda shape: pl.BlockSpec(shape, lambda b, i: (0,) * len(shape))
    return pl.pallas_call(
        functools.partial(_hgrn_kernel, n_chunks=t // HGRN_CHUNK),
        grid=(bsz, nt),
        in_specs=[col(COL_HQ // 512), col(COL_HF // 512), col(COL_HI // 512), col(COL_HG // 512),
                  const2((1, HGRN_WIDTH)), const2((1, HGRN_WIDTH)),
                  const2((win.shape[0], 2 * win.shape[1])), const2(mask.shape)],
        out_specs=pl.BlockSpec((t, HGRN_WIDTH), lambda b, i: (b * nt + i, 0)),
        out_shape=jax.ShapeDtypeStruct((bsz * seq, HGRN_WIDTH), F32),
        scratch_shapes=[pltpu.VMEM((HGRN_HEADS, HGRN_DK, HGRN_DK), F32)],
        compiler_params=_cparams(("parallel", "arbitrary")),
        name="hgrn",
    )(p, p, p, p, lb.reshape(1, -1), norm_w.reshape(1, -1), jnp.asarray(np.concatenate([win, win], 1), BF16),
      jnp.asarray(mask))


def _rms(x, w):
    ms = jnp.mean(x * x, axis=-1, keepdims=True)
    return x * lax.rsqrt(ms + RMS_EPS) * w


def _mla_proj_kernel(cq_ref, ckv_ref, kpe_ref, kper_ref, ang_ref, qnw_ref, kvnw_ref,
                     wqn_ref, wqp_ref, wqpr_ref, wkn_ref, wv_ref,
                     qn_ref, qp_ref, kn_ref, kp_ref, v_ref):
    ang = ang_ref[...]
    cos = jnp.cos(ang)
    sin = jnp.sin(ang)
    qn = _rms(cq_ref[...], qnw_ref[...]).astype(BF16)
    cos4 = jnp.concatenate([cos] * MLA_HEADS, axis=-1)
    sin4 = jnp.concatenate([sin] * MLA_HEADS, axis=-1)
    qn_ref[...] = _dot(qn, wqn_ref[...]).astype(BF16)
    qp_ref[...] = (_dot(qn, wqp_ref[...]) * cos4 + _dot(qn, wqpr_ref[...]) * sin4).astype(BF16)
    cn = _rms(ckv_ref[...], kvnw_ref[...]).astype(BF16)
    kn_ref[...] = _dot(cn, wkn_ref[...]).astype(BF16)
    vt = _dot_nt(wv_ref[...], cn)
    ones = jnp.ones((MLA_VROWS - MLA_V, vt.shape[1]), F32)
    v_ref[0] = jnp.concatenate(
        [piece for h in range(MLA_HEADS) for piece in (vt[h * MLA_V:(h + 1) * MLA_V], ones)], axis=0).astype(BF16)
    kp_ref[...] = (kpe_ref[...] * cos + kper_ref[...] * sin).astype(BF16)


def _rot_half_cols(w):
    half = w.shape[-1] // 2
    return jnp.concatenate([-w[..., half:], w[..., :half]], axis=-1)


def _mla_proj(p, positions, q_norm_w, wq_b, kv_norm_w, wkv_b, tm):
    n = p.shape[0]
    scale = MLA_QK ** -0.5 * math.log2(math.e)
    wq = wq_b.reshape(MLA_Q_RANK, MLA_HEADS, MLA_QK) * scale
    w_nope = wq[:, :, :MLA_NOPE].reshape(MLA_Q_RANK, MLA_HEADS * MLA_NOPE)
    w_pe = wq[:, :, MLA_NOPE:]
    pad = jnp.zeros((MLA_Q_RANK, MLA_HEADS, 128 - MLA_ROPE), F32)
    w_pe_p = jnp.concatenate([w_pe, pad], axis=-1).reshape(MLA_Q_RANK, MLA_HEADS * 128)
    w_per_p = jnp.concatenate([_rot_half_cols(w_pe), pad], axis=-1).reshape(MLA_Q_RANK, MLA_HEADS * 128)
    wkv = wkv_b.reshape(MLA_KV_RANK, MLA_HEADS, MLA_NOPE + MLA_V)
    w_kn = wkv[:, :, :MLA_NOPE].reshape(MLA_KV_RANK, MLA_HEADS * MLA_NOPE)
    w_v = wkv[:, :, MLA_NOPE:].reshape(MLA_KV_RANK, MLA_HEADS * MLA_V)
    inv_freq = ROPE_THETA ** (-jnp.arange(0, MLA_ROPE, 2, dtype=F32) / MLA_ROPE)
    freq_lane = jnp.concatenate([inv_freq, inv_freq, jnp.zeros((128 - MLA_ROPE,), F32)])
    ang = positions.reshape(n, 1).astype(F32) * freq_lane[None, :]

    row = lambda w, cb: pl.BlockSpec((tm, w), lambda i: (i, cb))
    full = lambda a: pl.BlockSpec(a.shape, lambda i: (0,) * a.ndim)
    weights = [w_nope.astype(BF16), w_pe_p.astype(BF16), w_per_p.astype(BF16), w_kn.astype(BF16),
               w_v.T.astype(BF16)]
    qnw = q_norm_w.reshape(1, -1)
    kvnw = kv_norm_w.reshape(1, -1)
    outs = pl.pallas_call(
        _mla_proj_kernel,
        grid=(n // tm,),
        in_specs=[row(256, COL_CQ // 256), row(128, COL_CKV // 128), row(128, COL_KPE // 128),
                  row(128, COL_KPER // 128), row(128, 0), full(qnw), full(kvnw)] + [full(w) for w in weights],
        out_specs=[row(512, 0), row(512, 0), row(512, 0), row(128, 0),
                   pl.BlockSpec((1, MLA_HEADS * MLA_VROWS, tm), lambda i: (i, 0, 0))],
        out_shape=[jax.ShapeDtypeStruct((n, 512), BF16), jax.ShapeDtypeStruct((n, 512), BF16),
                   jax.ShapeDtypeStruct((n, 512), BF16), jax.ShapeDtypeStruct((n, 128), BF16),
                   jax.ShapeDtypeStruct((n // tm, MLA_HEADS * MLA_VROWS, tm), BF16)],
        compiler_params=_cparams(("parallel",)),
        name="mla_proj",
    )(p, p, p, p, ang, qnw, kvnw, *weights)
    return outs


def _flash_kernel(qn_ref, qp_ref, kn_ref, kp_ref, vt_ref, o_ref, acc_ref, *, tq):
    i = pl.program_id(1)
    heads = range(MLA_HEADS)
    hs = [slice(h * 128, (h + 1) * 128) for h in heads]
    vs = [slice(h * MLA_VROWS, (h + 1) * MLA_VROWS) for h in heads]
    acc_ref[...] = jnp.zeros_like(acc_ref)

    def scores(j):
        r0 = pl.multiple_of(j * tq, tq)
        kp = kp_ref[pl.ds(r0, tq), :]
        return tuple(_dot_nt(jnp.concatenate([kn_ref[pl.ds(r0, tq), hs[h]], kp], axis=-1),
                             jnp.concatenate([qn_ref[:, hs[h]], qp_ref[:, hs[h]]], axis=-1)) for h in heads)

    def consume(j, s, m, mask):
        p, m_new, alpha = [], [], []
        for h in heads:
            sh = s[h] if mask is None else jnp.where(mask, s[h], -1e30)
            mn = jnp.maximum(m[h], jnp.max(sh, axis=0, keepdims=True))
            alpha.append(jnp.exp2(m[h] - mn))
            p.append(jnp.exp2(sh - mn).astype(BF16))
            m_new.append(mn)
        vt = vt_ref[j]
        pv = [_dot(vt[vs[h], :], p[h]) for h in heads]
        for h in heads:
            acc_ref[h] = acc_ref[h] * alpha[h] + pv[h]
        return tuple(m_new)

    def body(j, carry):
        s, m = carry
        s_next = scores(j + 1)
        return s_next, consume(j, s, m, None)

    m0 = tuple(jnp.full((1, tq), -1e30, F32) for _ in heads)
    s, m = lax.fori_loop(0, i, body, (scores(0), m0))
    shift = CHUNK.bit_length() - 1
    kc = lax.shift_right_logical(lax.broadcasted_iota(jnp.int32, (tq, tq), 0), shift)
    qc = lax.shift_right_logical(lax.broadcasted_iota(jnp.int32, (tq, tq), 1), shift)
    consume(i, s, m, kc <= qc)
    for h in heads:
        acc = acc_ref[h]
        o_ref[:, hs[h]] = (acc[:MLA_V] / acc[MLA_V:MLA_V + 1]).T.astype(o_ref.dtype)


def _flash(qn, qp, kn, kp, vt, bsz, seq):
    tq = vt.shape[2]
    nq = seq // tq
    width = MLA_HEADS * 128
    qspec = pl.BlockSpec((tq, width), lambda b, i: (b * nq + i, 0))
    kspec = pl.BlockSpec((seq, width), lambda b, i: (b, 0))
    kpspec = pl.BlockSpec((seq, 128), lambda b, i: (b, 0))
    vspec = pl.BlockSpec((nq, MLA_HEADS * MLA_VROWS, tq), lambda b, i: (b, 0, 0))
    return pl.pallas_call(
        functools.partial(_flash_kernel, tq=tq),
        grid=(bsz, nq),
        in_specs=[qspec, qspec, kspec, kpspec, vspec],
        out_specs=qspec,
        out_shape=jax.ShapeDtypeStruct((bsz * seq, width), BF16),
        scratch_shapes=[pltpu.VMEM((MLA_HEADS, MLA_VROWS, tq), F32)],
        compiler_params=_cparams(("parallel", "arbitrary")),
        name="flash",
    )(qn, qp, kn, kp, vt)


def _cmul(a, b):
    return a[0] * b[0] - a[1] * b[1], a[0] * b[1] + a[1] * b[0]


def _s5_tables(a_re, a_im, log_dt, b_re, b_im, c_re, c_im, d_skip, steps):
    lb = S5_BLOCK
    f32 = lambda v: v.astype(F32)
    a_re, a_im, b_re, b_im, c_re, c_im = map(f32, (a_re, a_im, b_re, b_im, c_re, c_im))
    dt = jnp.exp(f32(log_dt))[:, None]
    ld = (a_re * dt, a_im * dt)

    def power(tau):
        mag = jnp.exp(ld[0] * tau)
        return mag * jnp.cos(ld[1] * tau), mag * jnp.sin(ld[1] * tau)

    lam_bar = power(1.0)
    inv = a_re * a_re + a_im * a_im
    ratio = _cmul((lam_bar[0] - 1.0, lam_bar[1]), (a_re / inv, -a_im / inv))
    b_bar = _cmul((ratio[0][..., None], ratio[1][..., None]), (b_re, b_im))
    tau = jnp.arange(lb + 1, dtype=F32)[:, None, None]
    pw = power(tau)
    cp = _cmul((c_re[None], c_im[None]), (pw[0][:, :, None, :], pw[1][:, :, None, :]))
    kern = jnp.einsum('tgcn,gnd->tgcd', cp[0][:lb], b_bar[0]) - jnp.einsum('tgcn,gnd->tgcd', cp[1][:lb], b_bar[1])
    s_idx = np.arange(lb)[:, None]
    t_idx = np.arange(lb)[None, :]
    lag = np.clip(t_idx - s_idx, 0, lb - 1)
    causal = jnp.asarray((t_idx >= s_idx).astype(np.float32))
    toe = kern[lag] * causal[:, :, None, None, None]
    skip = jnp.asarray(np.eye(lb, dtype=np.float32))[:, :, None, None, None] * (
        d_skip.astype(F32)[None, None, :, :, None] * jnp.eye(S5_GROUP_CH, dtype=F32)[None, None, None])
    toe = (toe + skip).transpose(2, 1, 3, 0, 4).reshape(S5_GROUPS, lb * S5_GROUP_CH, lb * S5_GROUP_CH)
    emit = jnp.concatenate([cp[0][1:], -cp[1][1:]], axis=-1)
    emit = emit.transpose(1, 0, 2, 3).reshape(S5_GROUPS, lb * S5_GROUP_CH, 2 * S5_STATE)
    rev = (pw[0][:lb][::-1][..., None], pw[1][:lb][::-1][..., None])
    bp = _cmul(rev, (b_bar[0][None], b_bar[1][None]))
    fold = jnp.concatenate([bp[0], bp[1]], axis=2)
    fold = fold.transpose(1, 2, 0, 3).reshape(S5_GROUPS, 2 * S5_STATE, lb * S5_GROUP_CH)
    jump = (lb * 2.0 ** jnp.arange(steps, dtype=F32))[:, None, None]
    pj = power(jump)
    pw_re = jnp.concatenate([pj[0], pj[0]], axis=-1).transpose(1, 2, 0)
    pw_im = jnp.concatenate([-pj[1], pj[1]], axis=-1).transpose(1, 2, 0)
    return toe, emit, fold, pw_re, pw_im


def _s5_kernel(su0_ref, su1_ref, su2_ref, su3_ref, toe_ref, emit_ref, fold_ref, pre_ref, pim_ref, y_ref,
               ut_ref, carry_ref, *, r, steps):
    lb = S5_BLOCK
    gc = S5_GROUP_CH

    @pl.when(pl.program_id(1) == 0)
    def _():
        carry_ref[...] = jnp.zeros_like(carry_ref)

    for s in range(lb):
        for c, su_ref in enumerate((su0_ref, su1_ref, su2_ref, su3_ref)):
            ut_ref[s, c * 128:(c + 1) * 128, :] = su_ref[pl.ds(s, r, stride=lb), :].T
    lane = lax.broadcasted_iota(jnp.int32, (2 * S5_STATE, r), 1)
    swap = lambda v: jnp.concatenate([v[S5_STATE:], v[:S5_STATE]], axis=0)

    def groups(gi, carry):
        par = range(S5_PAR)
        g = [gi * S5_PAR + a for a in par]
        g0 = [pl.multiple_of(g[a] * gc, gc) for a in par]
        ug = [jnp.concatenate([ut_ref[s, pl.ds(g0[a], gc), :] for s in range(lb)], axis=0).astype(BF16)
              for a in par]
        pre = [pre_ref[g[a]] for a in par]
        pim = [pim_ref[g[a]] for a in par]
        cmul = lambda a, j, v: pre[a][:, j:j + 1] * v + pim[a][:, j:j + 1] * swap(v)
        car = [carry_ref[g[a]] for a in par]
        x = [_dot(fold_ref[g[a]], ug[a]) for a in par]
        x = [x[a] + jnp.where(lane == 0, cmul(a, 0, car[a]), 0.0) for a in par]
        for j in range(steps):
            xs = [jnp.where(lane >= 2 ** j, pltpu.roll(x[a], 2 ** j, 1), 0.0) for a in par]
            x = [x[a] + cmul(a, j, xs[a]) for a in par]
        x_prev = [jnp.where(lane == 0, car[a], pltpu.roll(x[a], 1, 1)) for a in par]
        y = [_dot(toe_ref[g[a]], ug[a]) + _dot(emit_ref[g[a]], x_prev[a].astype(BF16)) for a in par]
        for a in par:
            carry_ref[g[a]] = jnp.broadcast_to(x[a][:, r - 1:r], x[a].shape)
            for t in range(lb):
                ut_ref[t, pl.ds(g0[a], gc), :] = y[a][t * gc:(t + 1) * gc]
        return carry

    lax.fori_loop(0, S5_GROUPS // S5_PAR, groups, 0)
    for t in range(lb):
        for c in range(S5_WIDTH // 128):
            y_ref[c, pl.ds(t, r, stride=lb), :] = ut_ref[t, c * 128:(c + 1) * 128, :].T


def _s5(p, a_re, a_im, log_dt, b_re, b_im, c_re, c_im, d_skip, bsz, seq):
    lb = S5_BLOCK
    r = min(128, seq // lb)
    nt = seq // (lb * r)
    ncb = S5_WIDTH // 128
    steps = max(1, int(math.ceil(math.log2(r))))
    toe, emit, fold, pw_re, pw_im = _s5_tables(a_re, a_im, log_dt, b_re, b_im, c_re, c_im, d_skip, steps)
    consts = [toe.astype(BF16), emit.astype(BF16), fold.astype(BF16), pw_re, pw_im]
    full = lambda a: pl.BlockSpec(a.shape, lambda b, t: (0,) * a.ndim)
    return pl.pallas_call(
        functools.partial(_s5_kernel, r=r, steps=steps),
        grid=(bsz, nt),
        in_specs=[pl.BlockSpec((lb * r, 128), functools.partial(lambda b, t, c: (b * nt + t, COL_SU // 128 + c), c=c))
                  for c in range(ncb)] + [full(c) for c in consts],
        out_specs=pl.BlockSpec((ncb, lb * r, 128), lambda b, t: (0, b * nt + t, 0)),
        out_shape=jax.ShapeDtypeStruct((ncb, bsz * seq, 128), F32),
        scratch_shapes=[pltpu.VMEM((lb, S5_WIDTH, r), F32), pltpu.VMEM((S5_GROUPS, 2 * S5_STATE, r), F32)],
        compiler_params=_cparams(("parallel", "arbitrary")),
        name="s5",
    )(*([p] * ncb), *consts)


def _lru_kernel(lx_ref, lg_ref, cw_ref, cb_ref, wg_ref, bg_ref, sp_ref, o_ref, tail_ref, h_ref, *, ts):
    @pl.when(pl.program_id(1) == 0)
    def _():
        tail_ref[...] = jnp.zeros_like(tail_ref)
        h_ref[...] = jnp.zeros_like(h_ref)

    x = lx_ref[...]
    tail_ref[8:, :] = x
    xc = cb_ref[...] + cw_ref[LRU_CONV - 1:LRU_CONV, :] * x
    for j in range(1, LRU_CONV):
        xc = xc + cw_ref[LRU_CONV - 1 - j:LRU_CONV - j, :] * tail_ref[8 - j:8 - j + ts, :]
    tail_ref[0:8, :] = x[ts - 8:, :]
    gates = _sigmoid(_dot3(xc, wg_ref[0], wg_ref[1]) + bg_ref[...])
    r = gates[:, :LRU_WIDTH]
    ig = gates[:, LRU_WIDTH:]
    log_a = -LRU_C * r * sp_ref[...]
    a = jnp.exp(log_a)
    gap = jnp.maximum(1.0 - a * a, 0.0)
    b = jnp.where(gap > 0.0, gap * lax.rsqrt(gap), 0.0) * (ig * xc)
    sub = lax.broadcasted_iota(jnp.int32, a.shape, 0) & 7
    for k in (1, 2, 4):
        inside = sub >= k
        b = b + a * jnp.where(inside, pltpu.roll(b, k, 0), 0.0)
        a = a * jnp.where(inside, pltpu.roll(a, k, 0), 1.0)
    gate = jax.nn.gelu(lg_ref[...])
    h_in = h_ref[0:1, :]
    for g in range(ts // 8):
        rows = slice(8 * g, 8 * g + 8)
        h = b[rows] + a[rows] * h_in
        o_ref[rows, :] = h * gate[rows]
        h_in = h[7:8, :]
    h_ref[0:1, :] = h_in


def _block_diag(w):
    nb, bw, _ = w.shape
    eye = jnp.eye(nb, dtype=w.dtype)
    return (eye[:, None, :, None] * w[:, :, None, :]).reshape(nb * bw, nb * bw)


def _lru(p, conv_w, conv_b, wa, ba, wx, bx, a_param, bsz, seq):
    ts = min(512, seq)
    nt = seq // ts
    wg = jnp.concatenate([_block_diag(wa), _block_diag(wx)], axis=1)
    wg_hi = wg.astype(BF16)
    wg2 = jnp.stack([wg_hi, (wg - wg_hi.astype(F32)).astype(BF16)])
    bg = jnp.concatenate([ba, bx]).reshape(1, -1)
    sp = jax.nn.softplus(a_param.astype(F32)).reshape(1, -1)
    col = lambda cb: pl.BlockSpec((ts, LRU_WIDTH), lambda b, t: (b * nt + t, cb))
    full = lambda a: pl.BlockSpec(a.shape, lambda b, t: (0,) * a.ndim)
    cb2 = conv_b.reshape(1, -1)
    return pl.pallas_call(
        functools.partial(_lru_kernel, ts=ts),
        grid=(bsz, nt),
        in_specs=[col(COL_LX // 512), col(COL_LG // 512), full(conv_w), full(cb2), full(wg2), full(bg), full(sp)],
        out_specs=pl.BlockSpec((ts, LRU_WIDTH), lambda b, t: (b * nt + t, 0)),
        out_shape=jax.ShapeDtypeStruct((bsz * seq, LRU_WIDTH), F32),
        scratch_shapes=[pltpu.VMEM((8 + ts, LRU_WIDTH), F32), pltpu.VMEM((8, LRU_WIDTH), F32)],
        compiler_params=_cparams(("parallel", "arbitrary")),
        name="rglru",
    )(p, p, conv_w, cb2, wg2, bg, sp)


def _layer_norm(x, w, b):
    xc = x - jnp.mean(x, axis=-1, keepdims=True)
    var = jnp.mean(xc * xc, axis=-1, keepdims=True)
    return xc * lax.rsqrt(var + LN_EPS) * w + b


def _merge_kernel(ya_ref, yb_ref, ys_ref, yd_ref, gl_ref, x_ref, wbr_ref, wout_ref, wglu_ref, bglu_ref,
                  bgate_ref, lnw_ref, lnb_ref, rw_ref, rb_ref, h_ref, hb_ref, lg_ref):
    yc = jax.nn.gelu(jnp.concatenate([ys_ref[c] for c in range(S5_WIDTH // 128)], axis=-1))
    yc = yc * _sigmoid(_dot(yc.astype(BF16), wglu_ref[...]) + bglu_ref[...])
    branches = (ya_ref[...].astype(BF16), yb_ref[...], yc.astype(BF16), yd_ref[...].astype(BF16))
    mix = None
    for bi, yb in enumerate(branches):
        cs = slice(bi * D_MODEL, (bi + 1) * D_MODEL)
        gate = _sigmoid(gl_ref[:, cs].astype(F32) + bgate_ref[:, cs])
        term = gate * _dot(yb, wbr_ref[bi])
        mix = term if mix is None else mix + term
    mo = _dot(mix.astype(BF16), wout_ref[...])
    h = _layer_norm(DEEPNORM_ALPHA * x_ref[...] + mo, lnw_ref[...], lnb_ref[...])
    h_ref[...] = h
    h_hi, h_lo = _split(h)
    hb_ref[...] = _pack_pairs(h)
    hcat = jnp.concatenate([h_hi, h_lo, h_hi], axis=-1)
    lg_ref[...] = _dot_nt(rw_ref[...], hcat) + rb_ref[...]


def _merge(ya, yb, ys, yd, gl, x, w_branch, w_out, w_glu, b_glu, b_gate, ln_w, ln_b, router_w, router_b):
    n = x.shape[0]
    tm = min(512, n)
    rwt = router_w.T.astype(F32)
    rw_hi = rwt.astype(BF16)
    rw_lo = (rwt - rw_hi.astype(F32)).astype(BF16)
    rw3 = jnp.concatenate([rw_hi, rw_hi, rw_lo], axis=1)
    row = lambda w: pl.BlockSpec((tm, w), lambda i: (i, 0))
    full = lambda a: pl.BlockSpec(a.shape, lambda i: (0,) * a.ndim)
    consts = [w_branch.astype(BF16), w_out.astype(BF16), w_glu.astype(BF16), b_glu.reshape(1, -1),
              b_gate.reshape(1, -1), ln_w.reshape(1, -1), ln_b.reshape(1, -1), rw3, router_b.reshape(-1, 1)]
    return pl.pallas_call(
        _merge_kernel,
        grid=(n // tm,),
        in_specs=[row(512), row(512), pl.BlockSpec((S5_WIDTH // 128, tm, 128), lambda i: (0, i, 0)), row(512),
                  row(N_BRANCH * D_MODEL), row(D_MODEL)]
                 + [full(c) for c in consts],
        out_specs=[row(D_MODEL), row(D_MODEL // 2), pl.BlockSpec((N_EXPERTS, tm), lambda i: (0, i))],
        out_shape=[jax.ShapeDtypeStruct((n, D_MODEL), F32), jax.ShapeDtypeStruct((n, D_MODEL // 2), jnp.uint32),
                   jax.ShapeDtypeStruct((N_EXPERTS, n), F32)],
        compiler_params=_cparams(("parallel",)),
        name="merge",
    )(ya, yb, ys, yd, gl, x, *consts)


def _route_kernel(lg_ref, tri_ref, idx_ref, w_ref, rank_ref, cnt_ref, carry_ref, *, tr):
    @pl.when(pl.program_id(0) == 0)
    def _():
        carry_ref[...] = jnp.zeros_like(carry_ref)

    l = lg_ref[...]
    eidx = lax.broadcasted_iota(jnp.int32, l.shape, 0)
    vals, hots = [], []
    for _ in range(TOP_K):
        m = jnp.max(l, axis=0, keepdims=True)
        first = jnp.min(jnp.where(l == m, eidx, N_EXPERTS), axis=0, keepdims=True)
        hot = eidx == first
        l = jnp.where(hot, -jnp.inf, l)
        vals.append(m)
        hots.append(hot)
        idx_ref[len(vals) - 1:len(vals), :] = first
    ex = [jnp.exp(v - vals[0]) for v in vals]
    den = ex[0] + ex[1] + ex[2] + ex[3]
    for k in range(TOP_K):
        w_ref[k:k + 1, :] = ex[k] / den
    member = jnp.zeros(l.shape, F32)
    for hot in hots:
        member = member + hot.astype(F32)
    before = _dot(member.astype(BF16), tri_ref[...]) + carry_ref[:, 0:1]
    for k in range(TOP_K):
        rank = jnp.sum(jnp.where(hots[k], before, 0.0), axis=0, keepdims=True)
        rank_ref[k:k + 1, :] = rank.astype(jnp.int32)
    carry_ref[...] = carry_ref[...] + jnp.sum(member, axis=1, keepdims=True)
    cnt_ref[...] = carry_ref[...]
    idx_ref[TOP_K:, :] = jnp.zeros((8 - TOP_K, tr), jnp.int32)
    w_ref[TOP_K:, :] = jnp.zeros((8 - TOP_K, tr), F32)
    rank_ref[TOP_K:, :] = jnp.zeros((8 - TOP_K, tr), jnp.int32)


def _route(logits_t):
    n = logits_t.shape[1]
    tr = min(512, n)
    tri = jnp.asarray(np.triu(np.ones((tr, tr), np.float32), 1), BF16)
    tok = pl.BlockSpec((8, tr), lambda i: (0, i))
    return pl.pallas_call(
        functools.partial(_route_kernel, tr=tr),
        grid=(n // tr,),
        in_specs=[pl.BlockSpec((N_EXPERTS, tr), lambda i: (0, i)), pl.BlockSpec((tr, tr), lambda i: (0, 0))],
        out_specs=[tok, tok, tok, pl.BlockSpec((N_EXPERTS, 128), lambda i: (0, 0))],
        out_shape=[jax.ShapeDtypeStruct((8, n), jnp.int32), jax.ShapeDtypeStruct((8, n), F32),
                   jax.ShapeDtypeStruct((8, n), jnp.int32), jax.ShapeDtypeStruct((N_EXPERTS, 128), F32)],
        scratch_shapes=[pltpu.VMEM((N_EXPERTS, 128), F32)],
        compiler_params=_cparams(("arbitrary",)),
        name="route",
    )(logits_t, tri)


def _expert_kernel(be_ref, rows_ref, x_ref, w1_ref, b1_ref, w2_ref, b2_ref, o_ref, w1b_ref, w2b_ref, *, tm):
    i = pl.program_id(0)
    rows = rows_ref[i]
    half = tm // 2

    @pl.when(jnp.logical_and(rows > 0, jnp.logical_or(i == 0, be_ref[i] != be_ref[jnp.maximum(i - 1, 0)])))
    def _():
        w1b_ref[...] = w1_ref[0].astype(BF16)
        w2b_ref[...] = w2_ref[0].astype(BF16)

    def mlp(n):
        h1 = _dot(_unpack_pairs(x_ref[0:n, :]).astype(BF16), w1b_ref[...]) + b1_ref[0]
        glu = jnp.minimum(h1[:, :EXPERT_FF], SWIGLU_LIMIT)
        lin = jnp.clip(h1[:, EXPERT_FF:], -SWIGLU_LIMIT, SWIGLU_LIMIT)
        act = glu * _sigmoid(SWIGLU_ALPHA * glu) * (lin + 1.0)
        o_ref[0:n, :] = _pack_pairs(_dot(act.astype(BF16), w2b_ref[...]) + b2_ref[0])

    @pl.when(rows > half)
    def _():
        mlp(tm)

    @pl.when(jnp.logical_and(rows > 0, rows <= half))
    def _():
        mlp(half)
        o_ref[half:, :] = jnp.zeros((tm - half, o_ref.shape[1]), o_ref.dtype)

    @pl.when(rows == 0)
    def _():
        o_ref[...] = jnp.zeros_like(o_ref)


def _experts(x_slots, block_expert, block_rows, w1, b1, w2, b2, expert_offset):
    n_slots = x_slots.shape[0]
    tm = EXPERT_TILE
    n_blocks = n_slots // tm
    off = expert_offset
    grid_spec = pltpu.PrefetchScalarGridSpec(
        num_scalar_prefetch=2,
        grid=(n_blocks,),
        in_specs=[pl.BlockSpec((tm, D_MODEL // 2), lambda i, be, br: (i, 0)),
                  pl.BlockSpec((1, D_MODEL, 2 * EXPERT_FF), lambda i, be, br: (be[i] + off, 0, 0)),
                  pl.BlockSpec((1, 1, 2 * EXPERT_FF), lambda i, be, br: (be[i], 0, 0)),
                  pl.BlockSpec((1, EXPERT_FF, D_MODEL), lambda i, be, br: (be[i] + off, 0, 0)),
                  pl.BlockSpec((1, 1, D_MODEL), lambda i, be, br: (be[i], 0, 0))],
        out_specs=pl.BlockSpec((tm, D_MODEL // 2), lambda i, be, br: (i, 0)),
        scratch_shapes=[pltpu.VMEM((D_MODEL, 2 * EXPERT_FF), BF16), pltpu.VMEM((EXPERT_FF, D_MODEL), BF16)],
    )
    return pl.pallas_call(
        functools.partial(_expert_kernel, tm=tm),
        grid_spec=grid_spec,
        out_shape=jax.ShapeDtypeStruct((n_slots, D_MODEL // 2), jnp.uint32),
        compiler_params=_cparams(("arbitrary",)),
        name="experts",
    )(block_expert, block_rows, x_slots, w1, b1.reshape(N_EXPERTS, 1, -1), w2, b2.reshape(N_EXPERTS, 1, -1))


COMBINE_TILE = 256


def _dispatch_kernel(dest_ref, h_ref, init_ref, xs_ref, sem, *, tm):
    del init_ref

    for r in range(tm):
        for k in range(TOP_K):
            pltpu.make_async_copy(h_ref.at[pl.ds(r, 1)], xs_ref.at[pl.ds(dest_ref[k, r], 1)], sem).start(
                priority=k % 2)
    for k in range(TOP_K):
        pltpu.make_async_copy(h_ref, xs_ref.at[pl.ds(0, tm)], sem).wait()


def _dispatch(h_packed, dest, n_slots, slots_init):
    n, w = h_packed.shape
    if slots_init is None:
        slots_init = jnp.zeros((n_slots, w), jnp.uint32)
    tm = min(COMBINE_TILE, n)
    return pl.pallas_call(
        functools.partial(_dispatch_kernel, tm=tm),
        grid=(n // tm,),
        in_specs=[pl.BlockSpec((TOP_K, tm), lambda i: (0, i), memory_space=pltpu.SMEM),
                  pl.BlockSpec((tm, w), lambda i: (i, 0)), pl.BlockSpec(memory_space=pl.ANY)],
        out_specs=pl.BlockSpec(memory_space=pl.ANY),
        out_shape=jax.ShapeDtypeStruct((n_slots, w), jnp.uint32),
        scratch_shapes=[pltpu.SemaphoreType.DMA(())],
        input_output_aliases={2: 0},
        compiler_params=_cparams(("arbitrary",)),
        name="dispatch",
    )(dest, h_packed, slots_init)


def _combine_kernel(d_ref, h_ref, w_ref, lnw_ref, lnb_ref, ys_ref, o_ref, buf_ref, sem, *, tm):
    i = pl.program_id(0)
    n_tiles = pl.num_programs(0) - 1

    def issue(slot):
        for r in range(tm):
            for k in range(TOP_K):
                pltpu.make_async_copy(ys_ref.at[pl.ds(d_ref[k, r], 1)], buf_ref.at[slot, k, pl.ds(r, 1)],
                                      sem.at[slot]).start(priority=k % 2)

    def finish(slot):
        for k in range(TOP_K):
            pltpu.make_async_copy(ys_ref.at[pl.ds(0, tm)], buf_ref.at[slot, k], sem.at[slot]).wait()
        acc = DEEPNORM_ALPHA * h_ref[...]
        for k in range(TOP_K):
            acc = acc + w_ref[:, k:k + 1] * _unpack_pairs(buf_ref[slot, k])
        o_ref[...] = _layer_norm(acc, lnw_ref[...], lnb_ref[...])

    for parity in range(2):
        @pl.when(jnp.logical_and(i < n_tiles, i % 2 == parity))
        def _():
            issue(parity)

        @pl.when(jnp.logical_and(i > 0, i % 2 == parity))
        def _():
            finish(1 - parity)


def _combine(h, y_slots, dest, w_tok, ln_w, ln_b):
    n = h.shape[0]
    tm = min(COMBINE_TILE, n)
    n_tiles = n // tm
    lnw = ln_w.reshape(1, -1)
    lnb = ln_b.reshape(1, -1)
    prev = lambda i: jnp.maximum(i - 1, 0)
    return pl.pallas_call(
        functools.partial(_combine_kernel, tm=tm),
        grid=(n_tiles + 1,),
        in_specs=[pl.BlockSpec((TOP_K, tm), lambda i: (0, jnp.minimum(i, n_tiles - 1)), memory_space=pltpu.SMEM),
                  pl.BlockSpec((tm, D_MODEL), lambda i: (prev(i), 0)),
                  pl.BlockSpec((tm, TOP_K), lambda i: (prev(i), 0)),
                  pl.BlockSpec(lnw.shape, lambda i: (0, 0)), pl.BlockSpec(lnb.shape, lambda i: (0, 0)),
                  pl.BlockSpec(memory_space=pl.ANY)],
        out_specs=pl.BlockSpec((tm, D_MODEL), lambda i: (prev(i), 0)),
        out_shape=jax.ShapeDtypeStruct((n, D_MODEL), F32),
        scratch_shapes=[pltpu.VMEM((2, TOP_K, tm, D_MODEL // 2), jnp.uint32), pltpu.SemaphoreType.DMA((2,))],
        compiler_params=_cparams(("arbitrary",)),
        name="combine",
    )(dest, h, w_tok, lnw, lnb, y_slots)


def _moe(h, h16, logits_t, w1, b1, w2, b2, ln_w, ln_b, expert_offset, slots_init):
    n = h.shape[0]
    tm = EXPERT_TILE
    idx8, w8, rank8, cnt = _route(logits_t)
    idx, w_top, rank = idx8[:TOP_K], w8[:TOP_K], rank8[:TOP_K]
    counts = cnt[:, 0].astype(jnp.int32)
    padded = ((counts + tm - 1) // tm) * tm
    p_end = jnp.cumsum(padded)
    p_start = p_end - padded
    experts = jnp.arange(N_EXPERTS, dtype=jnp.int32)
    seg_start = jnp.sum(jnp.where(idx[:, :, None] == experts, p_start, 0), axis=-1)
    dest = seg_start + rank
    n_slots = n * TOP_K + N_EXPERTS * tm
    n_blocks = n_slots // tm
    starts = jnp.arange(n_blocks, dtype=jnp.int32) * tm
    block_expert = jnp.minimum(jnp.sum((p_end[None, :] <= starts[:, None]).astype(jnp.int32), axis=1),
                               N_EXPERTS - 1)
    seg_stop = p_start + counts
    block_rows = jnp.clip(seg_stop[block_expert] - starts, 0, tm).astype(jnp.int32)
    x_slots = _dispatch(h16, dest, n_slots, slots_init)
    y_slots = _experts(x_slots, block_expert, block_rows, w1, b1, w2, b2, expert_offset)
    return _combine(h, y_slots, dest, w_top.T, ln_w, ln_b), x_slots


def _mixer_weight(w_in):
    o = np.cumsum((512, 512, 512, 512, MLA_Q_RANK, MLA_KV_RANK + MLA_ROPE, S5_WIDTH, LRU_WIDTH, LRU_WIDTH))
    hgrn, cq, ckv = w_in[:, :o[3]], w_in[:, o[3]:o[4]], w_in[:, o[4]:o[4] + MLA_KV_RANK]
    kpe = w_in[:, o[4] + MLA_KV_RANK:o[5]]
    su, lx, lg = w_in[:, o[5]:o[6]], w_in[:, o[6]:o[7]], w_in[:, o[7]:o[8]]
    pad = jnp.zeros((D_MODEL, 128 - MLA_ROPE), w_in.dtype)
    w_mix = jnp.concatenate([hgrn, su, lx, lg, cq, ckv, kpe, pad, _rot_half_cols(kpe), pad], axis=1)
    return w_mix, w_in[:, o[8]:]


def _layer(x, positions, lb, bsz, seq, w_in, b_gate, hgrn_norm_w, mla_q_norm_w, mla_wq_b, mla_kv_norm_w, mla_wkv_b,
           s5_a_re, s5_a_im, s5_log_dt, s5_b_re, s5_b_im, s5_c_re, s5_c_im, s5_d, s5_w_glu, s5_b_glu,
           lru_conv_w, lru_conv_b, lru_wa, lru_ba, lru_wx, lru_bx, lru_a_param,
           w_branch, w_out, ln1_w, ln1_b, ln2_w, ln2_b, router_w, router_b, moe_w1, moe_b1, moe_w2, moe_b2,
           expert_offset=0, slots_init=None):
    w_mix, w_gl = _mixer_weight(w_in)
    p = _matmul(x, w_mix.astype(BF16), 2048, MIX_WIDTH // 3, F32)
    gl = _matmul(x, w_gl.astype(BF16), 2048, 1024, BF16)
    y_a = _hgrn(p, lb, hgrn_norm_w, bsz, seq)
    qn, qp, kn, kp, vt = _mla_proj(p, positions, mla_q_norm_w, mla_wq_b, mla_kv_norm_w, mla_wkv_b, min(256, seq))
    y_b = _flash(qn, qp, kn, kp, vt, bsz, seq)
    y_s = _s5(p, s5_a_re, s5_a_im, s5_log_dt, s5_b_re, s5_b_im, s5_c_re, s5_c_im, s5_d, bsz, seq)
    y_d = _lru(p, lru_conv_w, lru_conv_b, lru_wa, lru_ba, lru_wx, lru_bx, lru_a_param, bsz, seq)
    h, h16, logits_t = _merge(y_a, y_b, y_s, y_d, gl, x, w_branch, w_out, s5_w_glu, s5_b_glu, b_gate,
                              ln1_w, ln1_b, router_w, router_b)
    return _moe(h, h16, logits_t, moe_w1, moe_b1, moe_w2, moe_b2, ln2_w, ln2_b, expert_offset, slots_init)


def kernel(x, positions, w_in, b_gate, hgrn_lb_logits, hgrn_norm_w, mla_q_norm_w, mla_wq_b, mla_kv_norm_w, mla_wkv_b, s5_a_re, s5_a_im, s5_log_dt, s5_b_re, s5_b_im, s5_c_re, s5_c_im, s5_d, s5_w_glu, s5_b_glu, lru_conv_w, lru_conv_b, lru_wa, lru_ba, lru_wx, lru_bx, lru_a_param, w_branch, w_out, ln1_w, ln1_b, ln2_w, ln2_b, router_w, router_b, moe_w1, moe_b1, moe_w2, moe_b2):
    bsz, seq, _ = x.shape
    probs = jax.nn.softmax(hgrn_lb_logits.astype(F32), axis=0)
    lower_bounds = jnp.cumsum(probs, axis=0) - probs[0:1]
    per_layer = (w_in, b_gate, hgrn_norm_w, mla_q_norm_w, mla_wq_b, mla_kv_norm_w, mla_wkv_b,
                 s5_a_re, s5_a_im, s5_log_dt, s5_b_re, s5_b_im, s5_c_re, s5_c_im, s5_d, s5_w_glu, s5_b_glu,
                 lru_conv_w, lru_conv_b, lru_wa, lru_ba, lru_wx, lru_bx, lru_a_param,
                 w_branch, w_out, ln1_w, ln1_b, ln2_w, ln2_b, router_w, router_b, moe_w1, moe_b1, moe_w2, moe_b2)
    xf = x.reshape(bsz * seq, D_MODEL)
    w1_all = moe_w1.reshape((DEPTH * N_EXPERTS,) + moe_w1.shape[2:])
    w2_all = moe_w2.reshape((DEPTH * N_EXPERTS,) + moe_w2.shape[2:])
    slots = None
    for l in range(DEPTH):
        args = [a[l] for a in per_layer]
        args[-4], args[-2] = w1_all, w2_all
        xf, slots = _layer(xf, positions, lower_bounds[l], bsz, seq, *args, expert_offset=l * N_EXPERTS,
                           slots_init=slots)
    return xf.reshape(bsz, seq, D_MODEL)
```

```python
import functools
import math

import numpy as np
import jax
import jax.numpy as jnp
from jax import lax
from jax.experimental import pallas as pl
from jax.experimental.pallas import tpu as pltpu

F32 = jnp.float32
BF16 = jnp.bfloat16

D_MODEL = 1024
DEPTH = 2
CHUNK = 64

HGRN_HEADS = 4
HGRN_DK = 128
HGRN_WIDTH = 512
HGRN_F_MIN = 1e-30
HGRN_CHUNK = 64
HGRN_LEVELS = (1, 2, 4, 8, 16, 32)
HGRN_PAR = 4

MLA_HEADS = 4
MLA_Q_RANK = 256
MLA_KV_RANK = 128
MLA_NOPE = 128
MLA_ROPE = 64
MLA_V = 128
MLA_QK = MLA_NOPE + MLA_ROPE
MLA_VROWS = MLA_V + 16
ROPE_THETA = 10000.0

S5_GROUPS = 32
S5_GROUP_CH = 16
S5_STATE = 64
S5_WIDTH = 512
S5_BLOCK = 16
S5_PAR = 8

LRU_WIDTH = 512
LRU_BLOCKS = 8
LRU_BLOCK_W = 64
LRU_CONV = 4
LRU_C = 8.0

N_BRANCH = 4
BRANCH_WIDTH = 512

N_EXPERTS = 32
TOP_K = 4
EXPERT_FF = 1024
SWIGLU_ALPHA = 1.702
SWIGLU_LIMIT = 7.0
EXPERT_TILE = 1024

DEEPNORM_ALPHA = (2.0 * DEPTH) ** 0.25
LN_EPS = 1e-5
RMS_EPS = 1e-6

COL_HQ, COL_HF, COL_HI, COL_HG = 0, 512, 1024, 1536
COL_SU, COL_LX, COL_LG = 2048, 2560, 3072
COL_CQ, COL_CKV, COL_KPE, COL_KPER = 3584, 3840, 3968, 4096
MIX_WIDTH = 4224

VMEM_LIMIT = 56 * 1024 * 1024


def _cparams(sem):
    return pltpu.CompilerParams(dimension_semantics=sem, vmem_limit_bytes=VMEM_LIMIT)


def _dot(a, b):
    return jnp.dot(a, b, preferred_element_type=F32)


def _dot_nt(a, b):
    return lax.dot_general(a, b, (((1,), (1,)), ((), ())), preferred_element_type=F32)


def _dot_tn(a, b):
    return lax.dot_general(a, b, (((0,), (0,)), ((), ())), preferred_element_type=F32)


def _split(x):
    hi = x.astype(BF16)
    lo = (x - hi.astype(F32)).astype(BF16)
    return hi, lo


def _dot3(a, b_hi, b_lo):
    a_hi, a_lo = _split(a)
    return _dot(a_hi, b_hi) + (_dot(a_lo, b_hi) + _dot(a_hi, b_lo))


def _pack_pairs(x):
    w = x.shape[-1] // 2
    lo = lax.bitcast_convert_type(x[:, :w].astype(BF16).astype(F32), jnp.uint32)
    hi = lax.bitcast_convert_type(x[:, w:].astype(BF16).astype(F32), jnp.uint32)
    return lax.shift_right_logical(lo, jnp.uint32(16)) | (hi & jnp.uint32(0xFFFF0000))


def _unpack_pairs(u):
    lo = lax.bitcast_convert_type(lax.shift_left(u, jnp.uint32(16)), F32)
    hi = lax.bitcast_convert_type(u & jnp.uint32(0xFFFF0000), F32)
    return jnp.concatenate([lo, hi], axis=-1)


def _sigmoid(x):
    return 0.5 * jnp.tanh(0.5 * x) + 0.5


def _silu(x):
    return x * _sigmoid(x)


def _shift_rows(x, k, fill):
    rows = lax.broadcasted_iota(jnp.int32, x.shape, 0)
    return jnp.where(rows >= k, pltpu.roll(x, k, 0), fill)


def _mm_kernel(x_ref, w_ref, o_ref):
    o_ref[...] = _dot(x_ref[...].astype(BF16), w_ref[...]).astype(o_ref.dtype)


def _matmul(x, w, tm, tn, out_dtype):
    n, k = x.shape
    m = w.shape[1]
    tm = min(tm, n)
    return pl.pallas_call(
        _mm_kernel,
        grid=(n // tm, m // tn),
        in_specs=[pl.BlockSpec((tm, k), lambda i, j: (i, 0)),
                  pl.BlockSpec((k, tn), lambda i, j: (0, j))],
        out_specs=pl.BlockSpec((tm, tn), lambda i, j: (i, j)),
        out_shape=jax.ShapeDtypeStruct((n, m), out_dtype),
        compiler_params=_cparams(("parallel", "arbitrary")),
        name="in_proj",
    )(x, w)


def _hgrn_tables():
    c = HGRN_CHUNK
    t = np.arange(c)
    windows = [np.tril(np.ones((c, c), bool)),
               np.triu(np.ones((c, c), bool), 1)]
    masks = [np.eye(c, dtype=bool)]
    for h in HGRN_LEVELS:
        blk = t // h
        odd = blk % 2 == 1
        masks.append(odd[:, None] & (blk[None, :] == blk[:, None] - 1))
        if h > 1:
            windows.append((t[None, :] >= (h * blk)[:, None]) & (t[None, :] <= t[:, None]))
            windows.append((t[None, :] > t[:, None]) & (t[None, :] <= (h * blk + h - 1)[:, None]))
    return np.concatenate(windows, 0).astype(np.float32), np.stack(masks).astype(np.float32)


def _hgrn_kernel(hq_ref, hf_ref, hi_ref, hg_ref, lb_ref, nw_ref, win_ref, mask_ref,
                 o_ref, state_ref, *, n_chunks):
    c = HGRN_CHUNK
    heads = range(HGRN_HEADS)
    hs = [slice(h * HGRN_DK, (h + 1) * HGRN_DK) for h in heads]
    n_lvl = len(HGRN_LEVELS) + 1

    @pl.when(pl.program_id(1) == 0)
    def _():
        state_ref[...] = jnp.zeros_like(state_ref)

    def group_body(gi, carry):
        par = range(HGRN_PAR)
        rows = [pl.ds(pl.multiple_of((gi * HGRN_PAR + cc) * c, c), c) for cc in par]
        lb = lb_ref[...]
        z = [hf_ref[rows[cc], :] for cc in par]
        th = [jnp.tanh(0.5 * z[cc]) for cc in par]
        f = [jnp.maximum(lb + (1.0 - lb) * (0.5 + 0.5 * th[cc]), HGRN_F_MIN) for cc in par]
        k = [(1.0 - lb) * (0.5 - 0.5 * th[cc]) for cc in par]
        q = [_silu(hq_ref[rows[cc], :]) for cc in par]
        v16 = [hi_ref[rows[cc], :].astype(BF16) for cc in par]
        lf = [_split(jnp.log(f[cc])) for cc in par]
        lf_stack = jnp.concatenate([jnp.concatenate([lf[cc][part] for cc in par], axis=-1) for part in range(2)],
                                   axis=0)
        e_all = jnp.exp(_dot(win_ref[...], lf_stack))
        e = [e_all[:, cc * HGRN_WIDTH:(cc + 1) * HGRN_WIDTH] for cc in par]
        blk = lambda cc, w: e[cc][w * c:(w + 1) * c]
        qs, ks, q_in, k_out, decay = [], [], [], [], []
        for cc in par:
            k16 = k[cc].astype(BF16)
            qs.append([q[cc].astype(BF16), (q[cc] * f[cc]).astype(BF16)]
                      + [(q[cc] * blk(cc, 2 * w)).astype(BF16) for w in range(1, n_lvl - 1)])
            ks.append([k16, k16] + [(k[cc] * blk(cc, 2 * w + 1)).astype(BF16) for w in range(1, n_lvl - 1)])
            q_in.append((q[cc] * blk(cc, 0)).astype(BF16))
            k_out.append((k[cc] * blk(cc, 1)).astype(BF16))
            decay.append(e[cc][c - 1:c, :])
        scores = [[None] * HGRN_HEADS for _ in par]
        for lv in range(n_lvl):
            s_l = [[_dot_nt(qs[cc][lv][:, hs[h]], ks[cc][lv][:, hs[h]]) for h in heads] for cc in par]
            for cc in par:
                for h in heads:
                    term = mask_ref[lv] * s_l[cc][h]
                    scores[cc][h] = term if scores[cc][h] is None else scores[cc][h] + term
        intra = [[_dot(scores[cc][h].astype(BF16), v16[cc][:, hs[h]]) for h in heads] for cc in par]
        upd = [[_dot_tn(v16[cc][:, hs[h]], k_out[cc][:, hs[h]]) for h in heads] for cc in par]
        st = [state_ref[h] for h in heads]
        for cc in par:
            o = [intra[cc][h] + _dot_nt(q_in[cc][:, hs[h]], st[h].astype(BF16)) for h in heads]
            st = [st[h] * decay[cc][:, hs[h]] + upd[cc][h] for h in heads]
            o = [o[h] * lax.rsqrt(jnp.mean(o[h] * o[h], axis=-1, keepdims=True) + RMS_EPS) for h in heads]
            o_ref[rows[cc], :] = jnp.concatenate(o, axis=-1) * nw_ref[...] * _silu(hg_ref[rows[cc], :])
        for h in heads:
            state_ref[h] = st[h]
        return carry

    lax.fori_loop(0, n_chunks // HGRN_PAR, group_body, 0)


def _hgrn(p, lb, norm_w, bsz, seq):
    t = min(512, seq)
    nt = seq // t
    win, mask = _hgrn_tables()
    col = lambda cb: pl.BlockSpec((t, HGRN_WIDTH), lambda b, i: (b * nt + i, cb))
    const2 = lambda shape: pl.BlockSpec(shape, lambda b, i: (0,) * len(shape))
    return pl.pallas_call(
        functools.partial(_hgrn_kernel, n_chunks=t // HGRN_CHUNK),
        grid=(bsz, nt),
        in_specs=[col(COL_HQ // 512), col(COL_HF // 512), col(COL_HI // 512), col(COL_HG // 512),
                  const2((1, HGRN_WIDTH)), const2((1, HGRN_WIDTH)),
                  const2((win.shape[0], 2 * win.shape[1])), const2(mask.shape)],
        out_specs=pl.BlockSpec((t, HGRN_WIDTH), lambda b, i: (b * nt + i, 0)),
        out_shape=jax.ShapeDtypeStruct((bsz * seq, HGRN_WIDTH), F32),
        scratch_shapes=[pltpu.VMEM((HGRN_HEADS, HGRN_DK, HGRN_DK), F32)],
        compiler_params=_cparams(("parallel", "arbitrary")),
        name="hgrn",
    )(p, p, p, p, lb.reshape(1, -1), norm_w.reshape(1, -1), jnp.asarray(np.concatenate([win, win], 1), BF16),
      jnp.asarray(mask))


def _rms(x, w):
    ms = jnp.mean(x * x, axis=-1, keepdims=True)
    return x * lax.rsqrt(ms + RMS_EPS) * w


def _mla_proj_kernel(cq_ref, ckv_ref, kpe_ref, kper_ref, ang_ref, qnw_ref, kvnw_ref,
                     wqn_ref, wqp_ref, wqpr_ref, wkn_ref, wv_ref,
                     qn_ref, qp_ref, kn_ref, kp_ref, v_ref):
    ang = ang_ref[...]
    cos = jnp.cos(ang)
    sin = jnp.sin(ang)
    qn = _rms(cq_ref[...], qnw_ref[...]).astype(BF16)
    cos4 = jnp.concatenate([cos] * MLA_HEADS, axis=-1)
    sin4 = jnp.concatenate([sin] * MLA_HEADS, axis=-1)
    qn_ref[...] = _dot(qn, wqn_ref[...]).astype(BF16)
    qp_ref[...] = (_dot(qn, wqp_ref[...]) * cos4 + _dot(qn, wqpr_ref[...]) * sin4).astype(BF16)
    cn = _rms(ckv_ref[...], kvnw_ref[...]).astype(BF16)
    kn_ref[...] = _dot(cn, wkn_ref[...]).astype(BF16)
    vt = _dot_nt(wv_ref[...], cn)
    ones = jnp.ones((MLA_VROWS - MLA_V, vt.shape[1]), F32)
    v_ref[0] = jnp.concatenate(
        [piece for h in range(MLA_HEADS) for piece in (vt[h * MLA_V:(h + 1) * MLA_V], ones)], axis=0).astype(BF16)
    kp_ref[...] = (kpe_ref[...] * cos + kper_ref[...] * sin).astype(BF16)


def _rot_half_cols(w):
    half = w.shape[-1] // 2
    return jnp.concatenate([-w[..., half:], w[..., :half]], axis=-1)


def _mla_proj(p, positions, q_norm_w, wq_b, kv_norm_w, wkv_b, tm):
    n = p.shape[0]
    scale = MLA_QK ** -0.5 * math.log2(math.e)
    wq = wq_b.reshape(MLA_Q_RANK, MLA_HEADS, MLA_QK) * scale
    w_nope = wq[:, :, :MLA_NOPE].reshape(MLA_Q_RANK, MLA_HEADS * MLA_NOPE)
    w_pe = wq[:, :, MLA_NOPE:]
    pad = jnp.zeros((MLA_Q_RANK, MLA_HEADS, 128 - MLA_ROPE), F32)
    w_pe_p = jnp.concatenate([w_pe, pad], axis=-1).reshape(MLA_Q_RANK, MLA_HEADS * 128)
    w_per_p = jnp.concatenate([_rot_half_cols(w_pe), pad], axis=-1).reshape(MLA_Q_RANK, MLA_HEADS * 128)
    wkv = wkv_b.reshape(MLA_KV_RANK, MLA_HEADS, MLA_NOPE + MLA_V)
    w_kn = wkv[:, :, :MLA_NOPE].reshape(MLA_KV_RANK, MLA_HEADS * MLA_NOPE)
    w_v = wkv[:, :, MLA_NOPE:].reshape(MLA_KV_RANK, MLA_HEADS * MLA_V)
    inv_freq = ROPE_THETA ** (-jnp.arange(0, MLA_ROPE, 2, dtype=F32) / MLA_ROPE)
    freq_lane = jnp.concatenate([inv_freq, inv_freq, jnp.zeros((128 - MLA_ROPE,), F32)])
    ang = positions.reshape(n, 1).astype(F32) * freq_lane[None, :]

    row = lambda w, cb: pl.BlockSpec((tm, w), lambda i: (i, cb))
    full = lambda a: pl.BlockSpec(a.shape, lambda i: (0,) * a.ndim)
    weights = [w_nope.astype(BF16), w_pe_p.astype(BF16), w_per_p.astype(BF16), w_kn.astype(BF16),
               w_v.T.astype(BF16)]
    qnw = q_norm_w.reshape(1, -1)
    kvnw = kv_norm_w.reshape(1, -1)
    outs = pl.pallas_call(
        _mla_proj_kernel,
        grid=(n // tm,),
        in_specs=[row(256, COL_CQ // 256), row(128, COL_CKV // 128), row(128, COL_KPE // 128),
                  row(128, COL_KPER // 128), row(128, 0), full(qnw), full(kvnw)] + [full(w) for w in weights],
        out_specs=[row(512, 0), row(512, 0), row(512, 0), row(128, 0),
                   pl.BlockSpec((1, MLA_HEADS * MLA_VROWS, tm), lambda i: (i, 0, 0))],
        out_shape=[jax.ShapeDtypeStruct((n, 512), BF16), jax.ShapeDtypeStruct((n, 512), BF16),
                   jax.ShapeDtypeStruct((n, 512), BF16), jax.ShapeDtypeStruct((n, 128), BF16),
                   jax.ShapeDtypeStruct((n // tm, MLA_HEADS * MLA_VROWS, tm), BF16)],
        compiler_params=_cparams(("parallel",)),
        name="mla_proj",
    )(p, p, p, p, ang, qnw, kvnw, *weights)
    return outs


def _flash_kernel(qn_ref, qp_ref, kn_ref, kp_ref, vt_ref, o_ref, acc_ref, *, tq):
    i = pl.program_id(1)
    heads = range(MLA_HEADS)
    hs = [slice(h * 128, (h + 1) * 128) for h in heads]
    vs = [slice(h * MLA_VROWS, (h + 1) * MLA_VROWS) for h in heads]
    acc_ref[...] = jnp.zeros_like(acc_ref)

    def scores(j):
        r0 = pl.multiple_of(j * tq, tq)
        kp = kp_ref[pl.ds(r0, tq), :]
        return tuple(_dot_nt(jnp.concatenate([kn_ref[pl.ds(r0, tq), hs[h]], kp], axis=-1),
                             jnp.concatenate([qn_ref[:, hs[h]], qp_ref[:, hs[h]]], axis=-1)) for h in heads)

    def consume(j, s, m, mask):
        p, m_new, alpha = [], [], []
        for h in heads:
            sh = s[h] if mask is None else jnp.where(mask, s[h], -1e30)
            mn = jnp.maximum(m[h], jnp.max(sh, axis=0, keepdims=True))
            alpha.append(jnp.exp2(m[h] - mn))
            p.append(jnp.exp2(sh - mn).astype(BF16))
            m_new.append(mn)
        vt = vt_ref[j]
        pv = [_dot(vt[vs[h], :], p[h]) for h in heads]
        for h in heads:
            acc_ref[h] = acc_ref[h] * alpha[h] + pv[h]
        return tuple(m_new)

    def body(j, carry):
        s, m = carry
        s_next = scores(j + 1)
        return s_next, consume(j, s, m, None)

    m0 = tuple(jnp.full((1, tq), -1e30, F32) for _ in heads)
    s, m = lax.fori_loop(0, i, body, (scores(0), m0))
    shift = CHUNK.bit_length() - 1
    kc = lax.shift_right_logical(lax.broadcasted_iota(jnp.int32, (tq, tq), 0), shift)
    qc = lax.shift_right_logical(lax.broadcasted_iota(jnp.int32, (tq, tq), 1), shift)
    consume(i, s, m, kc <= qc)
    for h in heads:
        acc = acc_ref[h]
        o_ref[:, hs[h]] = (acc[:MLA_V] / acc[MLA_V:MLA_V + 1]).T.astype(o_ref.dtype)


def _flash(qn, qp, kn, kp, vt, bsz, seq):
    tq = vt.shape[2]
    nq = seq // tq
    width = MLA_HEADS * 128
    qspec = pl.BlockSpec((tq, width), lambda b, i: (b * nq + i, 0))
    kspec = pl.BlockSpec((seq, width), lambda b, i: (b, 0))
    kpspec = pl.BlockSpec((seq, 128), lambda b, i: (b, 0))
    vspec = pl.BlockSpec((nq, MLA_HEADS * MLA_VROWS, tq), lambda b, i: (b, 0, 0))
    return pl.pallas_call(
        functools.partial(_flash_kernel, tq=tq),
        grid=(bsz, nq),
        in_specs=[qspec, qspec, kspec, kpspec, vspec],
        out_specs=qspec,
        out_shape=jax.ShapeDtypeStruct((bsz * seq, width), BF16),
        scratch_shapes=[pltpu.VMEM((MLA_HEADS, MLA_VROWS, tq), F32)],
        compiler_params=_cparams(("parallel", "arbitrary")),
        name="flash",
    )(qn, qp, kn, kp, vt)


def _cmul(a, b):
    return a[0] * b[0] - a[1] * b[1], a[0] * b[1] + a[1] * b[0]


def _s5_tables(a_re, a_im, log_dt, b_re, b_im, c_re, c_im, d_skip, steps):
    lb = S5_BLOCK
    f32 = lambda v: v.astype(F32)
    a_re, a_im, b_re, b_im, c_re, c_im = map(f32, (a_re, a_im, b_re, b_im, c_re, c_im))
    dt = jnp.exp(f32(log_dt))[:, None]
    ld = (a_re * dt, a_im * dt)

    def power(tau):
        mag = jnp.exp(ld[0] * tau)
        return mag * jnp.cos(ld[1] * tau), mag * jnp.sin(ld[1] * tau)

    lam_bar = power(1.0)
    inv = a_re * a_re + a_im * a_im
    ratio = _cmul((lam_bar[0] - 1.0, lam_bar[1]), (a_re / inv, -a_im / inv))
    b_bar = _cmul((ratio[0][..., None], ratio[1][..., None]), (b_re, b_im))
    tau = jnp.arange(lb + 1, dtype=F32)[:, None, None]
    pw = power(tau)
    cp = _cmul((c_re[None], c_im[None]), (pw[0][:, :, None, :], pw[1][:, :, None, :]))
    kern = jnp.einsum('tgcn,gnd->tgcd', cp[0][:lb], b_bar[0]) - jnp.einsum('tgcn,gnd->tgcd', cp[1][:lb], b_bar[1])
    s_idx = np.arange(lb)[:, None]
    t_idx = np.arange(lb)[None, :]
    lag = np.clip(t_idx - s_idx, 0, lb - 1)
    causal = jnp.asarray((t_idx >= s_idx).astype(np.float32))
    toe = kern[lag] * causal[:, :, None, None, None]
    skip = jnp.asarray(np.eye(lb, dtype=np.float32))[:, :, None, None, None] * (
        d_skip.astype(F32)[None, None, :, :, None] * jnp.eye(S5_GROUP_CH, dtype=F32)[None, None, None])
    toe = (toe + skip).transpose(2, 1, 3, 0, 4).reshape(S5_GROUPS, lb * S5_GROUP_CH, lb * S5_GROUP_CH)
    emit = jnp.concatenate([cp[0][1:], -cp[1][1:]], axis=-1)
    emit = emit.transpose(1, 0, 2, 3).reshape(S5_GROUPS, lb * S5_GROUP_CH, 2 * S5_STATE)
    rev = (pw[0][:lb][::-1][..., None], pw[1][:lb][::-1][..., None])
    bp = _cmul(rev, (b_bar[0][None], b_bar[1][None]))
    fold = jnp.concatenate([bp[0], bp[1]], axis=2)
    fold = fold.transpose(1, 2, 0, 3).reshape(S5_GROUPS, 2 * S5_STATE, lb * S5_GROUP_CH)
    jump = (lb * 2.0 ** jnp.arange(steps, dtype=F32))[:, None, None]
    pj = power(jump)
    pw_re = jnp.concatenate([pj[0], pj[0]], axis=-1).transpose(1, 2, 0)
    pw_im = jnp.concatenate([-pj[1], pj[1]], axis=-1).transpose(1, 2, 0)
    return toe, emit, fold, pw_re, pw_im


def _s5_kernel(su0_ref, su1_ref, su2_ref, su3_ref, toe_ref, emit_ref, fold_ref, pre_ref, pim_ref, y_ref,
               ut_ref, carry_ref, *, r, steps):
    lb = S5_BLOCK
    gc = S5_GROUP_CH

    @pl.when(pl.program_id(1) == 0)
    def _():
        carry_ref[...] = jnp.zeros_like(carry_ref)

    for s in range(lb):
        for c, su_ref in enumerate((su0_ref, su1_ref, su2_ref, su3_ref)):
            ut_ref[s, c * 128:(c + 1) * 128, :] = su_ref[pl.ds(s, r, stride=lb), :].T
    lane = lax.broadcasted_iota(jnp.int32, (2 * S5_STATE, r), 1)
    swap = lambda v: jnp.concatenate([v[S5_STATE:], v[:S5_STATE]], axis=0)

    def groups(gi, carry):
        par = range(S5_PAR)
        g = [gi * S5_PAR + a for a in par]
        g0 = [pl.multiple_of(g[a] * gc, gc) for a in par]
        ug = [jnp.concatenate([ut_ref[s, pl.ds(g0[a], gc), :] for s in range(lb)], axis=0).astype(BF16)
              for a in par]
        pre = [pre_ref[g[a]] for a in par]
        pim = [pim_ref[g[a]] for a in par]
        cmul = lambda a, j, v: pre[a][:, j:j + 1] * v + pim[a][:, j:j + 1] * swap(v)
        car = [carry_ref[g[a]] for a in par]
        x = [_dot(fold_ref[g[a]], ug[a]) for a in par]
        x = [x[a] + jnp.where(lane == 0, cmul(a, 0, car[a]), 0.0) for a in par]
        for j in range(steps):
            xs = [jnp.where(lane >= 2 ** j, pltpu.roll(x[a], 2 ** j, 1), 0.0) for a in par]
            x = [x[a] + cmul(a, j, xs[a]) for a in par]
        x_prev = [jnp.where(lane == 0, car[a], pltpu.roll(x[a], 1, 1)) for a in par]
        y = [_dot(toe_ref[g[a]], ug[a]) + _dot(emit_ref[g[a]], x_prev[a].astype(BF16)) for a in par]
        for a in par:
            carry_ref[g[a]] = jnp.broadcast_to(x[a][:, r - 1:r], x[a].shape)
            for t in range(lb):
                ut_ref[t, pl.ds(g0[a], gc), :] = y[a][t * gc:(t + 1) * gc]
        return carry

    lax.fori_loop(0, S5_GROUPS // S5_PAR, groups, 0)
    for t in range(lb):
        for c in range(S5_WIDTH // 128):
            y_ref[c, pl.ds(t, r, stride=lb), :] = ut_ref[t, c * 128:(c + 1) * 128, :].T


def _s5(p, a_re, a_im, log_dt, b_re, b_im, c_re, c_im, d_skip, bsz, seq):
    lb = S5_BLOCK
    r = min(128, seq // lb)
    nt = seq // (lb * r)
    ncb = S5_WIDTH // 128
    steps = max(1, int(math.ceil(math.log2(r))))
    toe, emit, fold, pw_re, pw_im = _s5_tables(a_re, a_im, log_dt, b_re, b_im, c_re, c_im, d_skip, steps)
    consts = [toe.astype(BF16), emit.astype(BF16), fold.astype(BF16), pw_re, pw_im]
    full = lambda a: pl.BlockSpec(a.shape, lambda b, t: (0,) * a.ndim)
    return pl.pallas_call(
        functools.partial(_s5_kernel, r=r, steps=steps),
        grid=(bsz, nt),
        in_specs=[pl.BlockSpec((lb * r, 128), functools.partial(lambda b, t, c: (b * nt + t, COL_SU // 128 + c), c=c))
                  for c in range(ncb)] + [full(c) for c in consts],
        out_specs=pl.BlockSpec((ncb, lb * r, 128), lambda b, t: (0, b * nt + t, 0)),
        out_shape=jax.ShapeDtypeStruct((ncb, bsz * seq, 128), F32),
        scratch_shapes=[pltpu.VMEM((lb, S5_WIDTH, r), F32), pltpu.VMEM((S5_GROUPS, 2 * S5_STATE, r), F32)],
        compiler_params=_cparams(("parallel", "arbitrary")),
        name="s5",
    )(*([p] * ncb), *consts)


def _lru_kernel(lx_ref, lg_ref, cw_ref, cb_ref, wg_ref, bg_ref, sp_ref, o_ref, tail_ref, h_ref, *, ts):
    @pl.when(pl.program_id(1) == 0)
    def _():
        tail_ref[...] = jnp.zeros_like(tail_ref)
        h_ref[...] = jnp.zeros_like(h_ref)

    x = lx_ref[...]
    tail_ref[8:, :] = x
    xc = cb_ref[...] + cw_ref[LRU_CONV - 1:LRU_CONV, :] * x
    for j in range(1, LRU_CONV):
        xc = xc + cw_ref[LRU_CONV - 1 - j:LRU_CONV - j, :] * tail_ref[8 - j:8 - j + ts, :]
    tail_ref[0:8, :] = x[ts - 8:, :]
    gates = _sigmoid(_dot3(xc, wg_ref[0], wg_ref[1]) + bg_ref[...])
    r = gates[:, :LRU_WIDTH]
    ig = gates[:, LRU_WIDTH:]
    log_a = -LRU_C * r * sp_ref[...]
    a = jnp.exp(log_a)
    gap = jnp.maximum(1.0 - a * a, 0.0)
    b = jnp.where(gap > 0.0, gap * lax.rsqrt(gap), 0.0) * (ig * xc)
    sub = lax.broadcasted_iota(jnp.int32, a.shape, 0) & 7
    for k in (1, 2, 4):
        inside = sub >= k
        b = b + a * jnp.where(inside, pltpu.roll(b, k, 0), 0.0)
        a = a * jnp.where(inside, pltpu.roll(a, k, 0), 1.0)
    gate = jax.nn.gelu(lg_ref[...])
    h_in = h_ref[0:1, :]
    for g in range(ts // 8):
        rows = slice(8 * g, 8 * g + 8)
        h = b[rows] + a[rows] * h_in
        o_ref[rows, :] = h * gate[rows]
        h_in = h[7:8, :]
    h_ref[0:1, :] = h_in


def _block_diag(w):
    nb, bw, _ = w.shape
    eye = jnp.eye(nb, dtype=w.dtype)
    return (eye[:, None, :, None] * w[:, :, None, :]).reshape(nb * bw, nb * bw)


def _lru(p, conv_w, conv_b, wa, ba, wx, bx, a_param, bsz, seq):
    ts = min(512, seq)
    nt = seq // ts
    wg = jnp.concatenate([_block_diag(wa), _block_diag(wx)], axis=1)
    wg_hi = wg.astype(BF16)
    wg2 = jnp.stack([wg_hi, (wg - wg_hi.astype(F32)).astype(BF16)])
    bg = jnp.concatenate([ba, bx]).reshape(1, -1)
    sp = jax.nn.softplus(a_param.astype(F32)).reshape(1, -1)
    col = lambda cb: pl.BlockSpec((ts, LRU_WIDTH), lambda b, t: (b * nt + t, cb))
    full = lambda a: pl.BlockSpec(a.shape, lambda b, t: (0,) * a.ndim)
    cb2 = conv_b.reshape(1, -1)
    return pl.pallas_call(
        functools.partial(_lru_kernel, ts=ts),
        grid=(bsz, nt),
        in_specs=[col(COL_LX // 512), col(COL_LG // 512), full(conv_w), full(cb2), full(wg2), full(bg), full(sp)],
        out_specs=pl.BlockSpec((ts, LRU_WIDTH), lambda b, t: (b * nt + t, 0)),
        out_shape=jax.ShapeDtypeStruct((bsz * seq, LRU_WIDTH), F32),
        scratch_shapes=[pltpu.VMEM((8 + ts, LRU_WIDTH), F32), pltpu.VMEM((8, LRU_WIDTH), F32)],
        compiler_params=_cparams(("parallel", "arbitrary")),
        name="rglru",
    )(p, p, conv_w, cb2, wg2, bg, sp)


def _layer_norm(x, w, b):
    xc = x - jnp.mean(x, axis=-1, keepdims=True)
    var = jnp.mean(xc * xc, axis=-1, keepdims=True)
    return xc * lax.rsqrt(var + LN_EPS) * w + b


def _merge_kernel(ya_ref, yb_ref, ys_ref, yd_ref, gl_ref, x_ref, wbr_ref, wout_ref, wglu_ref, bglu_ref,
                  bgate_ref, lnw_ref, lnb_ref, rw_ref, rb_ref, h_ref, hb_ref, lg_ref):
    yc = jax.nn.gelu(jnp.concatenate([ys_ref[c] for c in range(S5_WIDTH // 128)], axis=-1))
    yc = yc * _sigmoid(_dot(yc.astype(BF16), wglu_ref[...]) + bglu_ref[...])
    branches = (ya_ref[...].astype(BF16), yb_ref[...], yc.astype(BF16), yd_ref[...].astype(BF16))
    mix = None
    for bi, yb in enumerate(branches):
        cs = slice(bi * D_MODEL, (bi + 1) * D_MODEL)
        gate = _sigmoid(gl_ref[:, cs].astype(F32) + bgate_ref[:, cs])
        term = gate * _dot(yb, wbr_ref[bi])
        mix = term if mix is None else mix + term
    mo = _dot(mix.astype(BF16), wout_ref[...])
    h = _layer_norm(DEEPNORM_ALPHA * x_ref[...] + mo, lnw_ref[...], lnb_ref[...])
    h_ref[...] = h
    h_hi, h_lo = _split(h)
    hb_ref[...] = _pack_pairs(h)
    hcat = jnp.concatenate([h_hi, h_lo, h_hi], axis=-1)
    lg_ref[...] = _dot_nt(rw_ref[...], hcat) + rb_ref[...]


def _merge(ya, yb, ys, yd, gl, x, w_branch, w_out, w_glu, b_glu, b_gate, ln_w, ln_b, router_w, router_b):
    n = x.shape[0]
    tm = min(512, n)
    rwt = router_w.T.astype(F32)
    rw_hi = rwt.astype(BF16)
    rw_lo = (rwt - rw_hi.astype(F32)).astype(BF16)
    rw3 = jnp.concatenate([rw_hi, rw_hi, rw_lo], axis=1)
    row = lambda w: pl.BlockSpec((tm, w), lambda i: (i, 0))
    full = lambda a: pl.BlockSpec(a.shape, lambda i: (0,) * a.ndim)
    consts = [w_branch.astype(BF16), w_out.astype(BF16), w_glu.astype(BF16), b_glu.reshape(1, -1),
              b_gate.reshape(1, -1), ln_w.reshape(1, -1), ln_b.reshape(1, -1), rw3, router_b.reshape(-1, 1)]
    return pl.pallas_call(
        _merge_kernel,
        grid=(n // tm,),
        in_specs=[row(512), row(512), pl.BlockSpec((S5_WIDTH // 128, tm, 128), lambda i: (0, i, 0)), row(512),
                  row(N_BRANCH * D_MODEL), row(D_MODEL)]
                 + [full(c) for c in consts],
        out_specs=[row(D_MODEL), row(D_MODEL // 2), pl.BlockSpec((N_EXPERTS, tm), lambda i: (0, i))],
        out_shape=[jax.ShapeDtypeStruct((n, D_MODEL), F32), jax.ShapeDtypeStruct((n, D_MODEL // 2), jnp.uint32),
                   jax.ShapeDtypeStruct((N_EXPERTS, n), F32)],
        compiler_params=_cparams(("parallel",)),
        name="merge",
    )(ya, yb, ys, yd, gl, x, *consts)


def _route_kernel(lg_ref, tri_ref, idx_ref, w_ref, rank_ref, cnt_ref, carry_ref, *, tr):
    @pl.when(pl.program_id(0) == 0)
    def _():
        carry_ref[...] = jnp.zeros_like(carry_ref)

    l = lg_ref[...]
    eidx = lax.broadcasted_iota(jnp.int32, l.shape, 0)
    vals, hots = [], []
    for _ in range(TOP_K):
        m = jnp.max(l, axis=0, keepdims=True)
        first = jnp.min(jnp.where(l == m, eidx, N_EXPERTS), axis=0, keepdims=True)
        hot = eidx == first
        l = jnp.where(hot, -jnp.inf, l)
        vals.append(m)
        hots.append(hot)
        idx_ref[len(vals) - 1:len(vals), :] = first
    ex = [jnp.exp(v - vals[0]) for v in vals]
    den = ex[0] + ex[1] + ex[2] + ex[3]
    for k in range(TOP_K):
        w_ref[k:k + 1, :] = ex[k] / den
    member = jnp.zeros(l.shape, F32)
    for hot in hots:
        member = member + hot.astype(F32)
    before = _dot(member.astype(BF16), tri_ref[...]) + carry_ref[:, 0:1]
    for k in range(TOP_K):
        rank = jnp.sum(jnp.where(hots[k], before, 0.0), axis=0, keepdims=True)
        rank_ref[k:k + 1, :] = rank.astype(jnp.int32)
    carry_ref[...] = carry_ref[...] + jnp.sum(member, axis=1, keepdims=True)
    cnt_ref[...] = carry_ref[...]
    idx_ref[TOP_K:, :] = jnp.zeros((8 - TOP_K, tr), jnp.int32)
    w_ref[TOP_K:, :] = jnp.zeros((8 - TOP_K, tr), F32)
    rank_ref[TOP_K:, :] = jnp.zeros((8 - TOP_K, tr), jnp.int32)


def _route(logits_t):
    n = logits_t.shape[1]
    tr = min(512, n)
    tri = jnp.asarray(np.triu(np.ones((tr, tr), np.float32), 1), BF16)
    tok = pl.BlockSpec((8, tr), lambda i: (0, i))
    return pl.pallas_call(
        functools.partial(_route_kernel, tr=tr),
        grid=(n // tr,),
        in_specs=[pl.BlockSpec((N_EXPERTS, tr), lambda i: (0, i)), pl.BlockSpec((tr, tr), lambda i: (0, 0))],
        out_specs=[tok, tok, tok, pl.BlockSpec((N_EXPERTS, 128), lambda i: (0, 0))],
        out_shape=[jax.ShapeDtypeStruct((8, n), jnp.int32), jax.ShapeDtypeStruct((8, n), F32),
                   jax.ShapeDtypeStruct((8, n), jnp.int32), jax.ShapeDtypeStruct((N_EXPERTS, 128), F32)],
        scratch_shapes=[pltpu.VMEM((N_EXPERTS, 128), F32)],
        compiler_params=_cparams(("arbitrary",)),
        name="route",
    )(logits_t, tri)


def _expert_kernel(be_ref, rows_ref, x_ref, w1_ref, b1_ref, w2_ref, b2_ref, o_ref, w1b_ref, w2b_ref, *, tm):
    i = pl.program_id(0)
    rows = rows_ref[i]
    half = tm // 2

    @pl.when(jnp.logical_and(rows > 0, jnp.logical_or(i == 0, be_ref[i] != be_ref[jnp.maximum(i - 1, 0)])))
    def _():
        w1b_ref[...] = w1_ref[0].astype(BF16)
        w2b_ref[...] = w2_ref[0].astype(BF16)

    def mlp(n):
        h1 = _dot(_unpack_pairs(x_ref[0:n, :]).astype(BF16), w1b_ref[...]) + b1_ref[0]
        glu = jnp.minimum(h1[:, :EXPERT_FF], SWIGLU_LIMIT)
        lin = jnp.clip(h1[:, EXPERT_FF:], -SWIGLU_LIMIT, SWIGLU_LIMIT)
        act = glu * _sigmoid(SWIGLU_ALPHA * glu) * (lin + 1.0)
        o_ref[0:n, :] = _pack_pairs(_dot(act.astype(BF16), w2b_ref[...]) + b2_ref[0])

    @pl.when(rows > half)
    def _():
        mlp(tm)

    @pl.when(jnp.logical_and(rows > 0, rows <= half))
    def _():
        mlp(half)
        o_ref[half:, :] = jnp.zeros((tm - half, o_ref.shape[1]), o_ref.dtype)

    @pl.when(rows == 0)
    def _():
        o_ref[...] = jnp.zeros_like(o_ref)


def _experts(x_slots, block_expert, block_rows, w1, b1, w2, b2, expert_offset):
    n_slots = x_slots.shape[0]
    tm = EXPERT_TILE
    n_blocks = n_slots // tm
    off = expert_offset
    grid_spec = pltpu.PrefetchScalarGridSpec(
        num_scalar_prefetch=2,
        grid=(n_blocks,),
        in_specs=[pl.BlockSpec((tm, D_MODEL // 2), lambda i, be, br: (i, 0)),
                  pl.BlockSpec((1, D_MODEL, 2 * EXPERT_FF), lambda i, be, br: (be[i] + off, 0, 0)),
                  pl.BlockSpec((1, 1, 2 * EXPERT_FF), lambda i, be, br: (be[i], 0, 0)),
                  pl.BlockSpec((1, EXPERT_FF, D_MODEL), lambda i, be, br: (be[i] + off, 0, 0)),
                  pl.BlockSpec((1, 1, D_MODEL), lambda i, be, br: (be[i], 0, 0))],
        out_specs=pl.BlockSpec((tm, D_MODEL // 2), lambda i, be, br: (i, 0)),
        scratch_shapes=[pltpu.VMEM((D_MODEL, 2 * EXPERT_FF), BF16), pltpu.VMEM((EXPERT_FF, D_MODEL), BF16)],
    )
    return pl.pallas_call(
        functools.partial(_expert_kernel, tm=tm),
        grid_spec=grid_spec,
        out_shape=jax.ShapeDtypeStruct((n_slots, D_MODEL // 2), jnp.uint32),
        compiler_params=_cparams(("arbitrary",)),
        name="experts",
    )(block_expert, block_rows, x_slots, w1, b1.reshape(N_EXPERTS, 1, -1), w2, b2.reshape(N_EXPERTS, 1, -1))


COMBINE_TILE = 256
DISPATCH_TILE = 512


def _dispatch_kernel(dest_ref, h_ref, init_ref, xs_ref, sem, *, tm):
    del init_ref

    for r in range(tm):
        for k in range(TOP_K):
            pltpu.make_async_copy(h_ref.at[pl.ds(r, 1)], xs_ref.at[pl.ds(dest_ref[k, r], 1)], sem).start(
                priority=k % 2)
    for k in range(TOP_K):
        pltpu.make_async_copy(h_ref, xs_ref.at[pl.ds(0, tm)], sem).wait()


def _dispatch(h_packed, dest, n_slots, slots_init):
    n, w = h_packed.shape
    if slots_init is None:
        slots_init = jnp.zeros((n_slots, w), jnp.uint32)
    tm = min(DISPATCH_TILE, n)
    return pl.pallas_call(
        functools.partial(_dispatch_kernel, tm=tm),
        grid=(n // tm,),
        in_specs=[pl.BlockSpec((TOP_K, tm), lambda i: (0, i), memory_space=pltpu.SMEM),
                  pl.BlockSpec((tm, w), lambda i: (i, 0)), pl.BlockSpec(memory_space=pl.ANY)],
        out_specs=pl.BlockSpec(memory_space=pl.ANY),
        out_shape=jax.ShapeDtypeStruct((n_slots, w), jnp.uint32),
        scratch_shapes=[pltpu.SemaphoreType.DMA(())],
        input_output_aliases={2: 0},
        compiler_params=_cparams(("arbitrary",)),
        name="dispatch",
    )(dest, h_packed, slots_init)


def _combine_kernel(d_ref, h_ref, w_ref, lnw_ref, lnb_ref, ys_ref, o_ref, buf_ref, sem, *, tm):
    i = pl.program_id(0)
    n_tiles = pl.num_programs(0) - 1

    def issue(slot):
        for r in range(tm):
            for k in range(TOP_K):
                pltpu.make_async_copy(ys_ref.at[pl.ds(d_ref[k, r], 1)], buf_ref.at[slot, k, pl.ds(r, 1)],
                                      sem.at[slot]).start(priority=k % 2)

    def finish(slot):
        for k in range(TOP_K):
            pltpu.make_async_copy(ys_ref.at[pl.ds(0, tm)], buf_ref.at[slot, k], sem.at[slot]).wait()
        acc = DEEPNORM_ALPHA * h_ref[...]
        for k in range(TOP_K):
            acc = acc + w_ref[:, k:k + 1] * _unpack_pairs(buf_ref[slot, k])
        o_ref[...] = _layer_norm(acc, lnw_ref[...], lnb_ref[...])

    for parity in range(2):
        @pl.when(jnp.logical_and(i < n_tiles, i % 2 == parity))
        def _():
            issue(parity)

        @pl.when(jnp.logical_and(i > 0, i % 2 == parity))
        def _():
            finish(1 - parity)


def _combine(h, y_slots, dest, w_tok, ln_w, ln_b):
    n = h.shape[0]
    tm = min(COMBINE_TILE, n)
    n_tiles = n // tm
    lnw = ln_w.reshape(1, -1)
    lnb = ln_b.reshape(1, -1)
    prev = lambda i: jnp.maximum(i - 1, 0)
    return pl.pallas_call(
        functools.partial(_combine_kernel, tm=tm),
        grid=(n_tiles + 1,),
        in_specs=[pl.BlockSpec((TOP_K, tm), lambda i: (0, jnp.minimum(i, n_tiles - 1)), memory_space=pltpu.SMEM),
                  pl.BlockSpec((tm, D_MODEL), lambda i: (prev(i), 0)),
                  pl.BlockSpec((tm, TOP_K), lambda i: (prev(i), 0)),
                  pl.BlockSpec(lnw.shape, lambda i: (0, 0)), pl.BlockSpec(lnb.shape, lambda i: (0, 0)),
                  pl.BlockSpec(memory_space=pl.ANY)],
        out_specs=pl.BlockSpec((tm, D_MODEL), lambda i: (prev(i), 0)),
        out_shape=jax.ShapeDtypeStruct((n, D_MODEL), F32),
        scratch_shapes=[pltpu.VMEM((2, TOP_K, tm, D_MODEL // 2), jnp.uint32), pltpu.SemaphoreType.DMA((2,))],
        compiler_params=_cparams(("arbitrary",)),
        name="combine",
    )(dest, h, w_tok, lnw, lnb, y_slots)


def _moe(h, h16, logits_t, w1, b1, w2, b2, ln_w, ln_b, expert_offset, slots_init):
    n = h.shape[0]
    tm = EXPERT_TILE
    idx8, w8, rank8, cnt = _route(logits_t)
    idx, w_top, rank = idx8[:TOP_K], w8[:TOP_K], rank8[:TOP_K]
    counts = cnt[:, 0].astype(jnp.int32)
    padded = ((counts + tm - 1) // tm) * tm
    p_end = jnp.cumsum(padded)
    p_start = p_end - padded
    experts = jnp.arange(N_EXPERTS, dtype=jnp.int32)
    seg_start = jnp.sum(jnp.where(idx[:, :, None] == experts, p_start, 0), axis=-1)
    dest = seg_start + rank
    n_slots = n * TOP_K + N_EXPERTS * tm
    n_blocks = n_slots // tm
    starts = jnp.arange(n_blocks, dtype=jnp.int32) * tm
    block_expert = jnp.minimum(jnp.sum((p_end[None, :] <= starts[:, None]).astype(jnp.int32), axis=1),
                               N_EXPERTS - 1)
    seg_stop = p_start + counts
    block_stop = jnp.sum(jnp.where(block_expert[:, None] == experts, seg_stop, 0), axis=-1)
    block_rows = jnp.clip(block_stop - starts, 0, tm).astype(jnp.int32)
    x_slots = _dispatch(h16, dest, n_slots, slots_init)
    y_slots = _experts(x_slots, block_expert, block_rows, w1, b1, w2, b2, expert_offset)
    return _combine(h, y_slots, dest, w_top.T, ln_w, ln_b), x_slots


def _mixer_weight(w_in):
    o = np.cumsum((512, 512, 512, 512, MLA_Q_RANK, MLA_KV_RANK + MLA_ROPE, S5_WIDTH, LRU_WIDTH, LRU_WIDTH))
    hgrn, cq, ckv = w_in[:, :o[3]], w_in[:, o[3]:o[4]], w_in[:, o[4]:o[4] + MLA_KV_RANK]
    kpe = w_in[:, o[4] + MLA_KV_RANK:o[5]]
    su, lx, lg = w_in[:, o[5]:o[6]], w_in[:, o[6]:o[7]], w_in[:, o[7]:o[8]]
    pad = jnp.zeros((D_MODEL, 128 - MLA_ROPE), w_in.dtype)
    w_mix = jnp.concatenate([hgrn, su, lx, lg, cq, ckv, kpe, pad, _rot_half_cols(kpe), pad], axis=1)
    return w_mix, w_in[:, o[8]:]


def _layer(x, positions, lb, bsz, seq, w_in, b_gate, hgrn_norm_w, mla_q_norm_w, mla_wq_b, mla_kv_norm_w, mla_wkv_b,
           s5_a_re, s5_a_im, s5_log_dt, s5_b_re, s5_b_im, s5_c_re, s5_c_im, s5_d, s5_w_glu, s5_b_glu,
           lru_conv_w, lru_conv_b, lru_wa, lru_ba, lru_wx, lru_bx, lru_a_param,
           w_branch, w_out, ln1_w, ln1_b, ln2_w, ln2_b, router_w, router_b, moe_w1, moe_b1, moe_w2, moe_b2,
           expert_offset=0, slots_init=None):
    w_mix, w_gl = _mixer_weight(w_in)
    p = _matmul(x, w_mix.astype(BF16), 2048, MIX_WIDTH // 3, F32)
    gl = _matmul(x, w_gl.astype(BF16), 2048, 1024, BF16)
    y_a = _hgrn(p, lb, hgrn_norm_w, bsz, seq)
    qn, qp, kn, kp, vt = _mla_proj(p, positions, mla_q_norm_w, mla_wq_b, mla_kv_norm_w, mla_wkv_b, min(256, seq))
    y_b = _flash(qn, qp, kn, kp, vt, bsz, seq)
    y_s = _s5(p, s5_a_re, s5_a_im, s5_log_dt, s5_b_re, s5_b_im, s5_c_re, s5_c_im, s5_d, bsz, seq)
    y_d = _lru(p, lru_conv_w, lru_conv_b, lru_wa, lru_ba, lru_wx, lru_bx, lru_a_param, bsz, seq)
    h, h16, logits_t = _merge(y_a, y_b, y_s, y_d, gl, x, w_branch, w_out, s5_w_glu, s5_b_glu, b_gate,
                              ln1_w, ln1_b, router_w, router_b)
    return _moe(h, h16, logits_t, moe_w1, moe_b1, moe_w2, moe_b2, ln2_w, ln2_b, expert_offset, slots_init)


def kernel(x, positions, w_in, b_gate, hgrn_lb_logits, hgrn_norm_w, mla_q_norm_w, mla_wq_b, mla_kv_norm_w, mla_wkv_b, s5_a_re, s5_a_im, s5_log_dt, s5_b_re, s5_b_im, s5_c_re, s5_c_im, s5_d, s5_w_glu, s5_b_glu, lru_conv_w, lru_conv_b, lru_wa, lru_ba, lru_wx, lru_bx, lru_a_param, w_branch, w_out, ln1_w, ln1_b, ln2_w, ln2_b, router_w, router_b, moe_w1, moe_b1, moe_w2, moe_b2):
    bsz, seq, _ = x.shape
    probs = jax.nn.softmax(hgrn_lb_logits.astype(F32), axis=0)
    lower_bounds = jnp.cumsum(probs, axis=0) - probs[0:1]
    per_layer = (w_in, b_gate, hgrn_norm_w, mla_q_norm_w, mla_wq_b, mla_kv_norm_w, mla_wkv_b,
                 s5_a_re, s5_a_im, s5_log_dt, s5_b_re, s5_b_im, s5_c_re, s5_c_im, s5_d, s5_w_glu, s5_b_glu,
                 lru_conv_w, lru_conv_b, lru_wa, lru_ba, lru_wx, lru_bx, lru_a_param,
                 w_branch, w_out, ln1_w, ln1_b, ln2_w, ln2_b, router_w, router_b, moe_w1, moe_b1, moe_w2, moe_b2)
    xf = x.reshape(bsz * seq, D_MODEL)
    w1_all = moe_w1.reshape((DEPTH * N_EXPERTS,) + moe_w1.shape[2:])
    w2_all = moe_w2.reshape((DEPTH * N_EXPERTS,) + moe_w2.shape[2:])
    slots = None
    for l in range(DEPTH):
        args = [a[l] for a in per_layer]
        args[-4], args[-2] = w1_all, w2_all
        xf, slots = _layer(xf, positions, lower_bounds[l], bsz, seq, *args, expert_offset=l * N_EXPERTS,
                           slots_init=slots)
    return xf.reshape(bsz, seq, D_MODEL)
```

```python
import functools
import math

import numpy as np
import jax
import jax.numpy as jnp
from jax import lax
from jax.experimental import pallas as pl
from jax.experimental.pallas import tpu as pltpu

F32 = jnp.float32
BF16 = jnp.bfloat16

D_MODEL = 1024
DEPTH = 2
CHUNK = 64

HGRN_HEADS = 4
HGRN_DK = 128
HGRN_WIDTH = 512
HGRN_F_MIN = 1e-30
HGRN_CHUNK = 64
HGRN_LEVELS = (1, 2, 4, 8, 16, 32)
HGRN_PAR = 4

MLA_HEADS = 4
MLA_Q_RANK = 256
MLA_KV_RANK = 128
MLA_NOPE = 128
MLA_ROPE = 64
MLA_V = 128
MLA_QK = MLA_NOPE + MLA_ROPE
MLA_VROWS = MLA_V + 16
ROPE_THETA = 10000.0

S5_GROUPS = 32
S5_GROUP_CH = 16
S5_STATE = 64
S5_WIDTH = 512
S5_BLOCK = 16
S5_PAR = 8

LRU_WIDTH = 512
LRU_BLOCKS = 8
LRU_BLOCK_W = 64
LRU_CONV = 4
LRU_C = 8.0

N_BRANCH = 4
BRANCH_WIDTH = 512

N_EXPERTS = 32
TOP_K = 4
EXPERT_FF = 1024
SWIGLU_ALPHA = 1.702
SWIGLU_LIMIT = 7.0
EXPERT_TILE = 1024

DEEPNORM_ALPHA = (2.0 * DEPTH) ** 0.25
LN_EPS = 1e-5
RMS_EPS = 1e-6

COL_HQ, COL_HF, COL_HI, COL_HG = 0, 512, 1024, 1536
COL_SU, COL_LX, COL_LG = 2048, 2560, 3072
COL_CQ, COL_CKV, COL_KPE, COL_KPER = 3584, 3840, 3968, 4096
MIX_WIDTH = 4224

VMEM_LIMIT = 56 * 1024 * 1024


def _cparams(sem):
    return pltpu.CompilerParams(dimension_semantics=sem, vmem_limit_bytes=VMEM_LIMIT)


def _dot(a, b):
    return jnp.dot(a, b, preferred_element_type=F32)


def _dot_nt(a, b):
    return lax.dot_general(a, b, (((1,), (1,)), ((), ())), preferred_element_type=F32)


def _dot_tn(a, b):
    return lax.dot_general(a, b, (((0,), (0,)), ((), ())), preferred_element_type=F32)


def _split(x):
    hi = x.astype(BF16)
    lo = (x - hi.astype(F32)).astype(BF16)
    return hi, lo


def _dot3(a, b_hi, b_lo):
    a_hi, a_lo = _split(a)
    return _dot(a_hi, b_hi) + (_dot(a_lo, b_hi) + _dot(a_hi, b_lo))


def _pack_pairs(x):
    w = x.shape[-1] // 2
    lo = lax.bitcast_convert_type(x[:, :w].astype(BF16).astype(F32), jnp.uint32)
    hi = lax.bitcast_convert_type(x[:, w:].astype(BF16).astype(F32), jnp.uint32)
    return lax.shift_right_logical(lo, jnp.uint32(16)) | (hi & jnp.uint32(0xFFFF0000))


def _unpack_pairs(u):
    lo = lax.bitcast_convert_type(lax.shift_left(u, jnp.uint32(16)), F32)
    hi = lax.bitcast_convert_type(u & jnp.uint32(0xFFFF0000), F32)
    return jnp.concatenate([lo, hi], axis=-1)


def _sigmoid(x):
    return 0.5 * jnp.tanh(0.5 * x) + 0.5


def _silu(x):
    return x * _sigmoid(x)


def _mm_kernel(x_ref, w_ref, o_ref):
    o_ref[...] = _dot(x_ref[...].astype(BF16), w_ref[...]).astype(o_ref.dtype)


def _matmul(x, w, tm, tn, out_dtype):
    n, k = x.shape
    m = w.shape[1]
    tm = min(tm, n)
    return pl.pallas_call(
        _mm_kernel,
        grid=(n // tm, m // tn),
        in_specs=[pl.BlockSpec((tm, k), lambda i, j: (i, 0)),
                  pl.BlockSpec((k, tn), lambda i, j: (0, j))],
        out_specs=pl.BlockSpec((tm, tn), lambda i, j: (i, j)),
        out_shape=jax.ShapeDtypeStruct((n, m), out_dtype),
        compiler_params=_cparams(("parallel", "arbitrary")),
        name="in_proj",
    )(x, w)


def _hgrn_tables():
    c = HGRN_CHUNK
    t = np.arange(c)
    windows = [np.tril(np.ones((c, c), bool)),
               np.triu(np.ones((c, c), bool), 1)]
    masks = [np.eye(c, dtype=bool)]
    for h in HGRN_LEVELS:
        blk = t // h
        odd = blk % 2 == 1
        masks.append(odd[:, None] & (blk[None, :] == blk[:, None] - 1))
        if h > 1:
            windows.append((t[None, :] >= (h * blk)[:, None]) & (t[None, :] <= t[:, None]))
            windows.append((t[None, :] > t[:, None]) & (t[None, :] <= (h * blk + h - 1)[:, None]))
    return np.concatenate(windows, 0).astype(np.float32), np.stack(masks).astype(np.float32)


def _hgrn_kernel(hq_ref, hf_ref, hi_ref, hg_ref, lb_ref, nw_ref, win_ref, mask_ref,
                 o_ref, state_ref, *, n_chunks):
    c = HGRN_CHUNK
    heads = range(HGRN_HEADS)
    hs = [slice(h * HGRN_DK, (h + 1) * HGRN_DK) for h in heads]
    n_lvl = len(HGRN_LEVELS) + 1

    @pl.when(pl.program_id(1) == 0)
    def _():
        state_ref[...] = jnp.zeros_like(state_ref)

    def group_body(gi, carry):
        par = range(HGRN_PAR)
        rows = [pl.ds(pl.multiple_of((gi * HGRN_PAR + cc) * c, c), c) for cc in par]
        lb = lb_ref[...]
        z = [hf_ref[rows[cc], :] for cc in par]
        th = [jnp.tanh(0.5 * z[cc]) for cc in par]
        f = [jnp.maximum(lb + (1.0 - lb) * (0.5 + 0.5 * th[cc]), HGRN_F_MIN) for cc in par]
        k = [(1.0 - lb) * (0.5 - 0.5 * th[cc]) for cc in par]
        q = [_silu(hq_ref[rows[cc], :]) for cc in par]
        v16 = [hi_ref[rows[cc], :].astype(BF16) for cc in par]
        lf = [_split(jnp.log(f[cc])) for cc in par]
        lf_stack = jnp.concatenate([jnp.concatenate([lf[cc][part] for cc in par], axis=-1) for part in range(2)],
                                   axis=0)
        e_all = jnp.exp(_dot(win_ref[...], lf_stack))
        e = [e_all[:, cc * HGRN_WIDTH:(cc + 1) * HGRN_WIDTH] for cc in par]
        blk = lambda cc, w: e[cc][w * c:(w + 1) * c]
        qs, ks, q_in, k_out, decay = [], [], [], [], []
        for cc in par:
            k16 = k[cc].astype(BF16)
            qs.append([q[cc].astype(BF16), (q[cc] * f[cc]).astype(BF16)]
                      + [(q[cc] * blk(cc, 2 * w)).astype(BF16) for w in range(1, n_lvl - 1)])
            ks.append([k16, k16] + [(k[cc] * blk(cc, 2 * w + 1)).astype(BF16) for w in range(1, n_lvl - 1)])
            q_in.append((q[cc] * blk(cc, 0)).astype(BF16))
            k_out.append((k[cc] * blk(cc, 1)).astype(BF16))
            decay.append(e[cc][c - 1:c, :])
        scores = [[None] * HGRN_HEADS for _ in par]
        for lv in range(n_lvl):
            s_l = [[_dot_nt(qs[cc][lv][:, hs[h]], ks[cc][lv][:, hs[h]]) for h in heads] for cc in par]
            for cc in par:
                for h in heads:
                    term = mask_ref[lv] * s_l[cc][h]
                    scores[cc][h] = term if scores[cc][h] is None else scores[cc][h] + term
        intra = [[_dot(scores[cc][h].astype(BF16), v16[cc][:, hs[h]]) for h in heads] for cc in par]
        upd = [[_dot_tn(v16[cc][:, hs[h]], k_out[cc][:, hs[h]]) for h in heads] for cc in par]
        st = [state_ref[h] for h in heads]
        for cc in par:
            o = [intra[cc][h] + _dot_nt(q_in[cc][:, hs[h]], st[h].astype(BF16)) for h in heads]
            st = [st[h] * decay[cc][:, hs[h]] + upd[cc][h] for h in heads]
            o = [o[h] * lax.rsqrt(jnp.mean(o[h] * o[h], axis=-1, keepdims=True) + RMS_EPS) for h in heads]
            o_ref[rows[cc], :] = jnp.concatenate(o, axis=-1) * nw_ref[...] * _silu(hg_ref[rows[cc], :])
        for h in heads:
            state_ref[h] = st[h]
        return carry

    lax.fori_loop(0, n_chunks // HGRN_PAR, group_body, 0)


def _hgrn(p, lb, norm_w, bsz, seq):
    t = min(1024, seq)
    nt = seq // t
    win, mask = _hgrn_tables()
    col = lambda cb: pl.BlockSpec((t, HGRN_WIDTH), lambda b, i: (b * nt + i, cb))
    const2 = lambda shape: pl.BlockSpec(shape, lambda b, i: (0,) * len(shape))
    return pl.pallas_call(
        functools.partial(_hgrn_kernel, n_chunks=t // HGRN_CHUNK),
        grid=(bsz, nt),
        in_specs=[col(COL_HQ // 512), col(COL_HF // 512), col(COL_HI // 512), col(COL_HG // 512),
                  const2((1, HGRN_WIDTH)), const2((1, HGRN_WIDTH)),
                  const2((win.shape[0], 2 * win.shape[1])), const2(mask.shape)],
        out_specs=pl.BlockSpec((t, HGRN_WIDTH), lambda b, i: (b * nt + i, 0)),
        out_shape=jax.ShapeDtypeStruct((bsz * seq, HGRN_WIDTH), F32),
        scratch_shapes=[pltpu.VMEM((HGRN_HEADS, HGRN_DK, HGRN_DK), F32)],
        compiler_params=_cparams(("parallel", "arbitrary")),
        name="hgrn",
    )(p, p, p, p, lb.reshape(1, -1), norm_w.reshape(1, -1), jnp.asarray(np.concatenate([win, win], 1), BF16),
      jnp.asarray(mask))


def _rms(x, w):
    ms = jnp.mean(x * x, axis=-1, keepdims=True)
    return x * lax.rsqrt(ms + RMS_EPS) * w


def _mla_proj_kernel(cq_ref, ckv_ref, kpe_ref, kper_ref, ang_ref, qnw_ref, kvnw_ref,
                     wqn_ref, wqp_ref, wqpr_ref, wkn_ref, wv_ref,
                     qn_ref, qp_ref, kn_ref, kp_ref, v_ref):
    ang = ang_ref[...]
    cos = jnp.cos(ang)
    sin = jnp.sin(ang)
    qn = _rms(cq_ref[...], qnw_ref[...]).astype(BF16)
    cos4 = jnp.concatenate([cos] * MLA_HEADS, axis=-1)
    sin4 = jnp.concatenate([sin] * MLA_HEADS, axis=-1)
    qn_ref[...] = _dot(qn, wqn_ref[...]).astype(BF16)
    qp_ref[...] = (_dot(qn, wqp_ref[...]) * cos4 + _dot(qn, wqpr_ref[...]) * sin4).astype(BF16)
    cn = _rms(ckv_ref[...], kvnw_ref[...]).astype(BF16)
    kn_ref[...] = _dot(cn, wkn_ref[...]).astype(BF16)
    vt = _dot_nt(wv_ref[...], cn)
    ones = jnp.ones((MLA_VROWS - MLA_V, vt.shape[1]), F32)
    v_ref[0] = jnp.concatenate(
        [piece for h in range(MLA_HEADS) for piece in (vt[h * MLA_V:(h + 1) * MLA_V], ones)], axis=0).astype(BF16)
    kp_ref[...] = (kpe_ref[...] * cos + kper_ref[...] * sin).astype(BF16)


def _rot_half_cols(w):
    half = w.shape[-1] // 2
    return jnp.concatenate([-w[..., half:], w[..., :half]], axis=-1)


def _mla_proj(p, positions, q_norm_w, wq_b, kv_norm_w, wkv_b, tm):
    n = p.shape[0]
    scale = MLA_QK ** -0.5 * math.log2(math.e)
    wq = wq_b.reshape(MLA_Q_RANK, MLA_HEADS, MLA_QK) * scale
    w_nope = wq[:, :, :MLA_NOPE].reshape(MLA_Q_RANK, MLA_HEADS * MLA_NOPE)
    w_pe = wq[:, :, MLA_NOPE:]
    pad = jnp.zeros((MLA_Q_RANK, MLA_HEADS, 128 - MLA_ROPE), F32)
    w_pe_p = jnp.concatenate([w_pe, pad], axis=-1).reshape(MLA_Q_RANK, MLA_HEADS * 128)
    w_per_p = jnp.concatenate([_rot_half_cols(w_pe), pad], axis=-1).reshape(MLA_Q_RANK, MLA_HEADS * 128)
    wkv = wkv_b.reshape(MLA_KV_RANK, MLA_HEADS, MLA_NOPE + MLA_V)
    w_kn = wkv[:, :, :MLA_NOPE].reshape(MLA_KV_RANK, MLA_HEADS * MLA_NOPE)
    w_v = wkv[:, :, MLA_NOPE:].reshape(MLA_KV_RANK, MLA_HEADS * MLA_V)
    inv_freq = ROPE_THETA ** (-jnp.arange(0, MLA_ROPE, 2, dtype=F32) / MLA_ROPE)
    freq_lane = jnp.concatenate([inv_freq, inv_freq, jnp.zeros((128 - MLA_ROPE,), F32)])
    ang = positions.reshape(n, 1).astype(F32) * freq_lane[None, :]

    row = lambda w, cb: pl.BlockSpec((tm, w), lambda i: (i, cb))
    full = lambda a: pl.BlockSpec(a.shape, lambda i: (0,) * a.ndim)
    weights = [w_nope.astype(BF16), w_pe_p.astype(BF16), w_per_p.astype(BF16), w_kn.astype(BF16),
               w_v.T.astype(BF16)]
    qnw = q_norm_w.reshape(1, -1)
    kvnw = kv_norm_w.reshape(1, -1)
    outs = pl.pallas_call(
        _mla_proj_kernel,
        grid=(n // tm,),
        in_specs=[row(256, COL_CQ // 256), row(128, COL_CKV // 128), row(128, COL_KPE // 128),
                  row(128, COL_KPER // 128), row(128, 0), full(qnw), full(kvnw)] + [full(w) for w in weights],
        out_specs=[row(512, 0), row(512, 0), row(512, 0), row(128, 0),
                   pl.BlockSpec((1, MLA_HEADS * MLA_VROWS, tm), lambda i: (i, 0, 0))],
        out_shape=[jax.ShapeDtypeStruct((n, 512), BF16), jax.ShapeDtypeStruct((n, 512), BF16),
                   jax.ShapeDtypeStruct((n, 512), BF16), jax.ShapeDtypeStruct((n, 128), BF16),
                   jax.ShapeDtypeStruct((n // tm, MLA_HEADS * MLA_VROWS, tm), BF16)],
        compiler_params=_cparams(("parallel",)),
        name="mla_proj",
    )(p, p, p, p, ang, qnw, kvnw, *weights)
    return outs


def _flash_kernel(qn_ref, qp_ref, kn_ref, kp_ref, vt_ref, o_ref, acc_ref, *, tq):
    i = pl.program_id(1)
    heads = range(MLA_HEADS)
    hs = [slice(h * 128, (h + 1) * 128) for h in heads]
    vs = [slice(h * MLA_VROWS, (h + 1) * MLA_VROWS) for h in heads]
    acc_ref[...] = jnp.zeros_like(acc_ref)

    def scores(j):
        r0 = pl.multiple_of(j * tq, tq)
        kp = kp_ref[pl.ds(r0, tq), :]
        return tuple(_dot_nt(jnp.concatenate([kn_ref[pl.ds(r0, tq), hs[h]], kp], axis=-1),
                             jnp.concatenate([qn_ref[:, hs[h]], qp_ref[:, hs[h]]], axis=-1)) for h in heads)

    def consume(j, s, m, mask):
        p, m_new, alpha = [], [], []
        for h in heads:
            sh = s[h] if mask is None else jnp.where(mask, s[h], -1e30)
            mn = jnp.maximum(m[h], jnp.max(sh, axis=0, keepdims=True))
            alpha.append(jnp.exp2(m[h] - mn))
            p.append(jnp.exp2(sh - mn).astype(BF16))
            m_new.append(mn)
        vt = vt_ref[j]
        pv = [_dot(vt[vs[h], :], p[h]) for h in heads]
        for h in heads:
            acc_ref[h] = acc_ref[h] * alpha[h] + pv[h]
        return tuple(m_new)

    def body(j, carry):
        s, m = carry
        s_next = scores(j + 1)
        return s_next, consume(j, s, m, None)

    m0 = tuple(jnp.full((1, tq), -1e30, F32) for _ in heads)
    s, m = lax.fori_loop(0, i, body, (scores(0), m0))
    shift = CHUNK.bit_length() - 1
    kc = lax.shift_right_logical(lax.broadcasted_iota(jnp.int32, (tq, tq), 0), shift)
    qc = lax.shift_right_logical(lax.broadcasted_iota(jnp.int32, (tq, tq), 1), shift)
    consume(i, s, m, kc <= qc)
    for h in heads:
        acc = acc_ref[h]
        o_ref[:, hs[h]] = (acc[:MLA_V] / acc[MLA_V:MLA_V + 1]).T.astype(o_ref.dtype)


def _flash(qn, qp, kn, kp, vt, bsz, seq):
    tq = vt.shape[2]
    nq = seq // tq
    width = MLA_HEADS * 128
    qspec = pl.BlockSpec((tq, width), lambda b, i: (b * nq + i, 0))
    kspec = pl.BlockSpec((seq, width), lambda b, i: (b, 0))
    kpspec = pl.BlockSpec((seq, 128), lambda b, i: (b, 0))
    vspec = pl.BlockSpec((nq, MLA_HEADS * MLA_VROWS, tq), lambda b, i: (b, 0, 0))
    return pl.pallas_call(
        functools.partial(_flash_kernel, tq=tq),
        grid=(bsz, nq),
        in_specs=[qspec, qspec, kspec, kpspec, vspec],
        out_specs=qspec,
        out_shape=jax.ShapeDtypeStruct((bsz * seq, width), BF16),
        scratch_shapes=[pltpu.VMEM((MLA_HEADS, MLA_VROWS, tq), F32)],
        compiler_params=_cparams(("parallel", "arbitrary")),
        name="flash",
    )(qn, qp, kn, kp, vt)


def _cmul(a, b):
    return a[0] * b[0] - a[1] * b[1], a[0] * b[1] + a[1] * b[0]


def _s5_tables(a_re, a_im, log_dt, b_re, b_im, c_re, c_im, d_skip, steps):
    lb = S5_BLOCK
    f32 = lambda v: v.astype(F32)
    a_re, a_im, b_re, b_im, c_re, c_im = map(f32, (a_re, a_im, b_re, b_im, c_re, c_im))
    dt = jnp.exp(f32(log_dt))[:, None]
    ld = (a_re * dt, a_im * dt)

    def power(tau):
        mag = jnp.exp(ld[0] * tau)
        return mag * jnp.cos(ld[1] * tau), mag * jnp.sin(ld[1] * tau)

    lam_bar = power(1.0)
    inv = a_re * a_re + a_im * a_im
    ratio = _cmul((lam_bar[0] - 1.0, lam_bar[1]), (a_re / inv, -a_im / inv))
    b_bar = _cmul((ratio[0][..., None], ratio[1][..., None]), (b_re, b_im))
    tau = jnp.arange(lb + 1, dtype=F32)[:, None, None]
    pw = power(tau)
    cp = _cmul((c_re[None], c_im[None]), (pw[0][:, :, None, :], pw[1][:, :, None, :]))
    kern = jnp.einsum('tgcn,gnd->tgcd', cp[0][:lb], b_bar[0]) - jnp.einsum('tgcn,gnd->tgcd', cp[1][:lb], b_bar[1])
    s_idx = np.arange(lb)[:, None]
    t_idx = np.arange(lb)[None, :]
    lag = np.clip(t_idx - s_idx, 0, lb - 1)
    causal = jnp.asarray((t_idx >= s_idx).astype(np.float32))
    toe = kern[lag] * causal[:, :, None, None, None]
    skip = jnp.asarray(np.eye(lb, dtype=np.float32))[:, :, None, None, None] * (
        d_skip.astype(F32)[None, None, :, :, None] * jnp.eye(S5_GROUP_CH, dtype=F32)[None, None, None])
    toe = (toe + skip).transpose(2, 1, 3, 0, 4).reshape(S5_GROUPS, lb * S5_GROUP_CH, lb * S5_GROUP_CH)
    emit = jnp.concatenate([cp[0][1:], -cp[1][1:]], axis=-1)
    emit = emit.transpose(1, 0, 2, 3).reshape(S5_GROUPS, lb * S5_GROUP_CH, 2 * S5_STATE)
    rev = (pw[0][:lb][::-1][..., None], pw[1][:lb][::-1][..., None])
    bp = _cmul(rev, (b_bar[0][None], b_bar[1][None]))
    fold = jnp.concatenate([bp[0], bp[1]], axis=2)
    fold = fold.transpose(1, 2, 0, 3).reshape(S5_GROUPS, 2 * S5_STATE, lb * S5_GROUP_CH)
    jump = (lb * 2.0 ** jnp.arange(steps, dtype=F32))[:, None, None]
    pj = power(jump)
    pw_re = jnp.concatenate([pj[0], pj[0]], axis=-1).transpose(1, 2, 0)
    pw_im = jnp.concatenate([-pj[1], pj[1]], axis=-1).transpose(1, 2, 0)
    return toe, emit, fold, pw_re, pw_im


def _s5_kernel(su0_ref, su1_ref, su2_ref, su3_ref, toe_ref, emit_ref, fold_ref, pre_ref, pim_ref, y_ref,
               ut_ref, carry_ref, *, r, steps):
    lb = S5_BLOCK
    gc = S5_GROUP_CH

    @pl.when(pl.program_id(1) == 0)
    def _():
        carry_ref[...] = jnp.zeros_like(carry_ref)

    for s in range(lb):
        for c, su_ref in enumerate((su0_ref, su1_ref, su2_ref, su3_ref)):
            ut_ref[s, c * 128:(c + 1) * 128, :] = su_ref[pl.ds(s, r, stride=lb), :].T
    lane = lax.broadcasted_iota(jnp.int32, (2 * S5_STATE, r), 1)
    swap = lambda v: jnp.concatenate([v[S5_STATE:], v[:S5_STATE]], axis=0)

    def groups(gi, carry):
        par = range(S5_PAR)
        g = [gi * S5_PAR + a for a in par]
        g0 = [pl.multiple_of(g[a] * gc, gc) for a in par]
        ug = [jnp.concatenate([ut_ref[s, pl.ds(g0[a], gc), :] for s in range(lb)], axis=0).astype(BF16)
              for a in par]
        pre = [pre_ref[g[a]] for a in par]
        pim = [pim_ref[g[a]] for a in par]
        cmul = lambda a, j, v: pre[a][:, j:j + 1] * v + pim[a][:, j:j + 1] * swap(v)
        car = [carry_ref[g[a]] for a in par]
        x = [_dot(fold_ref[g[a]], ug[a]) for a in par]
        x = [x[a] + jnp.where(lane == 0, cmul(a, 0, car[a]), 0.0) for a in par]
        for j in range(steps):
            xs = [jnp.where(lane >= 2 ** j, pltpu.roll(x[a], 2 ** j, 1), 0.0) for a in par]
            x = [x[a] + cmul(a, j, xs[a]) for a in par]
        x_prev = [jnp.where(lane == 0, car[a], pltpu.roll(x[a], 1, 1)) for a in par]
        y = [_dot(toe_ref[g[a]], ug[a]) + _dot(emit_ref[g[a]], x_prev[a].astype(BF16)) for a in par]
        for a in par:
            carry_ref[g[a]] = jnp.broadcast_to(x[a][:, r - 1:r], x[a].shape)
            for t in range(lb):
                ut_ref[t, pl.ds(g0[a], gc), :] = y[a][t * gc:(t + 1) * gc]
        return carry

    lax.fori_loop(0, S5_GROUPS // S5_PAR, groups, 0)
    for t in range(lb):
        for c in range(S5_WIDTH // 128):
            y_ref[c, pl.ds(t, r, stride=lb), :] = ut_ref[t, c * 128:(c + 1) * 128, :].T


def _s5(p, a_re, a_im, log_dt, b_re, b_im, c_re, c_im, d_skip, bsz, seq):
    lb = S5_BLOCK
    r = min(128, seq // lb)
    nt = seq // (lb * r)
    ncb = S5_WIDTH // 128
    steps = max(1, int(math.ceil(math.log2(r))))
    toe, emit, fold, pw_re, pw_im = _s5_tables(a_re, a_im, log_dt, b_re, b_im, c_re, c_im, d_skip, steps)
    consts = [toe.astype(BF16), emit.astype(BF16), fold.astype(BF16), pw_re, pw_im]
    full = lambda a: pl.BlockSpec(a.shape, lambda b, t: (0,) * a.ndim)
    return pl.pallas_call(
        functools.partial(_s5_kernel, r=r, steps=steps),
        grid=(bsz, nt),
        in_specs=[pl.BlockSpec((lb * r, 128), functools.partial(lambda b, t, c: (b * nt + t, COL_SU // 128 + c), c=c))
                  for c in range(ncb)] + [full(c) for c in consts],
        out_specs=pl.BlockSpec((ncb, lb * r, 128), lambda b, t: (0, b * nt + t, 0)),
        out_shape=jax.ShapeDtypeStruct((ncb, bsz * seq, 128), F32),
        scratch_shapes=[pltpu.VMEM((lb, S5_WIDTH, r), F32), pltpu.VMEM((S5_GROUPS, 2 * S5_STATE, r), F32)],
        compiler_params=_cparams(("parallel", "arbitrary")),
        name="s5",
    )(*([p] * ncb), *consts)


def _lru_kernel(lx_ref, lg_ref, cw_ref, cb_ref, wg_ref, bg_ref, sp_ref, o_ref, tail_ref, h_ref, *, ts):
    @pl.when(pl.program_id(1) == 0)
    def _():
        tail_ref[...] = jnp.zeros_like(tail_ref)
        h_ref[...] = jnp.zeros_like(h_ref)

    x = lx_ref[...]
    tail_ref[8:, :] = x
    xc = cb_ref[...] + cw_ref[LRU_CONV - 1:LRU_CONV, :] * x
    for j in range(1, LRU_CONV):
        xc = xc + cw_ref[LRU_CONV - 1 - j:LRU_CONV - j, :] * tail_ref[8 - j:8 - j + ts, :]
    tail_ref[0:8, :] = x[ts - 8:, :]
    gates = _sigmoid(_dot3(xc, wg_ref[0], wg_ref[1]) + bg_ref[...])
    r = gates[:, :LRU_WIDTH]
    ig = gates[:, LRU_WIDTH:]
    log_a = -LRU_C * r * sp_ref[...]
    a = jnp.exp(log_a)
    gap = jnp.maximum(1.0 - a * a, 0.0)
    b = jnp.where(gap > 0.0, gap * lax.rsqrt(gap), 0.0) * (ig * xc)
    sub = lax.broadcasted_iota(jnp.int32, a.shape, 0) & 7
    for k in (1, 2, 4):
        inside = sub >= k
        b = b + a * jnp.where(inside, pltpu.roll(b, k, 0), 0.0)
        a = a * jnp.where(inside, pltpu.roll(a, k, 0), 1.0)
    gate = jax.nn.gelu(lg_ref[...])
    h_in = h_ref[0:1, :]
    for g in range(ts // 8):
        rows = slice(8 * g, 8 * g + 8)
        h = b[rows] + a[rows] * h_in
        o_ref[rows, :] = h * gate[rows]
        h_in = h[7:8, :]
    h_ref[0:1, :] = h_in


def _block_diag(w):
    nb, bw, _ = w.shape
    eye = jnp.eye(nb, dtype=w.dtype)
    return (eye[:, None, :, None] * w[:, :, None, :]).reshape(nb * bw, nb * bw)


def _lru(p, conv_w, conv_b, wa, ba, wx, bx, a_param, bsz, seq):
    ts = min(512, seq)
    nt = seq // ts
    wg = jnp.concatenate([_block_diag(wa), _block_diag(wx)], axis=1)
    wg_hi = wg.astype(BF16)
    wg2 = jnp.stack([wg_hi, (wg - wg_hi.astype(F32)).astype(BF16)])
    bg = jnp.concatenate([ba, bx]).reshape(1, -1)
    sp = jax.nn.softplus(a_param.astype(F32)).reshape(1, -1)
    col = lambda cb: pl.BlockSpec((ts, LRU_WIDTH), lambda b, t: (b * nt + t, cb))
    full = lambda a: pl.BlockSpec(a.shape, lambda b, t: (0,) * a.ndim)
    cb2 = conv_b.reshape(1, -1)
    return pl.pallas_call(
        functools.partial(_lru_kernel, ts=ts),
        grid=(bsz, nt),
        in_specs=[col(COL_LX // 512), col(COL_LG // 512), full(conv_w), full(cb2), full(wg2), full(bg), full(sp)],
        out_specs=pl.BlockSpec((ts, LRU_WIDTH), lambda b, t: (b * nt + t, 0)),
        out_shape=jax.ShapeDtypeStruct((bsz * seq, LRU_WIDTH), F32),
        scratch_shapes=[pltpu.VMEM((8 + ts, LRU_WIDTH), F32), pltpu.VMEM((8, LRU_WIDTH), F32)],
        compiler_params=_cparams(("parallel", "arbitrary")),
        name="rglru",
    )(p, p, conv_w, cb2, wg2, bg, sp)


def _layer_norm(x, w, b):
    xc = x - jnp.mean(x, axis=-1, keepdims=True)
    var = jnp.mean(xc * xc, axis=-1, keepdims=True)
    return xc * lax.rsqrt(var + LN_EPS) * w + b


def _merge_kernel(ya_ref, yb_ref, ys_ref, yd_ref, gl_ref, x_ref, wbr_ref, wout_ref, wglu_ref, bglu_ref,
                  bgate_ref, lnw_ref, lnb_ref, rw_ref, rb_ref, h_ref, hb_ref, lg_ref):
    yc = jax.nn.gelu(jnp.concatenate([ys_ref[c] for c in range(S5_WIDTH // 128)], axis=-1))
    yc = yc * _sigmoid(_dot(yc.astype(BF16), wglu_ref[...]) + bglu_ref[...])
    branches = (ya_ref[...].astype(BF16), yb_ref[...], yc.astype(BF16), yd_ref[...].astype(BF16))
    mix = None
    for bi, yb in enumerate(branches):
        cs = slice(bi * D_MODEL, (bi + 1) * D_MODEL)
        gate = _sigmoid(gl_ref[:, cs].astype(F32) + bgate_ref[:, cs])
        term = gate * _dot(yb, wbr_ref[bi])
        mix = term if mix is None else mix + term
    mo = _dot(mix.astype(BF16), wout_ref[...])
    h = _layer_norm(DEEPNORM_ALPHA * x_ref[...] + mo, lnw_ref[...], lnb_ref[...])
    h_ref[...] = h
    h_hi, h_lo = _split(h)
    hb_ref[...] = _pack_pairs(h)
    hcat = jnp.concatenate([h_hi, h_lo, h_hi], axis=-1)
    lg_ref[...] = _dot_nt(rw_ref[...], hcat) + rb_ref[...]


def _merge(ya, yb, ys, yd, gl, x, w_branch, w_out, w_glu, b_glu, b_gate, ln_w, ln_b, router_w, router_b):
    n = x.shape[0]
    tm = min(512, n)
    rwt = router_w.T.astype(F32)
    rw_hi = rwt.astype(BF16)
    rw_lo = (rwt - rw_hi.astype(F32)).astype(BF16)
    rw3 = jnp.concatenate([rw_hi, rw_hi, rw_lo], axis=1)
    row = lambda w: pl.BlockSpec((tm, w), lambda i: (i, 0))
    full = lambda a: pl.BlockSpec(a.shape, lambda i: (0,) * a.ndim)
    consts = [w_branch.astype(BF16), w_out.astype(BF16), w_glu.astype(BF16), b_glu.reshape(1, -1),
              b_gate.reshape(1, -1), ln_w.reshape(1, -1), ln_b.reshape(1, -1), rw3, router_b.reshape(-1, 1)]
    return pl.pallas_call(
        _merge_kernel,
        grid=(n // tm,),
        in_specs=[row(512), row(512), pl.BlockSpec((S5_WIDTH // 128, tm, 128), lambda i: (0, i, 0)), row(512),
                  row(N_BRANCH * D_MODEL), row(D_MODEL)]
                 + [full(c) for c in consts],
        out_specs=[row(D_MODEL), row(D_MODEL // 2), pl.BlockSpec((N_EXPERTS, tm), lambda i: (0, i))],
        out_shape=[jax.ShapeDtypeStruct((n, D_MODEL), F32), jax.ShapeDtypeStruct((n, D_MODEL // 2), jnp.uint32),
                   jax.ShapeDtypeStruct((N_EXPERTS, n), F32)],
        compiler_params=_cparams(("parallel",)),
        name="merge",
    )(ya, yb, ys, yd, gl, x, *consts)


def _route_kernel(lg_ref, tri_ref, idx_ref, w_ref, rank_ref, cnt_ref, carry_ref, *, tr):
    @pl.when(pl.program_id(0) == 0)
    def _():
        carry_ref[...] = jnp.zeros_like(carry_ref)

    l = lg_ref[...]
    eidx = lax.broadcasted_iota(jnp.int32, l.shape, 0)
    vals, hots = [], []
    for _ in range(TOP_K):
        m = jnp.max(l, axis=0, keepdims=True)
        first = jnp.min(jnp.where(l == m, eidx, N_EXPERTS), axis=0, keepdims=True)
        hot = eidx == first
        l = jnp.where(hot, -jnp.inf, l)
        vals.append(m)
        hots.append(hot)
        idx_ref[len(vals) - 1:len(vals), :] = first
    ex = [jnp.exp(v - vals[0]) for v in vals]
    den = ex[0] + ex[1] + ex[2] + ex[3]
    for k in range(TOP_K):
        w_ref[k:k + 1, :] = ex[k] / den
    member = jnp.zeros(l.shape, F32)
    for hot in hots:
        member = member + hot.astype(F32)
    before = _dot(member.astype(BF16), tri_ref[...]) + carry_ref[:, 0:1]
    for k in range(TOP_K):
        rank = jnp.sum(jnp.where(hots[k], before, 0.0), axis=0, keepdims=True)
        rank_ref[k:k + 1, :] = rank.astype(jnp.int32)
    carry_ref[...] = carry_ref[...] + jnp.sum(member, axis=1, keepdims=True)
    cnt_ref[...] = carry_ref[...]
    idx_ref[TOP_K:, :] = jnp.zeros((8 - TOP_K, tr), jnp.int32)
    w_ref[TOP_K:, :] = jnp.zeros((8 - TOP_K, tr), F32)
    rank_ref[TOP_K:, :] = jnp.zeros((8 - TOP_K, tr), jnp.int32)


def _route(logits_t):
    n = logits_t.shape[1]
    tr = min(512, n)
    tri = jnp.asarray(np.triu(np.ones((tr, tr), np.float32), 1), BF16)
    tok = pl.BlockSpec((8, tr), lambda i: (0, i))
    return pl.pallas_call(
        functools.partial(_route_kernel, tr=tr),
        grid=(n // tr,),
        in_specs=[pl.BlockSpec((N_EXPERTS, tr), lambda i: (0, i)), pl.BlockSpec((tr, tr), lambda i: (0, 0))],
        out_specs=[tok, tok, tok, pl.BlockSpec((N_EXPERTS, 128), lambda i: (0, 0))],
        out_shape=[jax.ShapeDtypeStruct((8, n), jnp.int32), jax.ShapeDtypeStruct((8, n), F32),
                   jax.ShapeDtypeStruct((8, n), jnp.int32), jax.ShapeDtypeStruct((N_EXPERTS, 128), F32)],
        scratch_shapes=[pltpu.VMEM((N_EXPERTS, 128), F32)],
        compiler_params=_cparams(("arbitrary",)),
        name="route",
    )(logits_t, tri)


def _expert_kernel(be_ref, rows_ref, x_ref, w1_ref, b1_ref, w2_ref, b2_ref, o_ref, w1b_ref, w2b_ref, *, tm):
    i = pl.program_id(0)
    rows = rows_ref[i]
    half = tm // 2

    @pl.when(jnp.logical_and(rows > 0, jnp.logical_or(i == 0, be_ref[i] != be_ref[jnp.maximum(i - 1, 0)])))
    def _():
        w1b_ref[...] = w1_ref[0].astype(BF16)
        w2b_ref[...] = w2_ref[0].astype(BF16)

    def mlp(n):
        h1 = _dot(_unpack_pairs(x_ref[0:n, :]).astype(BF16), w1b_ref[...]) + b1_ref[0]
        glu = jnp.minimum(h1[:, :EXPERT_FF], SWIGLU_LIMIT)
        lin = jnp.clip(h1[:, EXPERT_FF:], -SWIGLU_LIMIT, SWIGLU_LIMIT)
        act = glu * _sigmoid(SWIGLU_ALPHA * glu) * (lin + 1.0)
        o_ref[0:n, :] = _pack_pairs(_dot(act.astype(BF16), w2b_ref[...]) + b2_ref[0])

    @pl.when(rows > half)
    def _():
        mlp(tm)

    @pl.when(jnp.logical_and(rows > 0, rows <= half))
    def _():
        mlp(half)
        o_ref[half:, :] = jnp.zeros((tm - half, o_ref.shape[1]), o_ref.dtype)

    @pl.when(rows == 0)
    def _():
        o_ref[...] = jnp.zeros_like(o_ref)


def _experts(x_slots, block_expert, block_rows, w1, b1, w2, b2, expert_offset):
    n_slots = x_slots.shape[0]
    tm = EXPERT_TILE
    n_blocks = n_slots // tm
    off = expert_offset
    grid_spec = pltpu.PrefetchScalarGridSpec(
        num_scalar_prefetch=2,
        grid=(n_blocks,),
        in_specs=[pl.BlockSpec((tm, D_MODEL // 2), lambda i, be, br: (i, 0)),
                  pl.BlockSpec((1, D_MODEL, 2 * EXPERT_FF), lambda i, be, br: (be[i] + off, 0, 0)),
                  pl.BlockSpec((1, 1, 2 * EXPERT_FF), lambda i, be, br: (be[i], 0, 0)),
                  pl.BlockSpec((1, EXPERT_FF, D_MODEL), lambda i, be, br: (be[i] + off, 0, 0)),
                  pl.BlockSpec((1, 1, D_MODEL), lambda i, be, br: (be[i], 0, 0))],
        out_specs=pl.BlockSpec((tm, D_MODEL // 2), lambda i, be, br: (i, 0)),
        scratch_shapes=[pltpu.VMEM((D_MODEL, 2 * EXPERT_FF), BF16), pltpu.VMEM((EXPERT_FF, D_MODEL), BF16)],
    )
    return pl.pallas_call(
        functools.partial(_expert_kernel, tm=tm),
        grid_spec=grid_spec,
        out_shape=jax.ShapeDtypeStruct((n_slots, D_MODEL // 2), jnp.uint32),
        compiler_params=_cparams(("arbitrary",)),
        name="experts",
    )(block_expert, block_rows, x_slots, w1, b1.reshape(N_EXPERTS, 1, -1), w2, b2.reshape(N_EXPERTS, 1, -1))


COMBINE_TILE = 512
DISPATCH_TILE = 1024


def _dispatch_kernel(dest_ref, h_ref, init_ref, xs_ref, sem, *, tm):
    del init_ref

    for r in range(tm):
        for k in range(TOP_K):
            pltpu.make_async_copy(h_ref.at[pl.ds(r, 1)], xs_ref.at[pl.ds(dest_ref[k, r], 1)], sem).start(
                priority=k % 2)
    for k in range(TOP_K):
        pltpu.make_async_copy(h_ref, xs_ref.at[pl.ds(0, tm)], sem).wait()


def _dispatch(h_packed, dest, n_slots, slots_init):
    n, w = h_packed.shape
    if slots_init is None:
        slots_init = jnp.zeros((n_slots, w), jnp.uint32)
    tm = min(DISPATCH_TILE, n)
    return pl.pallas_call(
        functools.partial(_dispatch_kernel, tm=tm),
        grid=(n // tm,),
        in_specs=[pl.BlockSpec((TOP_K, tm), lambda i: (0, i), memory_space=pltpu.SMEM),
                  pl.BlockSpec((tm, w), lambda i: (i, 0)), pl.BlockSpec(memory_space=pl.ANY)],
        out_specs=pl.BlockSpec(memory_space=pl.ANY),
        out_shape=jax.ShapeDtypeStruct((n_slots, w), jnp.uint32),
        scratch_shapes=[pltpu.SemaphoreType.DMA(())],
        input_output_aliases={2: 0},
        compiler_params=_cparams(("arbitrary",)),
        name="dispatch",
    )(dest, h_packed, slots_init)


def _combine_kernel(d_ref, h_ref, w_ref, lnw_ref, lnb_ref, ys_ref, o_ref, buf_ref, sem, *, tm):
    i = pl.program_id(0)
    n_tiles = pl.num_programs(0) - 1

    def issue(slot):
        for r in range(tm):
            for k in range(TOP_K):
                pltpu.make_async_copy(ys_ref.at[pl.ds(d_ref[k, r], 1)], buf_ref.at[slot, k, pl.ds(r, 1)],
                                      sem.at[slot]).start(priority=k % 2)

    def finish(slot):
        for k in range(TOP_K):
            pltpu.make_async_copy(ys_ref.at[pl.ds(0, tm)], buf_ref.at[slot, k], sem.at[slot]).wait()
        acc = DEEPNORM_ALPHA * h_ref[...]
        for k in range(TOP_K):
            acc = acc + w_ref[:, k:k + 1] * _unpack_pairs(buf_ref[slot, k])
        o_ref[...] = _layer_norm(acc, lnw_ref[...], lnb_ref[...])

    for parity in range(2):
        @pl.when(jnp.logical_and(i < n_tiles, i % 2 == parity))
        def _():
            issue(parity)

        @pl.when(jnp.logical_and(i > 0, i % 2 == parity))
        def _():
            finish(1 - parity)


def _combine(h, y_slots, dest, w_tok, ln_w, ln_b):
    n = h.shape[0]
    tm = min(COMBINE_TILE, n)
    n_tiles = n // tm
    lnw = ln_w.reshape(1, -1)
    lnb = ln_b.reshape(1, -1)
    prev = lambda i: jnp.maximum(i - 1, 0)
    return pl.pallas_call(
        functools.partial(_combine_kernel, tm=tm),
        grid=(n_tiles + 1,),
        in_specs=[pl.BlockSpec((TOP_K, tm), lambda i: (0, jnp.minimum(i, n_tiles - 1)), memory_space=pltpu.SMEM),
                  pl.BlockSpec((tm, D_MODEL), lambda i: (prev(i), 0)),
                  pl.BlockSpec((tm, TOP_K), lambda i: (prev(i), 0)),
                  pl.BlockSpec(lnw.shape, lambda i: (0, 0)), pl.BlockSpec(lnb.shape, lambda i: (0, 0)),
                  pl.BlockSpec(memory_space=pl.ANY)],
        out_specs=pl.BlockSpec((tm, D_MODEL), lambda i: (prev(i), 0)),
        out_shape=jax.ShapeDtypeStruct((n, D_MODEL), F32),
        scratch_shapes=[pltpu.VMEM((2, TOP_K, tm, D_MODEL // 2), jnp.uint32), pltpu.SemaphoreType.DMA((2,))],
        compiler_params=_cparams(("arbitrary",)),
        name="combine",
    )(dest, h, w_tok, lnw, lnb, y_slots)


def _moe(h, h16, logits_t, w1, b1, w2, b2, ln_w, ln_b, expert_offset, slots_init):
    n = h.shape[0]
    tm = EXPERT_TILE
    idx8, w8, rank8, cnt = _route(logits_t)
    idx, w_top, rank = idx8[:TOP_K], w8[:TOP_K], rank8[:TOP_K]
    counts = cnt[:, 0].astype(jnp.int32)
    padded = ((counts + tm - 1) // tm) * tm
    p_end = jnp.cumsum(padded)
    p_start = p_end - padded
    experts = jnp.arange(N_EXPERTS, dtype=jnp.int32)
    seg_start = jnp.sum(jnp.where(idx[:, :, None] == experts, p_start, 0), axis=-1)
    dest = seg_start + rank
    n_slots = n * TOP_K + N_EXPERTS * tm
    n_blocks = n_slots // tm
    starts = jnp.arange(n_blocks, dtype=jnp.int32) * tm
    block_expert = jnp.minimum(jnp.sum((p_end[None, :] <= starts[:, None]).astype(jnp.int32), axis=1),
                               N_EXPERTS - 1)
    seg_stop = p_start + counts
    block_stop = jnp.sum(jnp.where(block_expert[:, None] == experts, seg_stop, 0), axis=-1)
    block_rows = jnp.clip(block_stop - starts, 0, tm).astype(jnp.int32)
    x_slots = _dispatch(h16, dest, n_slots, slots_init)
    y_slots = _experts(x_slots, block_expert, block_rows, w1, b1, w2, b2, expert_offset)
    return _combine(h, y_slots, dest, w_top.T, ln_w, ln_b), x_slots


def _mixer_weight(w_in):
    o = np.cumsum((512, 512, 512, 512, MLA_Q_RANK, MLA_KV_RANK + MLA_ROPE, S5_WIDTH, LRU_WIDTH, LRU_WIDTH))
    hgrn, cq, ckv = w_in[:, :o[3]], w_in[:, o[3]:o[4]], w_in[:, o[4]:o[4] + MLA_KV_RANK]
    kpe = w_in[:, o[4] + MLA_KV_RANK:o[5]]
    su, lx, lg = w_in[:, o[5]:o[6]], w_in[:, o[6]:o[7]], w_in[:, o[7]:o[8]]
    pad = jnp.zeros((D_MODEL, 128 - MLA_ROPE), w_in.dtype)
    w_mix = jnp.concatenate([hgrn, su, lx, lg, cq, ckv, kpe, pad, _rot_half_cols(kpe), pad], axis=1)
    return w_mix, w_in[:, o[8]:]


def _layer(x, positions, lb, bsz, seq, w_in, b_gate, hgrn_norm_w, mla_q_norm_w, mla_wq_b, mla_kv_norm_w, mla_wkv_b,
           s5_a_re, s5_a_im, s5_log_dt, s5_b_re, s5_b_im, s5_c_re, s5_c_im, s5_d, s5_w_glu, s5_b_glu,
           lru_conv_w, lru_conv_b, lru_wa, lru_ba, lru_wx, lru_bx, lru_a_param,
           w_branch, w_out, ln1_w, ln1_b, ln2_w, ln2_b, router_w, router_b, moe_w1, moe_b1, moe_w2, moe_b2,
           expert_offset=0, slots_init=None):
    w_mix, w_gl = _mixer_weight(w_in)
    p = _matmul(x, w_mix.astype(BF16), 2048, MIX_WIDTH // 3, F32)
    gl = _matmul(x, w_gl.astype(BF16), 2048, 2048, BF16)
    y_a = _hgrn(p, lb, hgrn_norm_w, bsz, seq)
    qn, qp, kn, kp, vt = _mla_proj(p, positions, mla_q_norm_w, mla_wq_b, mla_kv_norm_w, mla_wkv_b, min(256, seq))
    y_b = _flash(qn, qp, kn, kp, vt, bsz, seq)
    y_s = _s5(p, s5_a_re, s5_a_im, s5_log_dt, s5_b_re, s5_b_im, s5_c_re, s5_c_im, s5_d, bsz, seq)
    y_d = _lru(p, lru_conv_w, lru_conv_b, lru_wa, lru_ba, lru_wx, lru_bx, lru_a_param, bsz, seq)
    h, h16, logits_t = _merge(y_a, y_b, y_s, y_d, gl, x, w_branch, w_out, s5_w_glu, s5_b_glu, b_gate,
                              ln1_w, ln1_b, router_w, router_b)
    return _moe(h, h16, logits_t, moe_w1, moe_b1, moe_w2, moe_b2, ln2_w, ln2_b, expert_offset, slots_init)


def kernel(x, positions, w_in, b_gate, hgrn_lb_logits, hgrn_norm_w, mla_q_norm_w, mla_wq_b, mla_kv_norm_w, mla_wkv_b, s5_a_re, s5_a_im, s5_log_dt, s5_b_re, s5_b_im, s5_c_re, s5_c_im, s5_d, s5_w_glu, s5_b_glu, lru_conv_w, lru_conv_b, lru_wa, lru_ba, lru_wx, lru_bx, lru_a_param, w_branch, w_out, ln1_w, ln1_b, ln2_w, ln2_b, router_w, router_b, moe_w1, moe_b1, moe_w2, moe_b2):
    bsz, seq, _ = x.shape
    probs = jax.nn.softmax(hgrn_lb_logits.astype(F32), axis=0)
    lower_bounds = jnp.cumsum(probs, axis=0) - probs[0:1]
    per_layer = (w_in, b_gate, hgrn_norm_w, mla_q_norm_w, mla_wq_b, mla_kv_norm_w, mla_wkv_b,
                 s5_a_re, s5_a_im, s5_log_dt, s5_b_re, s5_b_im, s5_c_re, s5_c_im, s5_d, s5_w_glu, s5_b_glu,
                 lru_conv_w, lru_conv_b, lru_wa, lru_ba, lru_wx, lru_bx, lru_a_param,
                 w_branch, w_out, ln1_w, ln1_b, ln2_w, ln2_b, router_w, router_b, moe_w1, moe_b1, moe_w2, moe_b2)
    xf = x.reshape(bsz * seq, D_MODEL)
    w1_all = moe_w1.reshape((DEPTH * N_EXPERTS,) + moe_w1.shape[2:])
    w2_all = moe_w2.reshape((DEPTH * N_EXPERTS,) + moe_w2.shape[2:])
    slots = None
    for l in range(DEPTH):
        args = [a[l] for a in per_layer]
        args[-4], args[-2] = w1_all, w2_all
        xf, slots = _layer(xf, positions, lower_bounds[l], bsz, seq, *args, expert_offset=l * N_EXPERTS,
                           slots_init=slots)
    return xf.reshape(bsz, seq, D_MODEL)
```
